```python
import math
import jax, jax.numpy as jnp
from jax import lax
import numpy as np

D_MODEL = 2048
BATCH = 4
SEQ = 8192
DEPTH = 4
DEC_BATCH = 1
DEC_SEQ = 8192
PAST_LEN = 128

GRID_W = 64
N_MEM = 256
XA_HEADS = 4
XA_HEAD_DIM = D_MODEL // XA_HEADS
D_SSD = D_MODEL
SSD_HEAD_DIM = 64
SSD_HEADS = D_SSD // SSD_HEAD_DIM
SSD_GROUPS = 4
SSD_STATE = 128
SSD_CONV_W = 5
SSD_CHUNK = 128
SSD_CONV_CH = D_SSD + 2 * SSD_GROUPS * SSD_STATE
ATTN_HEAD_DIM = 128
ATTN_HEADS = D_MODEL // ATTN_HEAD_DIM
ATTN_KV_HEADS = 4
ATTN_Q_GROUP = ATTN_HEADS // ATTN_KV_HEADS
D_ATTN = ATTN_HEADS * ATTN_HEAD_DIM
D_KV = ATTN_KV_HEADS * ATTN_HEAD_DIM
ROPE_THETA = 10000.0
Q_BLOCK = 128
EVEN_IN_W = D_SSD + SSD_CONV_CH + 2 * SSD_HEADS + D_ATTN + 2 * D_KV
EVEN_OUT_W = D_SSD + D_ATTN
HY_SHORT_W = 3
HY_EMB = 33
HY_BANDS = (HY_EMB - 1) // 2
HY_FILTER_W = 64
HY_TARGET = 1e-2
HY_FAST_PCT = 0.3
HY_SLOW_PCT = 1.5
D_FF = 5632
FFN_CONV_W = 3
N_EVEN = (DEPTH + 1) // 2
N_ODD = DEPTH // 2
EPS = 1e-6

kernel_name = "hybrid_ssd_gqa_hyena_encoder"

F32 = jnp.float32


def rms_norm(x, g):
    xf = x.astype(F32)
    y = xf * lax.rsqrt(jnp.mean(xf * xf, axis=-1, keepdims=True) + EPS)
    return (y * g.astype(F32)).astype(x.dtype)


def dwconv_centred(x, w, b):
    width = w.shape[0]
    half = width // 2
    seq = x.shape[1]
    xp = jnp.pad(x, ((0, 0), (half, half), (0, 0)))
    out = xp[:, 0:seq] * w[0]
    for k in range(1, width):
        out = out + xp[:, k:k + seq] * w[k]
    return out + b


def axial_rope_tables(seq):
    rows = seq // GRID_W
    row = jnp.repeat(jnp.arange(rows), GRID_W).astype(F32)
    col = jnp.tile(jnp.arange(GRID_W), rows).astype(F32)
    axis_dim = ATTN_HEAD_DIM // 2
    inv_freq = ROPE_THETA ** (-jnp.arange(0, axis_dim, 2, dtype=F32) / axis_dim)
    ang = jnp.concatenate([row[:, None] * inv_freq, col[:, None] * inv_freq], axis=-1)
    return jnp.cos(ang), jnp.sin(ang)


def apply_rope(x, cos, sin):
    xf = x.astype(F32).reshape(x.shape[:-1] + (-1, 2))
    xe, xo = xf[..., 0], xf[..., 1]
    c = cos[None, :, None, :]
    s = sin[None, :, None, :]
    out = jnp.stack([xe * c - xo * s, xe * s + xo * c], axis=-1)
    return out.reshape(x.shape).astype(x.dtype)


def hyena_pos_features(seq):
    t = jnp.linspace(0.0, 1.0, seq, dtype=F32)
    w = 2.0 * math.pi * jnp.arange(seq, dtype=F32) / seq
    f = jnp.linspace(1e-4, HY_BANDS - 1, HY_BANDS, dtype=F32)
    fw = w[:, None] * f[None, :]
    z = jnp.concatenate([t[:, None], jnp.cos(fw), -jnp.sin(fw)], axis=-1)
    deltas = jnp.abs(jnp.linspace(math.log(HY_TARGET) / HY_SLOW_PCT,
                                  math.log(HY_TARGET) / HY_FAST_PCT, D_MODEL, dtype=F32))
    window = jnp.exp(-t[:, None] * deltas[None, :])
    return z, window


def segsum_exp(cs):
    T = cs.shape[-1]
    diff = cs[..., :, None] - cs[..., None, :]
    mask = jnp.tril(jnp.ones((T, T), dtype=bool))
    return jnp.exp(jnp.where(mask, diff, -jnp.inf))


def ssd_scan(x, dt, a, bm, cm):
    bsz, seq = x.shape[:2]
    nc = seq // SSD_CHUNK
    R = SSD_HEADS // SSD_GROUPS
    dt = dt.astype(F32)
    xc = (x.astype(F32) * dt[..., None]).reshape(bsz, nc, SSD_CHUNK, SSD_GROUPS, R, SSD_HEAD_DIM)
    la = (dt * a.astype(F32)).reshape(bsz, nc, SSD_CHUNK, SSD_GROUPS, R).transpose(0, 3, 4, 1, 2)
    bc = bm.astype(F32).reshape(bsz, nc, SSD_CHUNK, SSD_GROUPS, SSD_STATE)
    cc = cm.astype(F32).reshape(bsz, nc, SSD_CHUNK, SSD_GROUPS, SSD_STATE)
    cs = jnp.cumsum(la, axis=-1)
    cb = jnp.einsum('bclgn,bcsgn->bgcls', cc, bc)
    wgt = cb[:, :, None] * segsum_exp(cs)
    y_diag = jnp.einsum('bgrcls,bcsgrp->bclgrp', wgt, xc)
    ds = jnp.exp(cs[..., -1:] - cs).transpose(0, 3, 4, 1, 2)
    states = jnp.einsum('bcsgn,bcsgrp->bcgrpn', bc, xc * ds[..., None])
    chunk_tot = jnp.pad(cs[..., -1], ((0, 0), (0, 0), (0, 0), (1, 0)))
    decay_chunk = segsum_exp(jnp.cumsum(chunk_tot, axis=-1))
    states = jnp.concatenate([jnp.zeros_like(states[:, :1]), states], axis=1)
    states_in = jnp.einsum('bgrzc,bcgrpn->bzgrpn', decay_chunk, states)[:, :-1]
    y_off = jnp.einsum('bclgn,bcgrpn->bclgrp', cc, states_in) * jnp.exp(cs).transpose(0, 3, 4, 1, 2)[..., None]
    return (y_diag + y_off).reshape(bsz, seq, SSD_HEADS, SSD_HEAD_DIM)


def block_attention(q, k, v):
    bsz, seq = q.shape[:2]
    nb = seq // Q_BLOCK
    scale = ATTN_HEAD_DIM ** -0.5
    qb = jnp.moveaxis(q.reshape(bsz, nb, Q_BLOCK, ATTN_KV_HEADS, ATTN_Q_GROUP, ATTN_HEAD_DIM), 1, 0)

    def one_block(qblk):
        s = jnp.einsum('bqkgd,bskd->bkgqs', qblk, k).astype(F32) * scale
        p = jax.nn.softmax(s, axis=-1).astype(v.dtype)
        return jnp.einsum('bkgqs,bskd->bqkgd', p, v)

    ob = lax.map(one_block, qb)
    return jnp.moveaxis(ob, 0, 1).reshape(bsz, seq, D_ATTN)


def ssd_attn_mixer(h, rope, w_in, w_out, conv_w, conv_b, a_log, dt_bias, d_skip, ssd_norm, q_norm, k_norm):
    bsz, seq = h.shape[:2]
    cos, sin = rope
    proj = h @ w_in
    o1 = D_SSD
    o2 = o1 + SSD_CONV_CH
    o3 = o2 + 2 * SSD_HEADS
    o4 = o3 + D_ATTN
    o5 = o4 + D_KV
    z, xbc, dt, q, k, v = jnp.split(proj, [o1, o2, o3, o4, o5], axis=-1)
    xbc = jax.nn.silu(dwconv_centred(xbc, conv_w, conv_b))
    xs, bm, cm = jnp.split(xbc, [D_SSD, D_SSD + SSD_GROUPS * SSD_STATE], axis=-1)
    xs = xs.reshape(bsz, seq, SSD_HEADS, SSD_HEAD_DIM)
    bm = bm.reshape(bsz, seq, SSD_GROUPS, SSD_STATE)
    cm = cm.reshape(bsz, seq, SSD_GROUPS, SSD_STATE)
    dt_f = jax.nn.softplus(dt[..., :SSD_HEADS] + dt_bias[0])
    dt_b = jax.nn.softplus(dt[..., SSD_HEADS:] + dt_bias[1])
    a_f = -jnp.exp(a_log[0].astype(F32))
    a_b = -jnp.exp(a_log[1].astype(F32))
    flip = lambda t: jnp.flip(t, axis=1)
    y = (ssd_scan(xs, dt_f, a_f, bm, cm)
         + flip(ssd_scan(flip(xs), flip(dt_b), a_b, flip(bm), flip(cm)))
         + xs.astype(F32) * d_skip.astype(F32)[:, None])
    y = y.reshape(bsz, seq, D_SSD).astype(h.dtype)
    y_ssd = rms_norm(y * jax.nn.silu(z), ssd_norm)
    q = apply_rope(rms_norm(q.reshape(bsz, seq, ATTN_HEADS, ATTN_HEAD_DIM), q_norm), cos, sin)
    k = apply_rope(rms_norm(k.reshape(bsz, seq, ATTN_KV_HEADS, ATTN_HEAD_DIM), k_norm), cos, sin)
    v = v.reshape(bsz, seq, ATTN_KV_HEADS, ATTN_HEAD_DIM)
    q = q.reshape(bsz, seq, ATTN_KV_HEADS, ATTN_Q_GROUP, ATTN_HEAD_DIM)
    y_attn = block_attention(q, k, v)
    return jnp.concatenate([y_ssd, y_attn], axis=-1) @ w_out


def hyena_kernel(z, window, w1, b1, w2, b2, w3, b3, freq, w_out):
    h = jnp.sin(freq * (z @ w1 + b1))
    h = jnp.sin(freq * (h @ w2 + b2))
    h = jnp.sin(freq * (h @ w3 + b3))
    h = (h @ w_out).astype(F32)
    h_fwd = h[:, :D_MODEL] * window
    h_bwd = h[:, D_MODEL:] * window
    kern = jnp.concatenate([h_fwd, jnp.zeros((1, D_MODEL), F32), h_bwd[:0:-1]], axis=0)
    return kern / jnp.sum(jnp.abs(kern), axis=0, keepdims=True)


def bidir_long_conv(u, kern, skip):
    seq = u.shape[1]
    n = 2 * seq
    uf = jnp.fft.rfft(u.astype(F32), n=n, axis=1)
    kf = jnp.fft.rfft(kern, n=n, axis=0)
    y = jnp.fft.irfft(uf * kf[None], n=n, axis=1)[:, :seq]
    return (y + u.astype(F32) * skip.astype(F32)).astype(u.dtype)


def hyena_mixer(h, hz, window, w_in, conv_w, conv_b, f_w1, f_b1, f_w2, f_b2, f_w3, f_b3, f_freq, f_w_out, skip, w_out):
    u = dwconv_centred(h @ w_in, conv_w, conv_b)
    x0, x1, v = jnp.split(u, 3, axis=-1)
    kern = hyena_kernel(hz, window, f_w1, f_b1, f_w2, f_b2, f_w3, f_b3, f_freq, f_w_out)
    y = x0 * bidir_long_conv(v * x1, kern, skip)
    return y @ w_out


def memory_cross_attention(h, mem_n, wq, wk, wv, wo):
    bsz, seq = h.shape[:2]
    n_mem = mem_n.shape[1]
    q = (h @ wq).reshape(bsz, seq, XA_HEADS, XA_HEAD_DIM)
    k = (mem_n @ wk).reshape(bsz, n_mem, XA_HEADS, XA_HEAD_DIM)
    v = (mem_n @ wv).reshape(bsz, n_mem, XA_HEADS, XA_HEAD_DIM)
    s = jnp.einsum('blhd,bmhd->bhlm', q, k).astype(F32) * (XA_HEAD_DIM ** -0.5)
    p = jax.nn.softmax(s, axis=-1).astype(v.dtype)
    o = jnp.einsum('bhlm,bmhd->blhd', p, v).reshape(bsz, seq, D_MODEL)
    return o @ wo


def conv_ffn(h, w_in, conv_w, conv_b, w_out):
    u = dwconv_centred(h @ w_in, conv_w, conv_b)
    g, up = jnp.split(u, 2, axis=-1)
    return (jax.nn.silu(g) * up) @ w_out


def trunk(x, mem, prm):
    seq = x.shape[1]
    rope = axial_rope_tables(seq)
    hz, window = hyena_pos_features(seq)
    for i in range(DEPTH):
        h = rms_norm(x, prm['norm_mix'][i])
        if i % 2 == 0:
            e = i // 2
            x = x + ssd_attn_mixer(h, rope, prm['mix_w_in'][e], prm['mix_w_out'][e],
                                   prm['ssd_conv_w'][e], prm['ssd_conv_b'][e], prm['ssd_a_log'][e],
                                   prm['ssd_dt_bias'][e], prm['ssd_d'][e], prm['ssd_norm'][e],
                                   prm['attn_q_norm'][e], prm['attn_k_norm'][e])
        else:
            o = i // 2
            x = x + hyena_mixer(h, hz, window, prm['hy_w_in'][o], prm['hy_conv_w'][o], prm['hy_conv_b'][o],
                                prm['hy_f_w1'][o], prm['hy_f_b1'][o], prm['hy_f_w2'][o], prm['hy_f_b2'][o],
                                prm['hy_f_w3'][o], prm['hy_f_b3'][o], prm['hy_f_freq'][o], prm['hy_f_w_out'][o],
                                prm['hy_skip'][o], prm['hy_w_out'][o])
        x = x + memory_cross_attention(rms_norm(x, prm['norm_xa'][i]), rms_norm(mem, prm['norm_mem'][i]),
                                       prm['xa_wq'][i], prm['xa_wk'][i], prm['xa_wv'][i], prm['xa_wo'][i])
        x = x + conv_ffn(rms_norm(x, prm['norm_ffn'][i]), prm['ffn_w_in'][i], prm['ffn_conv_w'][i],
                         prm['ffn_conv_b'][i], prm['ffn_w_out'][i])
    return rms_norm(x, prm['final_norm'])


def setup_inputs(seed: int = 0) -> dict:
    key = jax.random.key(seed)
    ks = iter(jax.random.split(key, 64))

    def w(shape, fan_in):
        return jax.random.normal(next(ks), shape, F32) * (fan_in ** -0.5)

    def gain(shape):
        return 1.0 + 0.02 * jax.random.normal(next(ks), shape, F32)

    def small(shape):
        return 0.01 * jax.random.normal(next(ks), shape, F32)

    F2 = 2 * D_FF
    dt0 = jnp.exp(jax.random.uniform(next(ks), (N_EVEN, 2, SSD_HEADS), F32, math.log(1e-3), math.log(1e-1)))
    dt_bias = dt0 + jnp.log(-jnp.expm1(-dt0))
    a_log = jnp.log(jax.random.uniform(next(ks), (N_EVEN, 2, SSD_HEADS), F32, 1.0, 16.0))
    return {
        'x_prompt': jax.random.normal(next(ks), (BATCH, SEQ, D_MODEL), F32),
        'x_sample': jax.random.normal(next(ks), (DEC_BATCH, DEC_SEQ, D_MODEL), F32),
        'mem_prompt': jax.random.normal(next(ks), (BATCH, N_MEM, D_MODEL), F32),
        'mem_sample': jax.random.normal(next(ks), (DEC_BATCH, N_MEM, D_MODEL), F32),
        'norm_mix': gain((DEPTH, D_MODEL)),
        'norm_xa': gain((DEPTH, D_MODEL)),
        'norm_mem': gain((DEPTH, D_MODEL)),
        'norm_ffn': gain((DEPTH, D_MODEL)),
        'xa_wq': w((DEPTH, D_MODEL, D_MODEL), D_MODEL),
        'xa_wk': w((DEPTH, D_MODEL, D_MODEL), D_MODEL),
        'xa_wv': w((DEPTH, D_MODEL, D_MODEL), D_MODEL),
        'xa_wo': w((DEPTH, D_MODEL, D_MODEL), D_MODEL),
        'ffn_w_in': w((DEPTH, D_MODEL, F2), D_MODEL),
        'ffn_conv_w': w((DEPTH, FFN_CONV_W, F2), FFN_CONV_W),
        'ffn_conv_b': small((DEPTH, F2)),
        'ffn_w_out': w((DEPTH, D_FF, D_MODEL), D_FF),
        'mix_w_in': w((N_EVEN, D_MODEL, EVEN_IN_W), D_MODEL),
        'mix_w_out': w((N_EVEN, EVEN_OUT_W, D_MODEL), EVEN_OUT_W),
        'ssd_conv_w': w((N_EVEN, SSD_CONV_W, SSD_CONV_CH), SSD_CONV_W),
        'ssd_conv_b': small((N_EVEN, SSD_CONV_CH)),
        'ssd_a_log': a_log,
        'ssd_dt_bias': dt_bias,
        'ssd_d': gain((N_EVEN, SSD_HEADS)),
        'ssd_norm': gain((N_EVEN, D_SSD)),
        'attn_q_norm': gain((N_EVEN, ATTN_HEAD_DIM)),
        'attn_k_norm': gain((N_EVEN, ATTN_HEAD_DIM)),
        'hy_w_in': w((N_ODD, D_MODEL, 3 * D_MODEL), D_MODEL),
        'hy_conv_w': w((N_ODD, HY_SHORT_W, 3 * D_MODEL), HY_SHORT_W),
        'hy_conv_b': small((N_ODD, 3 * D_MODEL)),
        'hy_f_w1': w((N_ODD, HY_EMB, HY_FILTER_W), HY_EMB),
        'hy_f_b1': small((N_ODD, HY_FILTER_W)),
        'hy_f_w2': w((N_ODD, HY_FILTER_W, HY_FILTER_W), HY_FILTER_W),
        'hy_f_b2': small((N_ODD, HY_FILTER_W)),
        'hy_f_w3': w((N_ODD, HY_FILTER_W, HY_FILTER_W), HY_FILTER_W),
        'hy_f_b3': small((N_ODD, HY_FILTER_W)),
        'hy_f_freq': gain((N_ODD, HY_FILTER_W)),
        'hy_f_w_out': w((N_ODD, HY_FILTER_W, 2 * D_MODEL), HY_FILTER_W),
        'hy_skip': jax.random.normal(next(ks), (N_ODD, D_MODEL), F32),
        'hy_w_out': w((N_ODD, D_MODEL, D_MODEL), D_MODEL),
        'final_norm': gain((D_MODEL,)),
    }


def reference(x_prompt, x_sample, mem_prompt, mem_sample, norm_mix, norm_xa, norm_mem, norm_ffn,
              xa_wq, xa_wk, xa_wv, xa_wo, ffn_w_in, ffn_conv_w, ffn_conv_b, ffn_w_out,
              mix_w_in, mix_w_out, ssd_conv_w, ssd_conv_b, ssd_a_log, ssd_dt_bias, ssd_d, ssd_norm,
              attn_q_norm, attn_k_norm, hy_w_in, hy_conv_w, hy_conv_b, hy_f_w1, hy_f_b1, hy_f_w2, hy_f_b2,
              hy_f_w3, hy_f_b3, hy_f_freq, hy_f_w_out, hy_skip, hy_w_out, final_norm):
    prm = dict(norm_mix=norm_mix, norm_xa=norm_xa, norm_mem=norm_mem, norm_ffn=norm_ffn,
               xa_wq=xa_wq, xa_wk=xa_wk, xa_wv=xa_wv, xa_wo=xa_wo,
               ffn_w_in=ffn_w_in, ffn_conv_w=ffn_conv_w, ffn_conv_b=ffn_conv_b, ffn_w_out=ffn_w_out,
               mix_w_in=mix_w_in, mix_w_out=mix_w_out, ssd_conv_w=ssd_conv_w, ssd_conv_b=ssd_conv_b,
               ssd_a_log=ssd_a_log, ssd_dt_bias=ssd_dt_bias, ssd_d=ssd_d, ssd_norm=ssd_norm,
               attn_q_norm=attn_q_norm, attn_k_norm=attn_k_norm,
               hy_w_in=hy_w_in, hy_conv_w=hy_conv_w, hy_conv_b=hy_conv_b,
               hy_f_w1=hy_f_w1, hy_f_b1=hy_f_b1, hy_f_w2=hy_f_w2, hy_f_b2=hy_f_b2,
               hy_f_w3=hy_f_w3, hy_f_b3=hy_f_b3, hy_f_freq=hy_f_freq, hy_f_w_out=hy_f_w_out,
               hy_skip=hy_skip, hy_w_out=hy_w_out, final_norm=final_norm)
    y_prompt = trunk(x_prompt, mem_prompt, prm)
    y_sample = trunk(x_sample, mem_sample, prm)
    return (y_prompt, y_sample)
```

```python
import functools
import math

import numpy as np
import jax
import jax.numpy as jnp
from jax import lax
from jax.experimental import pallas as pl
from jax.experimental.pallas import tpu as pltpu

F32 = jnp.float32
BF16 = jnp.bfloat16
EPS = 1e-6

GRID_W = 64
XA_HEADS = 4
SSD_HEAD_DIM = 64
SSD_GROUPS = 4
SSD_STATE = 128
SSD_CHUNK = 128
ATTN_HEAD_DIM = 128
ATTN_KV_HEADS = 4
ROPE_THETA = 10000.0
HY_EMB = 33
HY_BANDS = (HY_EMB - 1) // 2
HY_TARGET = 1e-2
HY_FAST_PCT = 0.3
HY_SLOW_PCT = 1.5

LANES = 128
DFT_N2 = 128
VMEM_LIMIT = 52 * 1024 * 1024
NEG_BIG = -1e30


def _cparams(*sem):
    return pltpu.CompilerParams(dimension_semantics=sem, vmem_limit_bytes=VMEM_LIMIT)


def _tile(dim, pref):
    t = min(dim, pref)
    while dim % t:
        t //= 2
    return t


def _split3(x):
    hi = x.astype(BF16)
    r1 = x - hi.astype(F32)
    mid = r1.astype(BF16)
    lo = (r1 - mid.astype(F32)).astype(BF16)
    return hi, mid, lo


def _dot(a, b):
    return jnp.dot(a, b, preferred_element_type=F32)


def _dot_exact_rhs(x, e):
    hi, mid, lo = _split3(x)
    return _dot(hi, e) + _dot(mid, e) + _dot(lo, e)


def _dot_exact_lhs(e, x):
    hi, mid, lo = _split3(x)
    return _dot(e, hi) + _dot(e, mid) + _dot(e, lo)


def _dot_f32(a, b):
    ah, am, _ = _split3(a)
    bh, bm, _ = _split3(b)
    return _dot(ah, bh) + _dot(ah, bm) + _dot(am, bh)


def _silu(x):
    return x * (1.0 / (1.0 + jnp.exp(-x)))


def _normmm_kernel(x_ref, g_ref, w_ref, o_ref, xn_ref):
    @pl.when(pl.program_id(1) == 0)
    def _():
        x = x_ref[...].astype(F32)
        ms = jnp.mean(x * x, axis=-1, keepdims=True)
        xn_ref[...] = (x * lax.rsqrt(ms + EPS) * g_ref[...]).astype(BF16)

    o_ref[...] = _dot(xn_ref[...], w_ref[...]).astype(o_ref.dtype)


def normmm(x, g, w, out_dtype=BF16, tm=1024, tn=1024):
    M, K = x.shape
    N = w.shape[1]
    tm = _tile(M, tm)
    tn = _tile(N, tn)
    return pl.pallas_call(
        _normmm_kernel,
        grid=(M // tm, N // tn),
        in_specs=[pl.BlockSpec((tm, K), lambda i, j: (i, 0)),
                  pl.BlockSpec((1, K), lambda i, j: (0, 0)),
                  pl.BlockSpec((K, tn), lambda i, j: (0, j))],
        out_specs=pl.BlockSpec((tm, tn), lambda i, j: (i, j)),
        out_shape=jax.ShapeDtypeStruct((M, N), out_dtype),
        scratch_shapes=[pltpu.VMEM((tm, K), BF16)],
        compiler_params=_cparams("parallel", "arbitrary"),
        name="normmm",
    )(x, g.reshape(1, K).astype(F32), w)


def _mmres_kernel(a_ref, w_ref, r_ref, o_ref):
    o_ref[...] = r_ref[...] + _dot(a_ref[...], w_ref[...])


def mm_res(a, w, res, tm=1024):
    M, K = a.shape
    N = w.shape[1]
    tm = _tile(M, tm)
    tn = _tile(N, 1024 if K <= 2048 else 512)
    return pl.pallas_call(
        _mmres_kernel,
        grid=(M // tm, N // tn),
        in_specs=[pl.BlockSpec((tm, K), lambda i, j: (i, 0)),
                  pl.BlockSpec((K, tn), lambda i, j: (0, j)),
                  pl.BlockSpec((tm, tn), lambda i, j: (i, j))],
        out_specs=pl.BlockSpec((tm, tn), lambda i, j: (i, j)),
        out_shape=jax.ShapeDtypeStruct((M, N), F32),
        compiler_params=_cparams("parallel", "arbitrary"),
        name="mm_res",
    )(a, w, res)


def _rmsnorm_kernel(x_ref, g_ref, o_ref):
    x = x_ref[...]
    ms = jnp.mean(x * x, axis=-1, keepdims=True)
    o_ref[...] = x * lax.rsqrt(ms + EPS) * g_ref[...]


def rmsnorm(x, g, tm=512):
    M, K = x.shape
    tm = _tile(M, tm)
    return pl.pallas_call(
        _rmsnorm_kernel,
        grid=(M // tm,),
        in_specs=[pl.BlockSpec((tm, K), lambda i: (i, 0)),
                  pl.BlockSpec((1, K), lambda i: (0, 0))],
        out_specs=pl.BlockSpec((tm, K), lambda i: (i, 0)),
        out_shape=jax.ShapeDtypeStruct((M, K), F32),
        compiler_params=_cparams("parallel"),
        name="final_norm",
    )(x, g.reshape(1, K).astype(F32))


CONV_HALO = 16


def _dwconv_kernel(*refs, nseg, width, tq, seq, epilogue, nout):
    segs = [refs[5 * s:5 * s + 5] for s in range(nseg)]
    outs = refs[5 * nseg:5 * nseg + nout]
    ext = refs[5 * nseg + nout]
    i = pl.program_id(1)
    row0 = i * tq
    at_start = (row0 % seq) == 0
    at_end = ((row0 + tq) % seq) == 0
    half = width // 2
    h = CONV_HALO
    vals = []
    for main, prev, nxt, w, b in segs:
        ext[0:h, :] = jnp.where(at_start, 0.0, prev[...].astype(F32))
        ext[h:h + tq, :] = main[...].astype(F32)
        ext[h + tq:h + tq + h, :] = jnp.where(at_end, 0.0, nxt[...].astype(F32))
        acc = None
        for k in range(width):
            term = ext[h - half + k:h - half + k + tq, :] * w[k:k + 1, :]
            acc = term if acc is None else acc + term
        vals.append(acc + b[...])
    res = epilogue(*vals)
    for o, r in zip(outs, res):
        o[...] = r.astype(o.dtype)


def dwconv(u, w, b, seg_cols, width_cols, epilogue, nout, seq, tq=512, tc=512):
    T, _ = u.shape
    width = w.shape[0]
    tq = _tile(seq, tq)
    tc = _tile(width_cols, tc)
    h = CONV_HALO
    nrb = T // h
    b2 = b.reshape(1, -1).astype(F32)
    w = w.astype(F32)
    in_specs, args = [], []
    for c0 in seg_cols:
        off = c0 // tc
        in_specs += [
            pl.BlockSpec((tq, tc), lambda j, i, off=off: (i, j + off)),
            pl.BlockSpec((h, tc), lambda j, i, off=off: (jnp.maximum(i * (tq // h) - 1, 0), j + off)),
            pl.BlockSpec((h, tc), lambda j, i, off=off: (jnp.minimum((i + 1) * (tq // h), nrb - 1), j + off)),
            pl.BlockSpec((width, tc), lambda j, i, off=off: (0, j + off)),
            pl.BlockSpec((1, tc), lambda j, i, off=off: (0, j + off)),
        ]
        args += [u, u, u, w, b2]
    kern = functools.partial(_dwconv_kernel, nseg=len(seg_cols), width=width, tq=tq, seq=seq,
                             epilogue=epilogue, nout=nout)
    outs = pl.pallas_call(
        kern,
        grid=(width_cols // tc, T // tq),
        in_specs=in_specs,
        out_specs=[pl.BlockSpec((tq, tc), lambda j, i: (i, j)) for _ in range(nout)],
        out_shape=[jax.ShapeDtypeStruct((T, width_cols), BF16) for _ in range(nout)],
        scratch_shapes=[pltpu.VMEM((tq + 2 * h, tc), F32)],
        compiler_params=_cparams("parallel", "arbitrary"),
        name="dwconv",
    )(*args)
    return outs


def _epi_silu(c):
    return (_silu(c),)


def _epi_glu(g, up):
    return (_silu(g) * up,)


def _epi_hyena(x0, x1, v):
    return (x0, v * x1)


def _softplus(x):
    return jnp.maximum(x, 0.0) + jnp.log(1.0 + jnp.exp(-jnp.abs(x)))


def _ssd_kernel(*refs, rev, nheads):
    if rev:
        (xs_ref, b_ref, c_ref, dt_ref, bias_ref, alog_ref, e_ref,
         yf_ref, z_ref, gain_ref, o_ref, s_ref, y_ref) = refs
    else:
        (xs_ref, b_ref, c_ref, dt_ref, bias_ref, alog_ref, e_ref,
         dskip_ref, o_ref, s_ref) = refs
        y_ref = o_ref
    Q = SSD_CHUNK
    P = SSD_HEAD_DIM
    hpg = nheads // SSD_GROUPS
    gw = hpg * P
    hoff = nheads if rev else 0

    @pl.when(pl.program_id(1) == 0)
    def _():
        s_ref[...] = jnp.zeros_like(s_ref)

    row = lax.broadcasted_iota(jnp.int32, (Q, Q), 0)
    col = lax.broadcasted_iota(jnp.int32, (Q, Q), 1)
    mask = (col >= row) if rev else (col <= row)
    tri = jnp.where(mask, 1.0, 0.0).astype(BF16)

    dtv = _softplus(dt_ref[...] + bias_ref[...])
    a_row = -jnp.exp(alog_ref[...])
    la = dtv * a_row
    cs = _dot_exact_lhs(tri, la)
    tot = cs[0:1, :] if rev else cs[Q - 1:Q, :]
    cs_t = cs.T
    dt_t = dtv.T
    e = e_ref[...]
    carry_in = _dot_exact_rhs(jnp.exp(cs), e)
    to_end = _dot_exact_rhs(jnp.exp(tot - cs) * dtv, e)
    dec = carry_in[0:1, :] if rev else carry_in[Q - 1:Q, :]

    xs = xs_ref[...]
    x_state = (xs.astype(F32) * to_end).astype(BF16)
    lane = lax.broadcasted_iota(jnp.int32, (Q, LANES), 1)
    low = lane < P

    for g in range(SSD_GROUPS):
        bg = b_ref[:, g * SSD_STATE:(g + 1) * SSD_STATE]
        cg = c_ref[:, g * SSD_STATE:(g + 1) * SSD_STATE]
        cb = lax.dot_general(cg, bg, (((1,), (1,)), ((), ())), preferred_element_type=F32)
        s_old = s_ref[g]
        y_off = _dot(cg, s_old.astype(BF16)) * carry_in[:, g * gw:(g + 1) * gw]
        s_ref[g] = s_old * dec[:, g * gw:(g + 1) * gw] + lax.dot_general(
            bg, x_state[:, g * gw:(g + 1) * gw], (((0,), (0,)), ((), ())), preferred_element_type=F32)
        for j in range(hpg // 2):
            ws = []
            for hh in range(2):
                hc = hoff + g * hpg + 2 * j + hh
                diff = cs[:, hc:hc + 1] - cs_t[hc:hc + 1, :]
                decay = jnp.exp(jnp.where(mask, diff, NEG_BIG))
                ws.append((cb * decay * dt_t[hc:hc + 1, :]).astype(BF16))
            c0 = g * gw + 2 * j * P
            xp = xs[:, c0:c0 + LANES]
            rhs = jnp.concatenate([jnp.where(low, xp, jnp.zeros_like(xp)),
                                   jnp.where(low, jnp.zeros_like(xp), xp)], axis=0)
            y = _dot(jnp.concatenate(ws, axis=1), rhs) + y_off[:, 2 * j * P:2 * j * P + LANES]
            if not rev:
                y = y + xp.astype(F32) * dskip_ref[:, c0:c0 + LANES]
            y_ref[:, c0:c0 + LANES] = y

    if rev:
        y = y_ref[...] + yf_ref[...]
        gated = y * _silu(z_ref[...].astype(F32))
        ms = jnp.mean(gated * gated, axis=-1, keepdims=True)
        o_ref[...] = (gated * lax.rsqrt(ms + EPS) * gain_ref[...]).astype(o_ref.dtype)


def ssd_scan(xbc, dtraw, bias_row, alog_row, d_row, z_src, gain_row, nb, seq):
    T = xbc.shape[0]
    Q = SSD_CHUNK
    nc = seq // Q
    gn = SSD_GROUPS * SSD_STATE
    hp = xbc.shape[1] - 2 * gn
    nheads = hp // SSD_HEAD_DIM
    hpg = nheads // SSD_GROUPS
    gw = hpg * SSD_HEAD_DIM
    assert hp % gn == 0 and 2 * nheads <= LANES

    def e_mat(off):
        r = np.arange(LANES)[:, None]
        c = np.arange(hp)[None, :]
        return jnp.asarray((r == off + c // SSD_HEAD_DIM).astype(np.float32), dtype=BF16)

    def specs(rev):
        def blk(c):
            return (nc - 1 - c) if rev else c
        return [
            pl.BlockSpec((Q, hp), lambda b, c: (b * nc + blk(c), 0)),
            pl.BlockSpec((Q, gn), lambda b, c: (b * nc + blk(c), hp // gn)),
            pl.BlockSpec((Q, gn), lambda b, c: (b * nc + blk(c), hp // gn + 1)),
            pl.BlockSpec((Q, LANES), lambda b, c: (b * nc + blk(c), 0)),
            pl.BlockSpec((1, LANES), lambda b, c: (0, 0)),
            pl.BlockSpec((1, LANES), lambda b, c: (0, 0)),
            pl.BlockSpec((LANES, hp), lambda b, c: (0, 0)),
        ], (lambda b, c: (b * nc + blk(c), 0))

    in_f, omap_f = specs(False)
    yf = pl.pallas_call(
        functools.partial(_ssd_kernel, rev=False, nheads=nheads),
        grid=(nb, nc),
        in_specs=in_f + [pl.BlockSpec((1, hp), lambda b, c: (0, 0))],
        out_specs=pl.BlockSpec((Q, hp), omap_f),
        out_shape=jax.ShapeDtypeStruct((T, hp), F32),
        scratch_shapes=[pltpu.VMEM((SSD_GROUPS, SSD_STATE, gw), F32)],
        compiler_params=_cparams("parallel", "arbitrary"),
        name="ssd_fwd",
    )(xbc, xbc, xbc, dtraw, bias_row, alog_row, e_mat(0), d_row)
    in_b, omap_b = specs(True)
    return pl.pallas_call(
        functools.partial(_ssd_kernel, rev=True, nheads=nheads),
        grid=(nb, nc),
        in_specs=in_b + [pl.BlockSpec((Q, hp), omap_b),
                         pl.BlockSpec((Q, hp), omap_b),
                         pl.BlockSpec((1, hp), lambda b, c: (0, 0))],
        out_specs=pl.BlockSpec((Q, hp), omap_b),
        out_shape=jax.ShapeDtypeStruct((T, hp), BF16),
        scratch_shapes=[pltpu.VMEM((SSD_GROUPS, SSD_STATE, gw), F32),
                        pltpu.VMEM((Q, hp), F32)],
        compiler_params=_cparams("parallel", "arbitrary"),
        name="ssd_bwd",
    )(xbc, xbc, xbc, dtraw, bias_row, alog_row, e_mat(nheads), yf, z_src, gain_row)


def _qkprep_kernel(x_ref, g_ref, cos_ref, sin_ref, o_ref):
    x = x_ref[...].astype(F32)
    ms = jnp.mean(x * x, axis=-1, keepdims=True)
    xn = x * lax.rsqrt(ms + EPS) * g_ref[0]
    o_ref[...] = (xn * cos_ref[...] + pltpu.roll(xn, ATTN_HEAD_DIM // 2, 1) * sin_ref[...]).astype(o_ref.dtype)


def qk_prep(proj, col0, gains, cos, sin, seq, tq=512):
    T = proj.shape[0]
    nh = gains.shape[0]
    hd = ATTN_HEAD_DIM
    tq = _tile(seq, tq)
    spt = seq // tq
    return pl.pallas_call(
        _qkprep_kernel,
        grid=(T // tq, nh),
        in_specs=[pl.BlockSpec((tq, hd), lambda i, h: (i, col0 // hd + h)),
                  pl.BlockSpec((1, 1, hd), lambda i, h: (h, 0, 0)),
                  pl.BlockSpec((tq, hd), lambda i, h: (i % spt, 0)),
                  pl.BlockSpec((tq, hd), lambda i, h: (i % spt, 0))],
        out_specs=pl.BlockSpec((tq, hd), lambda i, h: (i, h)),
        out_shape=jax.ShapeDtypeStruct((T, nh * hd), BF16),
        compiler_params=_cparams("parallel", "arbitrary"),
        name="qk_prep",
    )(proj, gains, cos, sin)


def _flash_kernel(q_ref, k_ref, v_ref, o_ref, *, tk, group):
    hd = ATTN_HEAD_DIM
    tq = q_ref.shape[0]
    nk = k_ref.shape[0] // tk
    for g in range(group):
        q = q_ref[:, g * hd:(g + 1) * hd]

        def body(t, carry):
            m, l, acc = carry
            k0 = pl.multiple_of(t * tk, tk)
            kt = k_ref[pl.ds(k0, tk), :]
            vt = v_ref[pl.ds(k0, tk), :]
            s = lax.dot_general(q, kt, (((1,), (1,)), ((), ())), preferred_element_type=F32)
            m_new = jnp.maximum(m, jnp.max(s, axis=-1, keepdims=True))
            p = jnp.exp(s - m_new)
            alpha = jnp.exp(m - m_new)
            l = alpha * l + jnp.sum(p, axis=-1, keepdims=True)
            acc = alpha * acc + _dot(p.astype(BF16), vt)
            return m_new, l, acc

        m0 = jnp.full((tq, 1), NEG_BIG, F32)
        l0 = jnp.zeros((tq, 1), F32)
        a0 = jnp.zeros((tq, hd), F32)
        m, l, acc = lax.fori_loop(0, nk, body, (m0, l0, a0))
        o_ref[:, g * hd:(g + 1) * hd] = (acc / l).astype(o_ref.dtype)


def flash_attention(qk, v_src, v_col0, nq_heads, nb, seq, tq=256, tk=512):
    T = qk.shape[0]
    hd = ATTN_HEAD_DIM
    nkv = ATTN_KV_HEADS
    group = nq_heads // nkv
    tq = _tile(seq, tq)
    tk = _tile(seq, tk)
    nqt = seq // tq
    return pl.pallas_call(
        functools.partial(_flash_kernel, tk=tk, group=group),
        grid=(nb, nkv, nqt),
        in_specs=[pl.BlockSpec((tq, group * hd), lambda b, h, i: (b * nqt + i, h)),
                  pl.BlockSpec((seq, hd), lambda b, h, i: (b, nq_heads + h)),
                  pl.BlockSpec((seq, hd), lambda b, h, i: (b, v_col0 // hd + h))],
        out_specs=pl.BlockSpec((tq, group * hd), lambda b, h, i: (b * nqt + i, h)),
        out_shape=jax.ShapeDtypeStruct((T, nq_heads * hd), BF16),
        compiler_params=_cparams("parallel", "parallel", "arbitrary"),
        name="flash_attn",
    )(qk, qk, v_src)


def _xattn_kernel(q_ref, kv_ref, o_ref, *, heads):
    d = q_ref.shape[1]
    hd = d // heads
    scale = hd ** -0.5
    for h in range(heads):
        q = q_ref[:, h * hd:(h + 1) * hd]
        k = kv_ref[:, h * hd:(h + 1) * hd]
        v = kv_ref[:, d + h * hd:d + (h + 1) * hd]
        s = lax.dot_general(q, k, (((1,), (1,)), ((), ())), preferred_element_type=F32) * scale
        p = jnp.exp(s - jnp.max(s, axis=-1, keepdims=True))
        l = jnp.sum(p, axis=-1, keepdims=True)
        o = _dot(p.astype(BF16), v) / l
        o_ref[:, h * hd:(h + 1) * hd] = o.astype(o_ref.dtype)


def xattn(q, kv, nb, seq, tq=512):
    T, d = q.shape
    n_mem = kv.shape[0] // nb
    tq = _tile(seq, tq)
    nqt = seq // tq
    return pl.pallas_call(
        functools.partial(_xattn_kernel, heads=XA_HEADS),
        grid=(nb, nqt),
        in_specs=[pl.BlockSpec((tq, d), lambda b, i: (b * nqt + i, 0)),
                  pl.BlockSpec((n_mem, 2 * d), lambda b, i: (b, 0))],
        out_specs=pl.BlockSpec((tq, d), lambda b, i: (b * nqt + i, 0)),
        out_shape=jax.ShapeDtypeStruct((T, d), BF16),
        compiler_params=_cparams("parallel", "arbitrary"),
        name="xattn",
    )(q, kv)


def _hyfilter_kernel(z_ref, t_ref, dl_ref, w1_ref, b1_ref, w2_ref, b2_ref, w3_ref, b3_ref,
                     fr_ref, wo_ref, h_ref, sum_ref, *, tl, d):
    i = pl.program_id(0)
    fr = fr_ref[...]
    h = jnp.sin(fr * (_dot_f32(z_ref[...], w1_ref[...]) + b1_ref[...]))
    h = jnp.sin(fr * (_dot_f32(h, w2_ref[...]) + b2_ref[...]))
    h = jnp.sin(fr * (_dot_f32(h, w3_ref[...]) + b3_ref[...]))
    window = jnp.exp(-t_ref[...] * dl_ref[...])
    rows = lax.broadcasted_iota(jnp.int32, (tl, 1), 0) + i * tl

    @pl.when(i == 0)
    def _():
        sum_ref[...] = jnp.zeros_like(sum_ref)

    for part in range(2):
        hp = _dot_f32(h, wo_ref[:, part * d:(part + 1) * d]) * window
        if part == 1:
            hp = jnp.where(rows == 0, 0.0, hp)
        h_ref[:, part * d:(part + 1) * d] = hp.astype(h_ref.dtype)
        sum_ref[:, part * d:(part + 1) * d] += jnp.sum(jnp.abs(hp), axis=0, keepdims=True)


def hyena_filter(z, t_col, deltas, w1, b1, w2, b2, w3, b3, freq, w_out, tl=256):
    L = z.shape[0]
    d2 = w_out.shape[1]
    d = d2 // 2
    fw = w2.shape[0]
    tl = _tile(L, tl)
    full = lambda a: pl.BlockSpec(a.shape, lambda i: (0,) * a.ndim)
    ops = [w1, b1.reshape(1, fw), w2, b2.reshape(1, fw), w3, b3.reshape(1, fw), freq.reshape(1, fw), w_out]
    return pl.pallas_call(
        functools.partial(_hyfilter_kernel, tl=tl, d=d),
        grid=(L // tl,),
        in_specs=[pl.BlockSpec((tl, z.shape[1]), lambda i: (i, 0)),
                  pl.BlockSpec((tl, 1), lambda i: (i, 0)),
                  full(deltas)] + [full(a) for a in ops],
        out_specs=[pl.BlockSpec((tl, d2), lambda i: (i, 0)),
                   pl.BlockSpec((1, d2), lambda i: (0, 0))],
        out_shape=[jax.ShapeDtypeStruct((L, d2), BF16),
                   jax.ShapeDtypeStruct((1, d2), F32)],
        compiler_params=_cparams("arbitrary"),
        name="hyena_filter",
    )(z, t_col, deltas, *ops)


def _mm_kernel(a_ref, b_ref, o_ref):
    o_ref[...] = _dot(a_ref[...], b_ref[...]).astype(o_ref.dtype)


def dft_stage1(f1, u3, tn=8192):
    nb, kh, nc = u3.shape
    m = f1.shape[0]
    tn = _tile(nc, tn)
    return pl.pallas_call(
        _mm_kernel,
        grid=(nb, nc // tn),
        in_specs=[pl.BlockSpec((m, kh), lambda b, j: (0, 0)),
                  pl.BlockSpec((None, kh, tn), lambda b, j: (b, 0, j))],
        out_specs=pl.BlockSpec((None, m, tn), lambda b, j: (b, 0, j)),
        out_shape=jax.ShapeDtypeStruct((nb, m, nc), BF16),
        compiler_params=_cparams("parallel", "arbitrary"),
        name="dft_stage1",
    )(f1, u3)


def _spectrum_kernel(af_ref, ab_ref, h_ref, sum_ref, o_ref, *, d_cols):
    n2 = DFT_N2
    hm = h_ref[...]
    xf = _dot(hm, jnp.concatenate([af_ref[0], af_ref[1]], axis=0))
    xb = _dot(hm, jnp.concatenate([ab_ref[0], ab_ref[1]], axis=0))
    inv = 1.0 / (sum_ref[:, 0:d_cols] + sum_ref[:, d_cols:2 * d_cols])
    o_ref[0] = (xf[:n2] + xb[:n2]) * inv
    o_ref[1] = (xf[n2:] - xb[n2:]) * inv


def filter_spectrum(a5, hmat, sums, d):
    n1 = a5.shape[1]
    n2 = DFT_N2
    return pl.pallas_call(
        functools.partial(_spectrum_kernel, d_cols=d),
        grid=(n1,),
        in_specs=[pl.BlockSpec((2, None, n2, d), lambda k: (0, k, 0, 0)),
                  pl.BlockSpec((2, None, n2, d), lambda k: (0, k, 0, 1)),
                  pl.BlockSpec((None, 2 * n2, 2 * n2), lambda k: (k, 0, 0)),
                  pl.BlockSpec((1, 2 * d), lambda k: (0, 0))],
        out_specs=pl.BlockSpec((2, None, n2, d), lambda k: (0, k, 0, 0)),
        out_shape=jax.ShapeDtypeStruct((2, n1, n2, d), F32),
        compiler_params=_cparams("arbitrary"),
        name="filter_spectrum",
    )(a5, a5, hmat, sums)


def _dftmid_kernel(a_ref, h_ref, g_ref, k_ref, o_ref):
    n2 = DFT_N2
    x = _dot(h_ref[...], jnp.concatenate([a_ref[0], a_ref[1]], axis=0))
    xr, xi = x[:n2], x[n2:]
    kr, ki = k_ref[0], k_ref[1]
    y = jnp.concatenate([xr * kr - xi * ki, xr * ki + xi * kr], axis=0).astype(BF16)
    zz = _dot(g_ref[...], y)
    o_ref[0] = zz[:n2].astype(o_ref.dtype)
    o_ref[1] = zz[n2:].astype(o_ref.dtype)


def dft_mid(a5, hmat, gmat, kspec):
    nb, _, n1, n2, c = a5.shape
    return pl.pallas_call(
        _dftmid_kernel,
        grid=(n1, nb),
        in_specs=[pl.BlockSpec((None, 2, None, n2, c), lambda k, b: (b, 0, k, 0, 0)),
                  pl.BlockSpec((None, 2 * n2, 2 * n2), lambda k, b: (k, 0, 0)),
                  pl.BlockSpec((None, 2 * n2, 2 * n2), lambda k, b: (k, 0, 0)),
                  pl.BlockSpec((2, None, n2, c), lambda k, b: (0, k, 0, 0))],
        out_specs=pl.BlockSpec((None, 2, None, n2, c), lambda k, b: (b, 0, k, 0, 0)),
        out_shape=jax.ShapeDtypeStruct(a5.shape, BF16),
        compiler_params=_cparams("parallel", "arbitrary"),
        name="dft_mid",
    )(a5, hmat, gmat, kspec)


def _dftout_kernel(f_ref, z_ref, x0_ref, w_ref, skip_ref, o_ref):
    y = _dot(f_ref[...], z_ref[...])
    w = w_ref[...].astype(F32)
    o_ref[...] = (x0_ref[...].astype(F32) * (y + w * skip_ref[...])).astype(o_ref.dtype)


def dft_out(f1i, z3, x0, w, skip_row, tn=8192):
    nb, m2, nc = z3.shape
    kh = f1i.shape[0]
    tn = _tile(nc, tn)
    return pl.pallas_call(
        _dftout_kernel,
        grid=(nb, nc // tn),
        in_specs=[pl.BlockSpec((kh, m2), lambda b, j: (0, 0)),
                  pl.BlockSpec((None, m2, tn), lambda b, j: (b, 0, j)),
                  pl.BlockSpec((None, kh, tn), lambda b, j: (b, 0, j)),
                  pl.BlockSpec((None, kh, tn), lambda b, j: (b, 0, j)),
                  pl.BlockSpec((1, tn), lambda b, j: (0, j))],
        out_specs=pl.BlockSpec((None, kh, tn), lambda b, j: (b, 0, j)),
        out_shape=jax.ShapeDtypeStruct((nb, kh, nc), BF16),
        compiler_params=_cparams("parallel", "arbitrary"),
        name="dft_out",
    )(f1i, z3, x0, w, skip_row)


def _rope_tables(seq):
    hd = ATTN_HEAD_DIM
    axis_dim = hd // 2
    t = jnp.arange(seq)
    row = (t // GRID_W).astype(F32)
    col = (t % GRID_W).astype(F32)
    inv_freq = ROPE_THETA ** (-jnp.arange(0, axis_dim, 2, dtype=F32) / axis_dim)
    ang = jnp.concatenate([row[:, None] * inv_freq, col[:, None] * inv_freq], axis=-1)
    c, s = jnp.cos(ang), jnp.sin(ang)
    return jnp.concatenate([c, c], axis=-1), jnp.concatenate([-s, s], axis=-1)


def _hyena_features(seq, d):
    t = jnp.linspace(0.0, 1.0, seq, dtype=F32)
    w = 2.0 * math.pi * jnp.arange(seq, dtype=F32) / seq
    f = jnp.linspace(1e-4, HY_BANDS - 1, HY_BANDS, dtype=F32)
    fw = w[:, None] * f[None, :]
    z = jnp.concatenate([t[:, None], jnp.cos(fw), -jnp.sin(fw)], axis=-1)
    z = jnp.pad(z, ((0, 0), (0, LANES - HY_EMB)))
    deltas = jnp.abs(jnp.linspace(math.log(HY_TARGET) / HY_SLOW_PCT,
                                  math.log(HY_TARGET) / HY_FAST_PCT, d, dtype=F32))
    return z, t[:, None], deltas[None, :]


def _dft_tables(seq):
    n = 2 * seq
    n2 = DFT_N2
    n1 = n // n2

    def cs(phase_int, mod):
        ang = (-2.0 * math.pi / mod) * (phase_int % mod).astype(F32)
        return jnp.cos(ang), jnp.sin(ang)

    k1 = jnp.arange(n1)[:, None]
    m1 = jnp.arange(n1 // 2)[None, :]
    fr, fi = cs(k1 * m1, n1)
    f1 = jnp.concatenate([fr, fi], axis=0).astype(BF16)
    f1i = (jnp.concatenate([fr.T, fi.T], axis=1) / n).astype(BF16)
    kk = (jnp.arange(n1)[:, None, None] + n1 * jnp.arange(n2)[None, :, None])
    nn = jnp.arange(n2)[None, None, :]
    hr, hi = cs(kk * nn, n)
    hmat = jnp.concatenate([jnp.concatenate([hr, -hi], axis=2),
                            jnp.concatenate([hi, hr], axis=2)], axis=1).astype(BF16)
    gr, gi = jnp.swapaxes(hr, 1, 2), -jnp.swapaxes(hi, 1, 2)
    gmat = jnp.concatenate([jnp.concatenate([gr, -gi], axis=2),
                            jnp.concatenate([gi, gr], axis=2)], axis=1).astype(BF16)
    return f1, f1i, hmat, gmat


def _deinterleave(nheads):
    hd = ATTN_HEAD_DIM
    one = np.concatenate([np.arange(0, hd, 2), np.arange(1, hd, 2)])
    return np.concatenate([h * hd + one for h in range(nheads)])


def kernel(x_prompt, x_sample, mem_prompt, mem_sample, norm_mix, norm_xa, norm_mem, norm_ffn, xa_wq, xa_wk, xa_wv, xa_wo, ffn_w_in, ffn_conv_w, ffn_conv_b, ffn_w_out, mix_w_in, mix_w_out, ssd_conv_w, ssd_conv_b, ssd_a_log, ssd_dt_bias, ssd_d, ssd_norm, attn_q_norm, attn_k_norm, hy_w_in, hy_conv_w, hy_conv_b, hy_f_w1, hy_f_b1, hy_f_w2, hy_f_b2, hy_f_w3, hy_f_b3, hy_f_freq, hy_f_w_out, hy_skip, hy_w_out, final_norm):
    nbp, seq, d = x_prompt.shape
    nbs = x_sample.shape[0]
    assert x_sample.shape[1] == seq
    nb = nbp + nbs
    T = nb * seq
    depth = norm_mix.shape[0]
    n_mem = mem_prompt.shape[1]
    d_ff = ffn_w_out.shape[1]

    x = jnp.concatenate([x_prompt, x_sample], axis=0).reshape(T, d)
    mem = jnp.concatenate([mem_prompt, mem_sample], axis=0).reshape(nb * n_mem, d)

    d_ssd = d
    nheads = d_ssd // SSD_HEAD_DIM
    gn = SSD_GROUPS * SSD_STATE
    conv_ch = d_ssd + 2 * gn
    n_att = d // ATTN_HEAD_DIM
    d_kv = ATTN_KV_HEADS * ATTN_HEAD_DIM
    o1 = d_ssd
    o2 = o1 + conv_ch
    o3 = o2 + 2 * nheads
    o4 = o3 + d
    o5 = o4 + d_kv
    qcol = o2
    vcol = o2 + d + d_kv
    cos, sin = _rope_tables(seq)
    perm_q = _deinterleave(n_att)
    perm_k = _deinterleave(ATTN_KV_HEADS)
    perm_h = _deinterleave(1)

    n2 = DFT_N2
    n1 = 2 * seq // n2
    hz, t_col, deltas = _hyena_features(seq, d)
    f1, f1i, hmat, gmat = _dft_tables(seq)

    for i in range(depth):
        if i % 2 == 0:
            e = i // 2
            w = mix_w_in[e]
            w_main = jnp.concatenate([w[:, :o2], w[:, o3:o4][:, perm_q], w[:, o4:o5][:, perm_k], w[:, o5:]],
                                     axis=1).astype(BF16)
            w_dt = jnp.pad(w[:, o2:o3], ((0, 0), (0, LANES - 2 * nheads))).astype(BF16)
            proj = normmm(x, norm_mix[i], w_main)
            dtraw = normmm(x, norm_mix[i], w_dt, out_dtype=F32, tn=LANES)
            (xbc,) = dwconv(proj, jnp.pad(ssd_conv_w[e], ((0, 0), (o1, 0))), jnp.pad(ssd_conv_b[e], ((o1, 0),)),
                            [o1], conv_ch, _epi_silu, 1, seq)
            pad_row = lambda a: jnp.pad(a.reshape(1, -1).astype(F32), ((0, 0), (0, LANES - 2 * nheads)))
            y_ssd = ssd_scan(xbc, dtraw, pad_row(ssd_dt_bias[e]), pad_row(ssd_a_log[e]),
                             jnp.repeat(ssd_d[e].astype(F32), SSD_HEAD_DIM)[None, :],
                             proj, ssd_norm[e].reshape(1, -1).astype(F32), nb, seq)
            scale = ATTN_HEAD_DIM ** -0.5
            gains = jnp.concatenate([jnp.tile(attn_q_norm[e][perm_h][None, :] * scale, (n_att, 1)),
                                     jnp.tile(attn_k_norm[e][perm_h][None, :], (ATTN_KV_HEADS, 1))],
                                    axis=0)[:, None, :].astype(F32)
            qk = qk_prep(proj, qcol, gains, cos, sin, seq)
            y_att = flash_attention(qk, proj, vcol, n_att, nb, seq)
            x = mm_res(jnp.concatenate([y_ssd, y_att], axis=1), mix_w_out[e].astype(BF16), x)
        else:
            o = i // 2
            u = normmm(x, norm_mix[i], hy_w_in[o].astype(BF16))
            x0, wv = dwconv(u, hy_conv_w[o], hy_conv_b[o], [0, d, 2 * d], d, _epi_hyena, 2, seq)
            w1 = jnp.pad(hy_f_w1[o], ((0, LANES - HY_EMB), (0, 0)))
            hfb, sums = hyena_filter(hz, t_col, deltas, w1, hy_f_b1[o], hy_f_w2[o], hy_f_b2[o],
                                     hy_f_w3[o], hy_f_b3[o], hy_f_freq[o], hy_f_w_out[o])
            a_f = dft_stage1(f1, hfb.reshape(1, n1 // 2, n2 * 2 * d))
            kspec = filter_spectrum(a_f.reshape(2, n1, n2, 2 * d), hmat, sums, d)
            a_u = dft_stage1(f1, wv.reshape(nb, n1 // 2, n2 * d))
            zz = dft_mid(a_u.reshape(nb, 2, n1, n2, d), hmat, gmat, kspec)
            yh = dft_out(f1i, zz.reshape(nb, 2 * n1, n2 * d), x0.reshape(nb, n1 // 2, n2 * d),
                         wv.reshape(nb, n1 // 2, n2 * d), jnp.tile(hy_skip[o].astype(F32), n2)[None, :])
            x = mm_res(yh.reshape(T, d), hy_w_out[o].astype(BF16), x)
        q = normmm(x, norm_xa[i], xa_wq[i].astype(BF16))
        kv = normmm(mem, norm_mem[i], jnp.concatenate([xa_wk[i], xa_wv[i]], axis=1).astype(BF16))
        x = mm_res(xattn(q, kv, nb, seq), xa_wo[i].astype(BF16), x)
        u = normmm(x, norm_ffn[i], ffn_w_in[i].astype(BF16))
        (act,) = dwconv(u, ffn_conv_w[i], ffn_conv_b[i], [0, d_ff], d_ff, _epi_glu, 1, seq)
        x = mm_res(act, ffn_w_out[i].astype(BF16), x)

    y = rmsnorm(x, final_norm).reshape(nb, seq, d)
    return (y[:nbp], y[nbp:])
```

```python
import functools
import math

import numpy as np
import jax
import jax.numpy as jnp
from jax import lax
from jax.experimental import pallas as pl
from jax.experimental.pallas import tpu as pltpu

F32 = jnp.float32
BF16 = jnp.bfloat16
EPS = 1e-6

GRID_W = 64
XA_HEADS = 4
SSD_HEAD_DIM = 64
SSD_GROUPS = 4
SSD_STATE = 128
SSD_CHUNK = 128
ATTN_HEAD_DIM = 128
ATTN_KV_HEADS = 4
ROPE_THETA = 10000.0
HY_EMB = 33
HY_BANDS = (HY_EMB - 1) // 2
HY_TARGET = 1e-2
HY_FAST_PCT = 0.3
HY_SLOW_PCT = 1.5

LANES = 128
DFT_N2 = 128
VMEM_LIMIT = 52 * 1024 * 1024
NEG_BIG = -1e30


def _cparams(*sem):
    return pltpu.CompilerParams(dimension_semantics=sem, vmem_limit_bytes=VMEM_LIMIT)


def _tile(dim, pref):
    t = min(dim, pref)
    while dim % t:
        t //= 2
    return t


def _split3(x):
    hi = x.astype(BF16)
    r1 = x - hi.astype(F32)
    mid = r1.astype(BF16)
    lo = (r1 - mid.astype(F32)).astype(BF16)
    return hi, mid, lo


def _dot(a, b):
    return jnp.dot(a, b, preferred_element_type=F32)


def _dot_exact_rhs(x, e):
    hi, mid, lo = _split3(x)
    return _dot(hi, e) + _dot(mid, e) + _dot(lo, e)


def _dot_exact_lhs(e, x):
    hi, mid, lo = _split3(x)
    return _dot(e, hi) + _dot(e, mid) + _dot(e, lo)


def _dot_f32(a, b):
    ah, am, _ = _split3(a)
    bh, bm, _ = _split3(b)
    return _dot(ah, bh) + _dot(ah, bm) + _dot(am, bh)


def _silu(x):
    return x * (1.0 / (1.0 + jnp.exp(-x)))


def _normmm_kernel(x_ref, g_ref, w_ref, o_ref, xn_ref):
    @pl.when(pl.program_id(1) == 0)
    def _():
        x = x_ref[...].astype(F32)
        ms = jnp.mean(x * x, axis=-1, keepdims=True)
        xn_ref[...] = (x * lax.rsqrt(ms + EPS) * g_ref[...]).astype(BF16)

    o_ref[...] = _dot(xn_ref[...], w_ref[...]).astype(o_ref.dtype)


def normmm(x, g, w, out_dtype=BF16, tm=1024, tn=1024):
    M, K = x.shape
    N = w.shape[1]
    tm = _tile(M, tm)
    tn = _tile(N, tn)
    return pl.pallas_call(
        _normmm_kernel,
        grid=(M // tm, N // tn),
        in_specs=[pl.BlockSpec((tm, K), lambda i, j: (i, 0)),
                  pl.BlockSpec((1, K), lambda i, j: (0, 0)),
                  pl.BlockSpec((K, tn), lambda i, j: (0, j))],
        out_specs=pl.BlockSpec((tm, tn), lambda i, j: (i, j)),
        out_shape=jax.ShapeDtypeStruct((M, N), out_dtype),
        scratch_shapes=[pltpu.VMEM((tm, K), BF16)],
        compiler_params=_cparams("parallel", "arbitrary"),
        name="normmm",
    )(x, g.reshape(1, K).astype(F32), w)


def _mmres_kernel(a_ref, w_ref, r_ref, o_ref):
    o_ref[...] = r_ref[...] + _dot(a_ref[...], w_ref[...])


def mm_res(a, w, res, tm=1024):
    M, K = a.shape
    N = w.shape[1]
    tm = _tile(M, tm)
    tn = _tile(N, 1024 if K <= 2048 else 512)
    return pl.pallas_call(
        _mmres_kernel,
        grid=(M // tm, N // tn),
        in_specs=[pl.BlockSpec((tm, K), lambda i, j: (i, 0)),
                  pl.BlockSpec((K, tn), lambda i, j: (0, j)),
                  pl.BlockSpec((tm, tn), lambda i, j: (i, j))],
        out_specs=pl.BlockSpec((tm, tn), lambda i, j: (i, j)),
        out_shape=jax.ShapeDtypeStruct((M, N), F32),
        compiler_params=_cparams("parallel", "arbitrary"),
        name="mm_res",
    )(a, w, res)


def _rmsnorm_kernel(x_ref, g_ref, o_ref):
    x = x_ref[...]
    ms = jnp.mean(x * x, axis=-1, keepdims=True)
    o_ref[...] = x * lax.rsqrt(ms + EPS) * g_ref[...]


def rmsnorm(x, g, tm=512):
    M, K = x.shape
    tm = _tile(M, tm)
    return pl.pallas_call(
        _rmsnorm_kernel,
        grid=(M // tm,),
        in_specs=[pl.BlockSpec((tm, K), lambda i: (i, 0)),
                  pl.BlockSpec((1, K), lambda i: (0, 0))],
        out_specs=pl.BlockSpec((tm, K), lambda i: (i, 0)),
        out_shape=jax.ShapeDtypeStruct((M, K), F32),
        compiler_params=_cparams("parallel"),
        name="final_norm",
    )(x, g.reshape(1, K).astype(F32))


CONV_HALO = 16


def _normmm_conv_kernel(*refs, nseg, width, tm, seq, epilogue, nout):
    xm_ref, xp_ref, xn_ref, g_ref = refs[:4]
    segs = [refs[4 + 3 * s:7 + 3 * s] for s in range(nseg)]
    outs = refs[4 + 3 * nseg:4 + 3 * nseg + nout]
    hn_ref = refs[4 + 3 * nseg + nout]
    exts = refs[5 + 3 * nseg + nout:]
    h = CONV_HALO
    half = width // 2
    row0 = pl.program_id(0) * tm
    at_start = (row0 % seq) == 0
    at_end = ((row0 + tm) % seq) == 0

    @pl.when(pl.program_id(1) == 0)
    def _():
        def nrm(x):
            ms = jnp.mean(x * x, axis=-1, keepdims=True)
            return (x * lax.rsqrt(ms + EPS) * g_ref[...]).astype(BF16)
        hn_ref[0:h, :] = nrm(xp_ref[...])
        hn_ref[h:h + tm, :] = nrm(xm_ref[...])
        hn_ref[h + tm:h + tm + h, :] = nrm(xn_ref[...])

    vals = []
    for (w_ref, cw_ref, cb_ref), ext in zip(segs, exts):
        ext[...] = _dot(hn_ref[...], w_ref[...])

        @pl.when(at_start)
        def _():
            ext[0:h, :] = jnp.zeros((h, ext.shape[1]), F32)

        @pl.when(at_end)
        def _():
            ext[h + tm:h + tm + h, :] = jnp.zeros((h, ext.shape[1]), F32)

        acc = None
        for k in range(width):
            term = ext[h - half + k:h - half + k + tm, :] * cw_ref[k:k + 1, :]
            acc = term if acc is None else acc + term
        vals.append(acc + cb_ref[...])
    for o, r in zip(outs, epilogue(*vals)):
        o[...] = r.astype(o.dtype)


def normmm_conv(x, g, w, conv_w, conv_b, seg_cols, width_cols, epilogue, nout, seq, tm=1024, tn=512):
    T, K = x.shape
    width = conv_w.shape[0]
    tm = _tile(seq, tm)
    tn = _tile(width_cols, tn)
    h = CONV_HALO
    nrb = T // h
    cb = conv_b.reshape(1, -1).astype(F32)
    cw = conv_w.astype(F32)
    in_specs = [pl.BlockSpec((tm, K), lambda i, j: (i, 0)),
                pl.BlockSpec((h, K), lambda i, j: (jnp.maximum(i * (tm // h) - 1, 0), 0)),
                pl.BlockSpec((h, K), lambda i, j: (jnp.minimum((i + 1) * (tm // h), nrb - 1), 0)),
                pl.BlockSpec((1, K), lambda i, j: (0, 0))]
    args = [x, x, x, g.reshape(1, K).astype(F32)]
    for c0 in seg_cols:
        off = c0 // tn
        in_specs += [pl.BlockSpec((K, tn), lambda i, j, off=off: (0, j + off)),
                     pl.BlockSpec((width, tn), lambda i, j, off=off: (0, j + off)),
                     pl.BlockSpec((1, tn), lambda i, j, off=off: (0, j + off))]
        args += [w, cw, cb]
    kern = functools.partial(_normmm_conv_kernel, nseg=len(seg_cols), width=width, tm=tm, seq=seq,
                             epilogue=epilogue, nout=nout)
    return pl.pallas_call(
        kern,
        grid=(T // tm, width_cols // tn),
        in_specs=in_specs,
        out_specs=[pl.BlockSpec((tm, tn), lambda i, j: (i, j)) for _ in range(nout)],
        out_shape=[jax.ShapeDtypeStruct((T, width_cols), BF16) for _ in range(nout)],
        scratch_shapes=[pltpu.VMEM((tm + 2 * h, K), BF16)]
        + [pltpu.VMEM((tm + 2 * h, tn), F32) for _ in seg_cols],
        compiler_params=_cparams("parallel", "arbitrary"),
        name="normmm_conv",
    )(*args)


def _epi_silu(c):
    return (_silu(c),)


def _epi_glu(g, up):
    return (_silu(g) * up,)


def _epi_hyena(x0, x1, v):
    return (x0, v * x1)


def _softplus(x):
    return jnp.maximum(x, 0.0) + jnp.log(1.0 + jnp.exp(-jnp.abs(x)))


def _ssd_kernel(*refs, rev, nheads):
    if rev:
        (xs_ref, b_ref, c_ref, dt_ref, bias_ref, alog_ref, e_ref,
         yf_ref, z_ref, gain_ref, o_ref, s_ref, y_ref) = refs
    else:
        (xs_ref, b_ref, c_ref, dt_ref, bias_ref, alog_ref, e_ref,
         dskip_ref, o_ref, s_ref) = refs
        y_ref = o_ref
    Q = SSD_CHUNK
    P = SSD_HEAD_DIM
    hpg = nheads // SSD_GROUPS
    gw = hpg * P
    hoff = nheads if rev else 0

    @pl.when(pl.program_id(1) == 0)
    def _():
        s_ref[...] = jnp.zeros_like(s_ref)

    row = lax.broadcasted_iota(jnp.int32, (Q, Q), 0)
    col = lax.broadcasted_iota(jnp.int32, (Q, Q), 1)
    mask = (col >= row) if rev else (col <= row)
    tri = jnp.where(mask, 1.0, 0.0).astype(BF16)

    dtv = _softplus(dt_ref[...] + bias_ref[...])
    a_row = -jnp.exp(alog_ref[...])
    la = dtv * a_row
    cs = _dot_exact_lhs(tri, la)
    tot = cs[0:1, :] if rev else cs[Q - 1:Q, :]
    cs_t = cs.T
    dt_t = dtv.T
    e = e_ref[...]
    carry_in = _dot_exact_rhs(jnp.exp(cs), e)
    to_end = _dot_exact_rhs(jnp.exp(tot - cs) * dtv, e)
    dec = carry_in[0:1, :] if rev else carry_in[Q - 1:Q, :]

    xs = xs_ref[...]
    x_state = (xs.astype(F32) * to_end).astype(BF16)
    lane = lax.broadcasted_iota(jnp.int32, (Q, LANES), 1)
    low = lane < P

    for g in range(SSD_GROUPS):
        bg = b_ref[:, g * SSD_STATE:(g + 1) * SSD_STATE]
        cg = c_ref[:, g * SSD_STATE:(g + 1) * SSD_STATE]
        cb = lax.dot_general(cg, bg, (((1,), (1,)), ((), ())), preferred_element_type=F32)
        s_old = s_ref[g]
        y_off = _dot(cg, s_old.astype(BF16)) * carry_in[:, g * gw:(g + 1) * gw]
        s_ref[g] = s_old * dec[:, g * gw:(g + 1) * gw] + lax.dot_general(
            bg, x_state[:, g * gw:(g + 1) * gw], (((0,), (0,)), ((), ())), preferred_element_type=F32)
        for j in range(hpg // 2):
            ws = []
            for hh in range(2):
                hc = hoff + g * hpg + 2 * j + hh
                diff = cs[:, hc:hc + 1] - cs_t[hc:hc + 1, :]
                decay = jnp.exp(jnp.where(mask, diff, NEG_BIG))
                ws.append((cb * decay * dt_t[hc:hc + 1, :]).astype(BF16))
            c0 = g * gw + 2 * j * P
            xp = xs[:, c0:c0 + LANES]
            rhs = jnp.concatenate([jnp.where(low, xp, jnp.zeros_like(xp)),
                                   jnp.where(low, jnp.zeros_like(xp), xp)], axis=0)
            y = _dot(jnp.concatenate(ws, axis=1), rhs) + y_off[:, 2 * j * P:2 * j * P + LANES]
            if not rev:
                y = y + xp.astype(F32) * dskip_ref[:, c0:c0 + LANES]
            y_ref[:, c0:c0 + LANES] = y

    if rev:
        y = y_ref[...] + yf_ref[...]
        gated = y * _silu(z_ref[...].astype(F32))
        ms = jnp.mean(gated * gated, axis=-1, keepdims=True)
        o_ref[...] = (gated * lax.rsqrt(ms + EPS) * gain_ref[...]).astype(o_ref.dtype)


def ssd_scan(xbc, dtraw, bias_row, alog_row, d_row, z_src, gain_row, nb, seq):
    T = xbc.shape[0]
    Q = SSD_CHUNK
    nc = seq // Q
    gn = SSD_GROUPS * SSD_STATE
    hp = xbc.shape[1] - 2 * gn
    nheads = hp // SSD_HEAD_DIM
    hpg = nheads // SSD_GROUPS
    gw = hpg * SSD_HEAD_DIM
    assert hp % gn == 0 and 2 * nheads <= LANES

    def e_mat(off):
        r = np.arange(LANES)[:, None]
        c = np.arange(hp)[None, :]
        return jnp.asarray((r == off + c // SSD_HEAD_DIM).astype(np.float32), dtype=BF16)

    def specs(rev):
        def blk(c):
            return (nc - 1 - c) if rev else c
        return [
            pl.BlockSpec((Q, hp), lambda b, c: (b * nc + blk(c), 0)),
            pl.BlockSpec((Q, gn), lambda b, c: (b * nc + blk(c), hp // gn)),
            pl.BlockSpec((Q, gn), lambda b, c: (b * nc + blk(c), hp // gn + 1)),
            pl.BlockSpec((Q, LANES), lambda b, c: (b * nc + blk(c), 0)),
            pl.BlockSpec((1, LANES), lambda b, c: (0, 0)),
            pl.BlockSpec((1, LANES), lambda b, c: (0, 0)),
            pl.BlockSpec((LANES, hp), lambda b, c: (0, 0)),
        ], (lambda b, c: (b * nc + blk(c), 0))

    in_f, omap_f = specs(False)
    yf = pl.pallas_call(
        functools.partial(_ssd_kernel, rev=False, nheads=nheads),
        grid=(nb, nc),
        in_specs=in_f + [pl.BlockSpec((1, hp), lambda b, c: (0, 0))],
        out_specs=pl.BlockSpec((Q, hp), omap_f),
        out_shape=jax.ShapeDtypeStruct((T, hp), F32),
        scratch_shapes=[pltpu.VMEM((SSD_GROUPS, SSD_STATE, gw), F32)],
        compiler_params=_cparams("parallel", "arbitrary"),
        name="ssd_fwd",
    )(xbc, xbc, xbc, dtraw, bias_row, alog_row, e_mat(0), d_row)
    in_b, omap_b = specs(True)
    return pl.pallas_call(
        functools.partial(_ssd_kernel, rev=True, nheads=nheads),
        grid=(nb, nc),
        in_specs=in_b + [pl.BlockSpec((Q, hp), omap_b),
                         pl.BlockSpec((Q, hp), omap_b),
                         pl.BlockSpec((1, hp), lambda b, c: (0, 0))],
        out_specs=pl.BlockSpec((Q, hp), omap_b),
        out_shape=jax.ShapeDtypeStruct((T, hp), BF16),
        scratch_shapes=[pltpu.VMEM((SSD_GROUPS, SSD_STATE, gw), F32),
                        pltpu.VMEM((Q, hp), F32)],
        compiler_params=_cparams("parallel", "arbitrary"),
        name="ssd_bwd",
    )(xbc, xbc, xbc, dtraw, bias_row, alog_row, e_mat(nheads), yf, z_src, gain_row)


def _qkprep_kernel(x_ref, g_ref, cos_ref, sin_ref, o_ref):
    x = x_ref[...].astype(F32)
    ms = jnp.mean(x * x, axis=-1, keepdims=True)
    xn = x * lax.rsqrt(ms + EPS) * g_ref[0]
    o_ref[...] = (xn * cos_ref[...] + pltpu.roll(xn, ATTN_HEAD_DIM // 2, 1) * sin_ref[...]).astype(o_ref.dtype)


def qk_prep(proj, col0, gains, cos, sin, seq, tq=512):
    T = proj.shape[0]
    nh = gains.shape[0]
    hd = ATTN_HEAD_DIM
    tq = _tile(seq, tq)
    spt = seq // tq
    return pl.pallas_call(
        _qkprep_kernel,
        grid=(T // tq, nh),
        in_specs=[pl.BlockSpec((tq, hd), lambda i, h: (i, col0 // hd + h)),
                  pl.BlockSpec((1, 1, hd), lambda i, h: (h, 0, 0)),
                  pl.BlockSpec((tq, hd), lambda i, h: (i % spt, 0)),
                  pl.BlockSpec((tq, hd), lambda i, h: (i % spt, 0))],
        out_specs=pl.BlockSpec((tq, hd), lambda i, h: (i, h)),
        out_shape=jax.ShapeDtypeStruct((T, nh * hd), BF16),
        compiler_params=_cparams("parallel", "arbitrary"),
        name="qk_prep",
    )(proj, gains, cos, sin)


FLASH_ROW_BLOCK = 32


def _flash_kernel(q_ref, k_ref, v_ref, o_ref, qs_ref, va_ref, s0_ref, s1_ref, p_ref, acc_ref,
                  m_ref, al_ref, *, tk, group):
    hd = ATTN_HEAD_DIM
    tq = q_ref.shape[0]
    rows = group * tq
    seq = k_ref.shape[0]
    nk = seq // tk

    @pl.when(pl.program_id(2) == 0)
    def _():
        va_ref[:, 0:hd] = v_ref[...]
        va_ref[:, hd:2 * hd] = jnp.ones((seq, hd), BF16)

    for g in range(group):
        qs_ref[g * tq:(g + 1) * tq, :] = q_ref[:, g * hd:(g + 1) * hd]
    m_ref[...] = jnp.full(m_ref.shape, NEG_BIG, F32)
    acc_ref[...] = jnp.zeros(acc_ref.shape, F32)
    nlc = tk // LANES

    def scores(t, s_ref):
        k0 = pl.multiple_of(jnp.minimum(t, nk - 1) * tk, tk)
        s_ref[...] = lax.dot_general(qs_ref[...], k_ref[pl.ds(k0, tk), :], (((1,), (1,)), ((), ())),
                                     preferred_element_type=F32)

    def update(t, s_ref):
        for r0 in range(0, rows, FLASH_ROW_BLOCK):
            rs = slice(r0, r0 + FLASH_ROW_BLOCK)
            ch = [s_ref[rs, c * LANES:(c + 1) * LANES] for c in range(nlc)]
            mx = ch[0]
            for c in range(1, nlc):
                mx = jnp.maximum(mx, ch[c])
            m_old = m_ref[rs, :]
            m_new = jnp.maximum(m_old, jnp.max(mx, axis=-1, keepdims=True))
            m_ref[rs, :] = m_new
            al_ref[rs, :] = jnp.exp2(m_old - m_new)
            for c in range(nlc):
                p_ref[rs, c * LANES:(c + 1) * LANES] = jnp.exp2(ch[c] - m_new).astype(BF16)
        k0 = pl.multiple_of(t * tk, tk)
        pv = _dot(p_ref[...], va_ref[pl.ds(k0, tk), :])
        al = al_ref[...]
        acc_ref[...] = acc_ref[...] * jnp.concatenate([al, al], axis=1) + pv

    scores(0, s0_ref)

    def body(t2, carry):
        scores(2 * t2 + 1, s1_ref)
        update(2 * t2, s0_ref)
        scores(2 * t2 + 2, s0_ref)
        update(2 * t2 + 1, s1_ref)
        return carry

    lax.fori_loop(0, nk // 2, body, 0)
    o = acc_ref[:, 0:hd] / acc_ref[:, hd:2 * hd]
    for g in range(group):
        o_ref[:, g * hd:(g + 1) * hd] = o[g * tq:(g + 1) * tq, :].astype(o_ref.dtype)


def flash_attention(qk, v_src, v_col0, nq_heads, nb, seq, tq=128, tk=512):
    T = qk.shape[0]
    hd = ATTN_HEAD_DIM
    nkv = ATTN_KV_HEADS
    group = nq_heads // nkv
    tq = _tile(seq, tq)
    tk = _tile(seq, tk)
    nqt = seq // tq
    return pl.pallas_call(
        functools.partial(_flash_kernel, tk=tk, group=group),
        grid=(nb, nkv, nqt),
        in_specs=[pl.BlockSpec((tq, group * hd), lambda b, h, i: (b * nqt + i, h)),
                  pl.BlockSpec((seq, hd), lambda b, h, i: (b, nq_heads + h)),
                  pl.BlockSpec((seq, hd), lambda b, h, i: (b, v_col0 // hd + h))],
        out_specs=pl.BlockSpec((tq, group * hd), lambda b, h, i: (b * nqt + i, h)),
        out_shape=jax.ShapeDtypeStruct((T, nq_heads * hd), BF16),
        scratch_shapes=[pltpu.VMEM((group * tq, hd), BF16),
                        pltpu.VMEM((seq, 2 * hd), BF16),
                        pltpu.VMEM((group * tq, tk), F32),
                        pltpu.VMEM((group * tq, tk), F32),
                        pltpu.VMEM((group * tq, tk), BF16),
                        pltpu.VMEM((group * tq, 2 * hd), F32),
                        pltpu.VMEM((group * tq, LANES), F32),
                        pltpu.VMEM((group * tq, LANES), F32)],
        compiler_params=_cparams("arbitrary", "arbitrary", "arbitrary"),
        name="flash_attn",
    )(qk, qk, v_src)


def _xattn_kernel(q_ref, kv_ref, o_ref, *, heads):
    d = q_ref.shape[1]
    hd = d // heads
    scale = hd ** -0.5
    for h in range(heads):
        q = q_ref[:, h * hd:(h + 1) * hd]
        k = kv_ref[:, h * hd:(h + 1) * hd]
        v = kv_ref[:, d + h * hd:d + (h + 1) * hd]
        s = lax.dot_general(q, k, (((1,), (1,)), ((), ())), preferred_element_type=F32) * scale
        p = jnp.exp(s - jnp.max(s, axis=-1, keepdims=True))
        l = jnp.sum(p, axis=-1, keepdims=True)
        o = _dot(p.astype(BF16), v) / l
        o_ref[:, h * hd:(h + 1) * hd] = o.astype(o_ref.dtype)


def xattn(q, kv, nb, seq, tq=512):
    T, d = q.shape
    n_mem = kv.shape[0] // nb
    tq = _tile(seq, tq)
    nqt = seq // tq
    return pl.pallas_call(
        functools.partial(_xattn_kernel, heads=XA_HEADS),
        grid=(nb, nqt),
        in_specs=[pl.BlockSpec((tq, d), lambda b, i: (b * nqt + i, 0)),
                  pl.BlockSpec((n_mem, 2 * d), lambda b, i: (b, 0))],
        out_specs=pl.BlockSpec((tq, d), lambda b, i: (b * nqt + i, 0)),
        out_shape=jax.ShapeDtypeStruct((T, d), BF16),
        compiler_params=_cparams("parallel", "arbitrary"),
        name="xattn",
    )(q, kv)


def _hyfilter_kernel(z_ref, t_ref, dl_ref, w1_ref, b1_ref, w2_ref, b2_ref, w3_ref, b3_ref,
                     fr_ref, wo_ref, h_ref, sum_ref, *, tl, d):
    i = pl.program_id(0)
    fr = fr_ref[...]
    h = jnp.sin(fr * (_dot_f32(z_ref[...], w1_ref[...]) + b1_ref[...]))
    h = jnp.sin(fr * (_dot_f32(h, w2_ref[...]) + b2_ref[...]))
    h = jnp.sin(fr * (_dot_f32(h, w3_ref[...]) + b3_ref[...]))
    window = jnp.exp(-t_ref[...] * dl_ref[...])
    rows = lax.broadcasted_iota(jnp.int32, (tl, 1), 0) + i * tl

    @pl.when(i == 0)
    def _():
        sum_ref[...] = jnp.zeros_like(sum_ref)

    for part in range(2):
        hp = _dot_f32(h, wo_ref[:, part * d:(part + 1) * d]) * window
        if part == 1:
            hp = jnp.where(rows == 0, 0.0, hp)
        h_ref[:, part * d:(part + 1) * d] = hp.astype(h_ref.dtype)
        sum_ref[:, part * d:(part + 1) * d] += jnp.sum(jnp.abs(hp), axis=0, keepdims=True)


def hyena_filter(z, t_col, deltas, w1, b1, w2, b2, w3, b3, freq, w_out, tl=256):
    L = z.shape[0]
    d2 = w_out.shape[1]
    d = d2 // 2
    fw = w2.shape[0]
    tl = _tile(L, tl)
    full = lambda a: pl.BlockSpec(a.shape, lambda i: (0,) * a.ndim)
    ops = [w1, b1.reshape(1, fw), w2, b2.reshape(1, fw), w3, b3.reshape(1, fw), freq.reshape(1, fw), w_out]
    return pl.pallas_call(
        functools.partial(_hyfilter_kernel, tl=tl, d=d),
        grid=(L // tl,),
        in_specs=[pl.BlockSpec((tl, z.shape[1]), lambda i: (i, 0)),
                  pl.BlockSpec((tl, 1), lambda i: (i, 0)),
                  full(deltas)] + [full(a) for a in ops],
        out_specs=[pl.BlockSpec((tl, d2), lambda i: (i, 0)),
                   pl.BlockSpec((1, d2), lambda i: (0, 0))],
        out_shape=[jax.ShapeDtypeStruct((L, d2), BF16),
                   jax.ShapeDtypeStruct((1, d2), F32)],
        compiler_params=_cparams("arbitrary"),
        name="hyena_filter",
    )(z, t_col, deltas, *ops)


DFT_ROWS = 16


def _dft1_kernel(f_ref, u_ref, o_ref):
    f = f_ref[...]
    for r in range(DFT_ROWS):
        o_ref[:, r, :] = _dot(f, u_ref[:, r, :]).astype(o_ref.dtype)


def dft_stage1(f1, u4, tc=512):
    nb, kh, n2, c = u4.shape
    m = f1.shape[0]
    tc = _tile(c, tc)
    return pl.pallas_call(
        _dft1_kernel,
        grid=(nb, n2 // DFT_ROWS, c // tc),
        in_specs=[pl.BlockSpec((m, kh), lambda b, i, j: (0, 0)),
                  pl.BlockSpec((None, kh, DFT_ROWS, tc), lambda b, i, j: (b, 0, i, j))],
        out_specs=pl.BlockSpec((None, m, DFT_ROWS, tc), lambda b, i, j: (b, 0, i, j)),
        out_shape=jax.ShapeDtypeStruct((nb, m, n2, c), BF16),
        compiler_params=_cparams("parallel", "parallel", "arbitrary"),
        name="dft_stage1",
    )(f1, u4)


def _spectrum_kernel(af_ref, ab_ref, h_ref, sum_ref, o_ref, *, d_cols):
    n2 = DFT_N2
    hm = h_ref[...]
    xf = _dot(hm, jnp.concatenate([af_ref[0], af_ref[1]], axis=0))
    xb = _dot(hm, jnp.concatenate([ab_ref[0], ab_ref[1]], axis=0))
    inv = 1.0 / (sum_ref[:, 0:d_cols] + sum_ref[:, d_cols:2 * d_cols])
    o_ref[0] = (xf[:n2] + xb[:n2]) * inv
    o_ref[1] = (xf[n2:] - xb[n2:]) * inv


def filter_spectrum(a5, hmat, sums, d):
    n1 = a5.shape[1]
    n2 = DFT_N2
    return pl.pallas_call(
        functools.partial(_spectrum_kernel, d_cols=d),
        grid=(n1,),
        in_specs=[pl.BlockSpec((2, None, n2, d), lambda k: (0, k, 0, 0)),
                  pl.BlockSpec((2, None, n2, d), lambda k: (0, k, 0, 1)),
                  pl.BlockSpec((None, 2 * n2, 2 * n2), lambda k: (k, 0, 0)),
                  pl.BlockSpec((1, 2 * d), lambda k: (0, 0))],
        out_specs=pl.BlockSpec((2, None, n2, d), lambda k: (0, k, 0, 0)),
        out_shape=jax.ShapeDtypeStruct((2, n1, n2, d), F32),
        compiler_params=_cparams("arbitrary"),
        name="filter_spectrum",
    )(a5, a5, hmat, sums)


def _dftmid_kernel(a_ref, h_ref, g_ref, k_ref, o_ref):
    n2 = DFT_N2
    x = _dot(h_ref[...], jnp.concatenate([a_ref[0], a_ref[1]], axis=0))
    xr, xi = x[:n2], x[n2:]
    kr, ki = k_ref[0], k_ref[1]
    y = jnp.concatenate([xr * kr - xi * ki, xr * ki + xi * kr], axis=0).astype(BF16)
    zz = _dot(g_ref[...], y)
    o_ref[0] = zz[:n2].astype(o_ref.dtype)
    o_ref[1] = zz[n2:].astype(o_ref.dtype)


def dft_mid(a5, hmat, gmat, kspec):
    nb, _, n1, n2, c = a5.shape
    return pl.pallas_call(
        _dftmid_kernel,
        grid=(n1, nb),
        in_specs=[pl.BlockSpec((None, 2, None, n2, c), lambda k, b: (b, 0, k, 0, 0)),
                  pl.BlockSpec((None, 2 * n2, 2 * n2), lambda k, b: (k, 0, 0)),
                  pl.BlockSpec((None, 2 * n2, 2 * n2), lambda k, b: (k, 0, 0)),
                  pl.BlockSpec((2, None, n2, c), lambda k, b: (0, k, 0, 0))],
        out_specs=pl.BlockSpec((None, 2, None, n2, c), lambda k, b: (b, 0, k, 0, 0)),
        out_shape=jax.ShapeDtypeStruct(a5.shape, BF16),
        compiler_params=_cparams("parallel", "arbitrary"),
        name="dft_mid",
    )(a5, hmat, gmat, kspec)


def _dftout_kernel(f_ref, z_ref, x0_ref, w_ref, skip_ref, o_ref, y_ref):
    f = f_ref[...]
    for r in range(DFT_ROWS):
        y_ref[:, r, :] = _dot(f, z_ref[:, r, :])
    w = w_ref[...].astype(F32)
    o_ref[...] = (x0_ref[...].astype(F32) * (y_ref[...] + w * skip_ref[...])).astype(o_ref.dtype)


def dft_out(f1i, z4, x0, w, skip, tc=512):
    nb, m2, n2, c = z4.shape
    kh = f1i.shape[0]
    tc = _tile(c, tc)
    tok = pl.BlockSpec((None, kh, DFT_ROWS, tc), lambda b, i, j: (b, 0, i, j))
    return pl.pallas_call(
        _dftout_kernel,
        grid=(nb, n2 // DFT_ROWS, c // tc),
        in_specs=[pl.BlockSpec((kh, m2), lambda b, i, j: (0, 0)),
                  pl.BlockSpec((None, m2, DFT_ROWS, tc), lambda b, i, j: (b, 0, i, j)),
                  tok, tok,
                  pl.BlockSpec((1, 1, tc), lambda b, i, j: (0, 0, j))],
        out_specs=tok,
        out_shape=jax.ShapeDtypeStruct((nb, kh, n2, c), BF16),
        scratch_shapes=[pltpu.VMEM((kh, DFT_ROWS, tc), F32)],
        compiler_params=_cparams("parallel", "parallel", "arbitrary"),
        name="dft_out",
    )(f1i, z4, x0, w, skip.reshape(1, 1, c).astype(F32))


def _rope_tables(seq):
    hd = ATTN_HEAD_DIM
    axis_dim = hd // 2
    t = jnp.arange(seq)
    row = (t // GRID_W).astype(F32)
    col = (t % GRID_W).astype(F32)
    inv_freq = ROPE_THETA ** (-jnp.arange(0, axis_dim, 2, dtype=F32) / axis_dim)
    ang = jnp.concatenate([row[:, None] * inv_freq, col[:, None] * inv_freq], axis=-1)
    c, s = jnp.cos(ang), jnp.sin(ang)
    return jnp.concatenate([c, c], axis=-1), jnp.concatenate([-s, s], axis=-1)


def _hyena_features(seq, d):
    t = jnp.linspace(0.0, 1.0, seq, dtype=F32)
    w = 2.0 * math.pi * jnp.arange(seq, dtype=F32) / seq
    f = jnp.linspace(1e-4, HY_BANDS - 1, HY_BANDS, dtype=F32)
    fw = w[:, None] * f[None, :]
    z = jnp.concatenate([t[:, None], jnp.cos(fw), -jnp.sin(fw)], axis=-1)
    z = jnp.pad(z, ((0, 0), (0, LANES - HY_EMB)))
    deltas = jnp.abs(jnp.linspace(math.log(HY_TARGET) / HY_SLOW_PCT,
                                  math.log(HY_TARGET) / HY_FAST_PCT, d, dtype=F32))
    return z, t[:, None], deltas[None, :]


def _dft_tables(seq):
    n = 2 * seq
    n2 = DFT_N2
    n1 = n // n2

    def cs(phase_int, mod):
        ang = (-2.0 * math.pi / mod) * (phase_int % mod).astype(F32)
        return jnp.cos(ang), jnp.sin(ang)

    k1 = jnp.arange(n1)[:, None]
    m1 = jnp.arange(n1 // 2)[None, :]
    fr, fi = cs(k1 * m1, n1)
    f1 = jnp.concatenate([fr, fi], axis=0).astype(BF16)
    f1i = (jnp.concatenate([fr.T, fi.T], axis=1) / n).astype(BF16)
    kk = (jnp.arange(n1)[:, None, None] + n1 * jnp.arange(n2)[None, :, None])
    nn = jnp.arange(n2)[None, None, :]
    hr, hi = cs(kk * nn, n)
    hmat = jnp.concatenate([jnp.concatenate([hr, -hi], axis=2),
                            jnp.concatenate([hi, hr], axis=2)], axis=1).astype(BF16)
    gr, gi = jnp.swapaxes(hr, 1, 2), -jnp.swapaxes(hi, 1, 2)
    gmat = jnp.concatenate([jnp.concatenate([gr, -gi], axis=2),
                            jnp.concatenate([gi, gr], axis=2)], axis=1).astype(BF16)
    return f1, f1i, hmat, gmat


def _deinterleave(nheads):
    hd = ATTN_HEAD_DIM
    one = np.concatenate([np.arange(0, hd, 2), np.arange(1, hd, 2)])
    return np.concatenate([h * hd + one for h in range(nheads)])


def kernel(x_prompt, x_sample, mem_prompt, mem_sample, norm_mix, norm_xa, norm_mem, norm_ffn, xa_wq, xa_wk, xa_wv, xa_wo, ffn_w_in, ffn_conv_w, ffn_conv_b, ffn_w_out, mix_w_in, mix_w_out, ssd_conv_w, ssd_conv_b, ssd_a_log, ssd_dt_bias, ssd_d, ssd_norm, attn_q_norm, attn_k_norm, hy_w_in, hy_conv_w, hy_conv_b, hy_f_w1, hy_f_b1, hy_f_w2, hy_f_b2, hy_f_w3, hy_f_b3, hy_f_freq, hy_f_w_out, hy_skip, hy_w_out, final_norm):
    nbp, seq, d = x_prompt.shape
    nbs = x_sample.shape[0]
    assert x_sample.shape[1] == seq
    nb = nbp + nbs
    T = nb * seq
    depth = norm_mix.shape[0]
    n_mem = mem_prompt.shape[1]
    d_ff = ffn_w_out.shape[1]

    x = jnp.concatenate([x_prompt, x_sample], axis=0).reshape(T, d)
    mem = jnp.concatenate([mem_prompt, mem_sample], axis=0).reshape(nb * n_mem, d)

    d_ssd = d
    nheads = d_ssd // SSD_HEAD_DIM
    gn = SSD_GROUPS * SSD_STATE
    conv_ch = d_ssd + 2 * gn
    n_att = d // ATTN_HEAD_DIM
    d_kv = ATTN_KV_HEADS * ATTN_HEAD_DIM
    o1 = d_ssd
    o2 = o1 + conv_ch
    o3 = o2 + 2 * nheads
    o4 = o3 + d
    o5 = o4 + d_kv
    qcol = o1
    vcol = o1 + d + d_kv
    cos, sin = _rope_tables(seq)
    perm_q = _deinterleave(n_att)
    perm_k = _deinterleave(ATTN_KV_HEADS)
    perm_h = _deinterleave(1)

    n2 = DFT_N2
    n1 = 2 * seq // n2
    hz, t_col, deltas = _hyena_features(seq, d)
    f1, f1i, hmat, gmat = _dft_tables(seq)

    for i in range(depth):
        if i % 2 == 0:
            e = i // 2
            w = mix_w_in[e]
            w_main = jnp.concatenate([w[:, :o1], w[:, o3:o4][:, perm_q], w[:, o4:o5][:, perm_k], w[:, o5:]],
                                     axis=1).astype(BF16)
            w_dt = jnp.pad(w[:, o2:o3], ((0, 0), (0, LANES - 2 * nheads))).astype(BF16)
            proj = normmm(x, norm_mix[i], w_main)
            dtraw = normmm(x, norm_mix[i], w_dt, out_dtype=F32, tn=LANES)
            (xbc,) = normmm_conv(x, norm_mix[i], w[:, o1:o2].astype(BF16), ssd_conv_w[e], ssd_conv_b[e],
                                 [0], conv_ch, _epi_silu, 1, seq)
            pad_row = lambda a: jnp.pad(a.reshape(1, -1).astype(F32), ((0, 0), (0, LANES - 2 * nheads)))
            y_ssd = ssd_scan(xbc, dtraw, pad_row(ssd_dt_bias[e]), pad_row(ssd_a_log[e]),
                             jnp.repeat(ssd_d[e].astype(F32), SSD_HEAD_DIM)[None, :],
                             proj, ssd_norm[e].reshape(1, -1).astype(F32), nb, seq)
            scale = ATTN_HEAD_DIM ** -0.5 * math.log2(math.e)
            gains = jnp.concatenate([jnp.tile(attn_q_norm[e][perm_h][None, :] * scale, (n_att, 1)),
                                     jnp.tile(attn_k_norm[e][perm_h][None, :], (ATTN_KV_HEADS, 1))],
                                    axis=0)[:, None, :].astype(F32)
            qk = qk_prep(proj, qcol, gains, cos, sin, seq)
            y_att = flash_attention(qk, proj, vcol, n_att, nb, seq)
            x = mm_res(jnp.concatenate([y_ssd, y_att], axis=1), mix_w_out[e].astype(BF16), x)
        else:
            o = i // 2
            x0, wv = normmm_conv(x, norm_mix[i], hy_w_in[o].astype(BF16), hy_conv_w[o], hy_conv_b[o],
                                 [0, d, 2 * d], d, _epi_hyena, 2, seq)
            w1 = jnp.pad(hy_f_w1[o], ((0, LANES - HY_EMB), (0, 0)))
            hfb, sums = hyena_filter(hz, t_col, deltas, w1, hy_f_b1[o], hy_f_w2[o], hy_f_b2[o],
                                     hy_f_w3[o], hy_f_b3[o], hy_f_freq[o], hy_f_w_out[o])
            a_f = dft_stage1(f1, hfb.reshape(1, n1 // 2, n2, 2 * d))
            kspec = filter_spectrum(a_f.reshape(2, n1, n2, 2 * d), hmat, sums, d)
            a_u = dft_stage1(f1, wv.reshape(nb, n1 // 2, n2, d))
            zz = dft_mid(a_u.reshape(nb, 2, n1, n2, d), hmat, gmat, kspec)
            yh = dft_out(f1i, zz.reshape(nb, 2 * n1, n2, d), x0.reshape(nb, n1 // 2, n2, d),
                         wv.reshape(nb, n1 // 2, n2, d), hy_skip[o])
            x = mm_res(yh.reshape(T, d), hy_w_out[o].astype(BF16), x)
        q = normmm(x, norm_xa[i], xa_wq[i].astype(BF16))
        kv = normmm(mem, norm_mem[i], jnp.concatenate([xa_wk[i], xa_wv[i]], axis=1).astype(BF16))
        x = mm_res(xattn(q, kv, nb, seq), xa_wo[i].astype(BF16), x)
        (act,) = normmm_conv(x, norm_ffn[i], ffn_w_in[i].astype(BF16), ffn_conv_w[i], ffn_conv_b[i],
                             [0, d_ff], d_ff, _epi_glu, 1, seq)
        x = mm_res(act, ffn_w_out[i].astype(BF16), x)

    y = rmsnorm(x, final_norm).reshape(nb, seq, d)
    return (y[:nbp], y[nbp:])
```

```python
import functools
import math

import numpy as np
import jax
import jax.numpy as jnp
from jax import lax
from jax.experimental import pallas as pl
from jax.experimental.pallas import tpu as pltpu

F32 = jnp.float32
BF16 = jnp.bfloat16
EPS = 1e-6

GRID_W = 64
XA_HEADS = 4
SSD_HEAD_DIM = 64
SSD_GROUPS = 4
SSD_STATE = 128
SSD_CHUNK = 128
ATTN_HEAD_DIM = 128
ATTN_KV_HEADS = 4
ROPE_THETA = 10000.0
HY_EMB = 33
HY_BANDS = (HY_EMB - 1) // 2
HY_TARGET = 1e-2
HY_FAST_PCT = 0.3
HY_SLOW_PCT = 1.5

LANES = 128
MXU_WIDTH = 256
DFT_N2 = 128
VMEM_LIMIT = 52 * 1024 * 1024
NEG_BIG = -1e30


def _cparams(*sem):
    return pltpu.CompilerParams(dimension_semantics=sem, vmem_limit_bytes=VMEM_LIMIT)


def _tile(dim, pref):
    t = min(dim, pref)
    while dim % t:
        t //= 2
    return t


def _split3(x):
    hi = x.astype(BF16)
    r1 = x - hi.astype(F32)
    mid = r1.astype(BF16)
    lo = (r1 - mid.astype(F32)).astype(BF16)
    return hi, mid, lo


def _dot(a, b):
    return jnp.dot(a, b, preferred_element_type=F32)


def _dot_exact_rhs(x, e):
    hi, mid, lo = _split3(x)
    return _dot(hi, e) + _dot(mid, e) + _dot(lo, e)


def _dot_exact_lhs(e, x):
    hi, mid, lo = _split3(x)
    return _dot(e, hi) + _dot(e, mid) + _dot(e, lo)


def _dot_f32(a, b):
    ah, am, _ = _split3(a)
    bh, bm, _ = _split3(b)
    return _dot(ah, bh) + _dot(ah, bm) + _dot(am, bh)


def _silu(x):
    return x * (1.0 / (1.0 + jnp.exp(-x)))


def _normmm_kernel(x_ref, g_ref, w_ref, o_ref, xn_ref):
    @pl.when(pl.program_id(1) == 0)
    def _():
        x = x_ref[...].astype(F32)
        ms = jnp.mean(x * x, axis=-1, keepdims=True)
        xn_ref[...] = (x * lax.rsqrt(ms + EPS) * g_ref[...]).astype(BF16)

    o_ref[...] = _dot(xn_ref[...], w_ref[...]).astype(o_ref.dtype)


def normmm(x, g, w, out_dtype=BF16, tm=1024, tn=1024):
    M, K = x.shape
    N = w.shape[1]
    tm = _tile(M, tm)
    tn = _tile(N, tn)
    return pl.pallas_call(
        _normmm_kernel,
        grid=(M // tm, N // tn),
        in_specs=[pl.BlockSpec((tm, K), lambda i, j: (i, 0)),
                  pl.BlockSpec((1, K), lambda i, j: (0, 0)),
                  pl.BlockSpec((K, tn), lambda i, j: (0, j))],
        out_specs=pl.BlockSpec((tm, tn), lambda i, j: (i, j)),
        out_shape=jax.ShapeDtypeStruct((M, N), out_dtype),
        scratch_shapes=[pltpu.VMEM((tm, K), BF16)],
        compiler_params=_cparams("parallel", "arbitrary"),
        name="normmm",
    )(x, g.reshape(1, K).astype(F32), w)


def _mmres_kernel(a_ref, w_ref, r_ref, o_ref):
    o_ref[...] = r_ref[...] + _dot(a_ref[...], w_ref[...])


def mm_res(a, w, res, tm=1024):
    M, K = a.shape
    N = w.shape[1]
    tm = _tile(M, tm)
    tn = _tile(N, 1024 if K <= 2048 else 512)
    return pl.pallas_call(
        _mmres_kernel,
        grid=(M // tm, N // tn),
        in_specs=[pl.BlockSpec((tm, K), lambda i, j: (i, 0)),
                  pl.BlockSpec((K, tn), lambda i, j: (0, j)),
                  pl.BlockSpec((tm, tn), lambda i, j: (i, j))],
        out_specs=pl.BlockSpec((tm, tn), lambda i, j: (i, j)),
        out_shape=jax.ShapeDtypeStruct((M, N), F32),
        compiler_params=_cparams("parallel", "arbitrary"),
        name="mm_res",
    )(a, w, res)


def _rmsnorm_kernel(x_ref, g_ref, o_ref):
    x = x_ref[...]
    ms = jnp.mean(x * x, axis=-1, keepdims=True)
    o_ref[...] = x * lax.rsqrt(ms + EPS) * g_ref[...]


def rmsnorm(x, g, tm=512):
    M, K = x.shape
    tm = _tile(M, tm)
    return pl.pallas_call(
        _rmsnorm_kernel,
        grid=(M // tm,),
        in_specs=[pl.BlockSpec((tm, K), lambda i: (i, 0)),
                  pl.BlockSpec((1, K), lambda i: (0, 0))],
        out_specs=pl.BlockSpec((tm, K), lambda i: (i, 0)),
        out_shape=jax.ShapeDtypeStruct((M, K), F32),
        compiler_params=_cparams("parallel"),
        name="final_norm",
    )(x, g.reshape(1, K).astype(F32))


CONV_HALO = 16
CONV_ROWS = 16


def _normmm_conv_kernel(*refs, nseg, width, tm, seq, epilogue, nout, ncol):
    xm_ref, xp_ref, xn_ref, g_ref = refs[:4]
    segs = [refs[4 + 3 * s:7 + 3 * s] for s in range(nseg)]
    outs = refs[4 + 3 * nseg:4 + 3 * nseg + nout]
    hn_ref = refs[4 + 3 * nseg + nout]
    exts = refs[5 + 3 * nseg + nout:]
    slots = [exts[0:nseg], exts[nseg:2 * nseg]]
    h = CONV_HALO
    half = width // 2
    row0 = pl.program_id(0) * tm
    at_start = (row0 % seq) == 0
    at_end = ((row0 + tm) % seq) == 0
    j = pl.program_id(1)

    @pl.when(j == 0)
    def _():
        def nrm(x):
            ms = jnp.mean(x * x, axis=-1, keepdims=True)
            return (x * lax.rsqrt(ms + EPS) * g_ref[...]).astype(BF16)
        hn_ref[0:h, :] = nrm(xp_ref[...])
        hn_ref[h:h + tm, :] = nrm(xm_ref[...])
        hn_ref[h + tm:h + tm + h, :] = nrm(xn_ref[...])

    tn = slots[0][0].shape[1]
    pieces = [(s, c) for s in range(nseg) for c in range(0, tn, MXU_WIDTH)]
    nchunk = len(pieces) if tm % (CONV_HALO * len(pieces)) == 0 else 4
    rc = tm // nchunk

    def project(slot, s, c):
        ext = slots[slot][s]
        cs = slice(c, c + MXU_WIDTH)
        ext[:, cs] = _dot(hn_ref[...], segs[s][0][:, cs])
        ext[0:h, cs] = jnp.where(at_start, 0.0, ext[0:h, cs])
        ext[h + tm:h + tm + h, cs] = jnp.where(at_end, 0.0, ext[h + tm:h + tm + h, cs])

    def convolve(slot, r0):
        for rb in range(r0, r0 + rc, CONV_ROWS):
            vals = []
            for (_, cw_ref, cb_ref), ext in zip(segs, slots[slot]):
                acc = None
                for k in range(width):
                    term = ext[h - half + k + rb:h - half + k + rb + CONV_ROWS, :] * cw_ref[k:k + 1, :]
                    acc = term if acc is None else acc + term
                vals.append(acc + cb_ref[...])
            for o, r in zip(outs, epilogue(*vals)):
                o[rb:rb + CONV_ROWS, :] = r.astype(o.dtype)

    def step(pslot, cslot):
        for idx in range(max(len(pieces), nchunk)):
            if cslot is not None and idx < nchunk:
                convolve(cslot, idx * rc)
            if pslot is not None and idx < len(pieces):
                project(pslot, *pieces[idx])

    even = (j % 2) == 0

    @pl.when(j == 0)
    def _():
        step(0, None)

    @pl.when(jnp.logical_and(even, jnp.logical_and(j > 0, j < ncol)))
    def _():
        step(0, 1)

    @pl.when(jnp.logical_and(jnp.logical_not(even), j < ncol))
    def _():
        step(1, 0)

    @pl.when(j == ncol)
    def _():
        step(None, (ncol - 1) % 2)


def normmm_conv(x, g, w, conv_w, conv_b, seg_cols, width_cols, epilogue, nout, seq, tm=1024, tn=512):
    T, K = x.shape
    width = conv_w.shape[0]
    tm = _tile(seq, tm if len(seg_cols) <= 2 else tm // 2)
    tn = _tile(width_cols, tn)
    h = CONV_HALO
    nrb = T // h
    cb = conv_b.reshape(1, -1).astype(F32)
    cw = conv_w.astype(F32)
    in_specs = [pl.BlockSpec((tm, K), lambda i, j: (i, 0)),
                pl.BlockSpec((h, K), lambda i, j: (jnp.maximum(i * (tm // h) - 1, 0), 0)),
                pl.BlockSpec((h, K), lambda i, j: (jnp.minimum((i + 1) * (tm // h), nrb - 1), 0)),
                pl.BlockSpec((1, K), lambda i, j: (0, 0))]
    args = [x, x, x, g.reshape(1, K).astype(F32)]
    ncol = width_cols // tn
    for c0 in seg_cols:
        off = c0 // tn
        in_specs += [pl.BlockSpec((K, tn), lambda i, j, off=off: (0, jnp.minimum(j, ncol - 1) + off)),
                     pl.BlockSpec((width, tn), lambda i, j, off=off: (0, jnp.maximum(j - 1, 0) + off)),
                     pl.BlockSpec((1, tn), lambda i, j, off=off: (0, jnp.maximum(j - 1, 0) + off))]
        args += [w, cw, cb]
    kern = functools.partial(_normmm_conv_kernel, nseg=len(seg_cols), width=width, tm=tm, seq=seq,
                             epilogue=epilogue, nout=nout, ncol=ncol)
    return pl.pallas_call(
        kern,
        grid=(T // tm, ncol + 1),
        in_specs=in_specs,
        out_specs=[pl.BlockSpec((tm, tn), lambda i, j: (i, jnp.maximum(j - 1, 0))) for _ in range(nout)],
        out_shape=[jax.ShapeDtypeStruct((T, width_cols), BF16) for _ in range(nout)],
        scratch_shapes=[pltpu.VMEM((tm + 2 * h, K), BF16)]
        + [pltpu.VMEM((tm + 2 * h, tn), F32) for _ in range(2 * len(seg_cols))],
        compiler_params=_cparams("parallel", "arbitrary"),
        name="normmm_conv",
    )(*args)


def _epi_silu(c):
    return (_silu(c),)


def _epi_glu(g, up):
    return (_silu(g) * up,)


def _epi_hyena(x0, x1, v):
    return (x0, v * x1)


def _softplus(x):
    return jnp.maximum(x, 0.0) + jnp.log(1.0 + jnp.exp(-jnp.abs(x)))


def _ssd_kernel(*refs, rev, nheads):
    if rev:
        (xs_ref, b_ref, c_ref, dt_ref, bias_ref, alog_ref, e_ref,
         yf_ref, z_ref, gain_ref, o_ref, s_ref, y_ref) = refs
    else:
        (xs_ref, b_ref, c_ref, dt_ref, bias_ref, alog_ref, e_ref,
         dskip_ref, o_ref, s_ref) = refs
        y_ref = o_ref
    Q = SSD_CHUNK
    P = SSD_HEAD_DIM
    hpg = nheads // SSD_GROUPS
    gw = hpg * P
    hoff = nheads if rev else 0

    @pl.when(pl.program_id(1) == 0)
    def _():
        s_ref[...] = jnp.zeros_like(s_ref)

    row = lax.broadcasted_iota(jnp.int32, (Q, Q), 0)
    col = lax.broadcasted_iota(jnp.int32, (Q, Q), 1)
    mask = (col >= row) if rev else (col <= row)
    tri = jnp.where(mask, 1.0, 0.0).astype(BF16)

    dtv = _softplus(dt_ref[...] + bias_ref[...])
    a_row = -jnp.exp(alog_ref[...])
    la = dtv * a_row
    cs = _dot_exact_lhs(tri, la)
    tot = cs[0:1, :] if rev else cs[Q - 1:Q, :]
    cs_t = cs.T
    dt_t = dtv.T
    e = e_ref[...]
    carry_in = _dot_exact_rhs(jnp.exp(cs), e)
    to_end = _dot_exact_rhs(jnp.exp(tot - cs) * dtv, e)
    dec = carry_in[0:1, :] if rev else carry_in[Q - 1:Q, :]

    xs = xs_ref[...]
    x_state = (xs.astype(F32) * to_end).astype(BF16)
    lane = lax.broadcasted_iota(jnp.int32, (Q, LANES), 1)
    low = lane < P

    for g in range(SSD_GROUPS):
        bg = b_ref[:, g * SSD_STATE:(g + 1) * SSD_STATE]
        cg = c_ref[:, g * SSD_STATE:(g + 1) * SSD_STATE]
        cb = lax.dot_general(cg, bg, (((1,), (1,)), ((), ())), preferred_element_type=F32)
        s_old = s_ref[g]
        y_off = _dot(cg, s_old.astype(BF16)) * carry_in[:, g * gw:(g + 1) * gw]
        s_ref[g] = s_old * dec[:, g * gw:(g + 1) * gw] + lax.dot_general(
            bg, x_state[:, g * gw:(g + 1) * gw], (((0,), (0,)), ((), ())), preferred_element_type=F32)
        for j in range(hpg // 2):
            ws = []
            for hh in range(2):
                hc = hoff + g * hpg + 2 * j + hh
                diff = cs[:, hc:hc + 1] - cs_t[hc:hc + 1, :]
                decay = jnp.exp(jnp.where(mask, diff, NEG_BIG))
                ws.append((cb * decay * dt_t[hc:hc + 1, :]).astype(BF16))
            c0 = g * gw + 2 * j * P
            xp = xs[:, c0:c0 + LANES]
            rhs = jnp.concatenate([jnp.where(low, xp, jnp.zeros_like(xp)),
                                   jnp.where(low, jnp.zeros_like(xp), xp)], axis=0)
            y = _dot(jnp.concatenate(ws, axis=1), rhs) + y_off[:, 2 * j * P:2 * j * P + LANES]
            if not rev:
                y = y + xp.astype(F32) * dskip_ref[:, c0:c0 + LANES]
            y_ref[:, c0:c0 + LANES] = y

    if rev:
        y = y_ref[...] + yf_ref[...]
        gated = y * _silu(z_ref[...].astype(F32))
        ms = jnp.mean(gated * gated, axis=-1, keepdims=True)
        o_ref[...] = (gated * lax.rsqrt(ms + EPS) * gain_ref[...]).astype(o_ref.dtype)


def ssd_scan(xbc, dtraw, bias_row, alog_row, d_row, z_src, gain_row, nb, seq):
    T = xbc.shape[0]
    Q = SSD_CHUNK
    nc = seq // Q
    gn = SSD_GROUPS * SSD_STATE
    hp = xbc.shape[1] - 2 * gn
    nheads = hp // SSD_HEAD_DIM
    hpg = nheads // SSD_GROUPS
    gw = hpg * SSD_HEAD_DIM
    assert hp % gn == 0 and 2 * nheads <= LANES

    def e_mat(off):
        r = np.arange(LANES)[:, None]
        c = np.arange(hp)[None, :]
        return jnp.asarray((r == off + c // SSD_HEAD_DIM).astype(np.float32), dtype=BF16)

    def specs(rev):
        def blk(c):
            return (nc - 1 - c) if rev else c
        return [
            pl.BlockSpec((Q, hp), lambda b, c: (b * nc + blk(c), 0)),
            pl.BlockSpec((Q, gn), lambda b, c: (b * nc + blk(c), hp // gn)),
            pl.BlockSpec((Q, gn), lambda b, c: (b * nc + blk(c), hp // gn + 1)),
            pl.BlockSpec((Q, LANES), lambda b, c: (b * nc + blk(c), 0)),
            pl.BlockSpec((1, LANES), lambda b, c: (0, 0)),
            pl.BlockSpec((1, LANES), lambda b, c: (0, 0)),
            pl.BlockSpec((LANES, hp), lambda b, c: (0, 0)),
        ], (lambda b, c: (b * nc + blk(c), 0))

    in_f, omap_f = specs(False)
    yf = pl.pallas_call(
        functools.partial(_ssd_kernel, rev=False, nheads=nheads),
        grid=(nb, nc),
        in_specs=in_f + [pl.BlockSpec((1, hp), lambda b, c: (0, 0))],
        out_specs=pl.BlockSpec((Q, hp), omap_f),
        out_shape=jax.ShapeDtypeStruct((T, hp), F32),
        scratch_shapes=[pltpu.VMEM((SSD_GROUPS, SSD_STATE, gw), F32)],
        compiler_params=_cparams("parallel", "arbitrary"),
        name="ssd_fwd",
    )(xbc, xbc, xbc, dtraw, bias_row, alog_row, e_mat(0), d_row)
    in_b, omap_b = specs(True)
    return pl.pallas_call(
        functools.partial(_ssd_kernel, rev=True, nheads=nheads),
        grid=(nb, nc),
        in_specs=in_b + [pl.BlockSpec((Q, hp), omap_b),
                         pl.BlockSpec((Q, hp), omap_b),
                         pl.BlockSpec((1, hp), lambda b, c: (0, 0))],
        out_specs=pl.BlockSpec((Q, hp), omap_b),
        out_shape=jax.ShapeDtypeStruct((T, hp), BF16),
        scratch_shapes=[pltpu.VMEM((SSD_GROUPS, SSD_STATE, gw), F32),
                        pltpu.VMEM((Q, hp), F32)],
        compiler_params=_cparams("parallel", "arbitrary"),
        name="ssd_bwd",
    )(xbc, xbc, xbc, dtraw, bias_row, alog_row, e_mat(nheads), yf, z_src, gain_row)


def _qkprep_kernel(q_ref, k_ref, g_ref, cos_ref, sin_ref, o_ref, *, nq, nk):
    hd = ATTN_HEAD_DIM
    cos = cos_ref[...]
    sin = sin_ref[...]
    for h in range(nq + nk):
        src, c0 = (q_ref, h * hd) if h < nq else (k_ref, (h - nq) * hd)
        x = src[:, c0:c0 + hd].astype(F32)
        ms = jnp.mean(x * x, axis=-1, keepdims=True)
        xn = x * lax.rsqrt(ms + EPS) * g_ref[h]
        o_ref[:, h * hd:(h + 1) * hd] = (xn * cos + pltpu.roll(xn, hd // 2, 1) * sin).astype(o_ref.dtype)


def qk_prep(proj, qcol, nq, nk, gains, cos, sin, seq, tq=256):
    T = proj.shape[0]
    hd = ATTN_HEAD_DIM
    tq = _tile(seq, tq)
    spt = seq // tq
    assert qcol % (nq * hd) == 0 and (qcol + nq * hd) % (nk * hd) == 0
    return pl.pallas_call(
        functools.partial(_qkprep_kernel, nq=nq, nk=nk),
        grid=(T // tq,),
        in_specs=[pl.BlockSpec((tq, nq * hd), lambda i: (i, qcol // (nq * hd))),
                  pl.BlockSpec((tq, nk * hd), lambda i: (i, (qcol + nq * hd) // (nk * hd))),
                  pl.BlockSpec((nq + nk, 1, hd), lambda i: (0, 0, 0)),
                  pl.BlockSpec((tq, hd), lambda i: (i % spt, 0)),
                  pl.BlockSpec((tq, hd), lambda i: (i % spt, 0))],
        out_specs=pl.BlockSpec((tq, (nq + nk) * hd), lambda i: (i, 0)),
        out_shape=jax.ShapeDtypeStruct((T, (nq + nk) * hd), BF16),
        compiler_params=_cparams("parallel"),
        name="qk_prep",
    )(proj, proj, gains, cos, sin)


FLASH_ROW_BLOCK = 32


def _flash_kernel(q_ref, k_ref, v_ref, o_ref, qs_ref, va_ref, s0_ref, s1_ref, p_ref, acc_ref,
                  m_ref, al_ref, *, tk, group):
    hd = ATTN_HEAD_DIM
    tq = q_ref.shape[0]
    rows = group * tq
    seq = k_ref.shape[0]
    nk = seq // tk

    @pl.when(pl.program_id(2) == 0)
    def _():
        va_ref[:, 0:hd] = v_ref[...]
        va_ref[:, hd:2 * hd] = jnp.ones((seq, hd), BF16)

    for g in range(group):
        qs_ref[g * tq:(g + 1) * tq, :] = q_ref[:, g * hd:(g + 1) * hd]
    m_ref[...] = jnp.full(m_ref.shape, NEG_BIG, F32)
    acc_ref[...] = jnp.zeros(acc_ref.shape, F32)
    nlc = tk // LANES

    def scores(t, s_ref):
        k0 = pl.multiple_of(jnp.minimum(t, nk - 1) * tk, tk)
        s_ref[...] = lax.dot_general(qs_ref[...], k_ref[pl.ds(k0, tk), :], (((1,), (1,)), ((), ())),
                                     preferred_element_type=F32)

    def update(t, s_ref):
        for r0 in range(0, rows, FLASH_ROW_BLOCK):
            rs = slice(r0, r0 + FLASH_ROW_BLOCK)
            ch = [s_ref[rs, c * LANES:(c + 1) * LANES] for c in range(nlc)]
            mx = ch[0]
            for c in range(1, nlc):
                mx = jnp.maximum(mx, ch[c])
            m_old = m_ref[rs, :]
            m_new = jnp.maximum(m_old, jnp.max(mx, axis=-1, keepdims=True))
            m_ref[rs, :] = m_new
            al_ref[rs, :] = jnp.exp2(m_old - m_new)
            for c in range(nlc):
                p_ref[rs, c * LANES:(c + 1) * LANES] = jnp.exp2(ch[c] - m_new).astype(BF16)
        k0 = pl.multiple_of(t * tk, tk)
        pv = _dot(p_ref[...], va_ref[pl.ds(k0, tk), :])
        al = al_ref[...]
        acc_ref[...] = acc_ref[...] * jnp.concatenate([al, al], axis=1) + pv

    scores(0, s0_ref)

    def body(t2, carry):
        scores(2 * t2 + 1, s1_ref)
        update(2 * t2, s0_ref)
        scores(2 * t2 + 2, s0_ref)
        update(2 * t2 + 1, s1_ref)
        return carry

    lax.fori_loop(0, nk // 2, body, 0)
    o = acc_ref[:, 0:hd] / acc_ref[:, hd:2 * hd]
    for g in range(group):
        o_ref[:, g * hd:(g + 1) * hd] = o[g * tq:(g + 1) * tq, :].astype(o_ref.dtype)


def flash_attention(qk, v_src, v_col0, nq_heads, nb, seq, tq=256, tk=1024):
    T = qk.shape[0]
    hd = ATTN_HEAD_DIM
    nkv = ATTN_KV_HEADS
    group = nq_heads // nkv
    tq = _tile(seq, tq)
    tk = _tile(seq, tk)
    nqt = seq // tq
    return pl.pallas_call(
        functools.partial(_flash_kernel, tk=tk, group=group),
        grid=(nb, nkv, nqt),
        in_specs=[pl.BlockSpec((tq, group * hd), lambda b, h, i: (b * nqt + i, h)),
                  pl.BlockSpec((seq, hd), lambda b, h, i: (b, nq_heads + h)),
                  pl.BlockSpec((seq, hd), lambda b, h, i: (b, v_col0 // hd + h))],
        out_specs=pl.BlockSpec((tq, group * hd), lambda b, h, i: (b * nqt + i, h)),
        out_shape=jax.ShapeDtypeStruct((T, nq_heads * hd), BF16),
        scratch_shapes=[pltpu.VMEM((group * tq, hd), BF16),
                        pltpu.VMEM((seq, 2 * hd), BF16),
                        pltpu.VMEM((group * tq, tk), F32),
                        pltpu.VMEM((group * tq, tk), F32),
                        pltpu.VMEM((group * tq, tk), BF16),
                        pltpu.VMEM((group * tq, 2 * hd), F32),
                        pltpu.VMEM((group * tq, LANES), F32),
                        pltpu.VMEM((group * tq, LANES), F32)],
        compiler_params=_cparams("arbitrary", "arbitrary", "arbitrary"),
        name="flash_attn",
    )(qk, qk, v_src)


def _xattn_kernel(q_ref, kv_ref, o_ref, *, heads):
    d = q_ref.shape[1]
    hd = d // heads
    scale = hd ** -0.5
    for h in range(heads):
        q = q_ref[:, h * hd:(h + 1) * hd]
        k = kv_ref[:, h * hd:(h + 1) * hd]
        v = kv_ref[:, d + h * hd:d + (h + 1) * hd]
        s = lax.dot_general(q, k, (((1,), (1,)), ((), ())), preferred_element_type=F32) * scale
        p = jnp.exp(s - jnp.max(s, axis=-1, keepdims=True))
        l = jnp.sum(p, axis=-1, keepdims=True)
        o = _dot(p.astype(BF16), v) / l
        o_ref[:, h * hd:(h + 1) * hd] = o.astype(o_ref.dtype)


def xattn(q, kv, nb, seq, tq=512):
    T, d = q.shape
    n_mem = kv.shape[0] // nb
    tq = _tile(seq, tq)
    nqt = seq // tq
    return pl.pallas_call(
        functools.partial(_xattn_kernel, heads=XA_HEADS),
        grid=(nb, nqt),
        in_specs=[pl.BlockSpec((tq, d), lambda b, i: (b * nqt + i, 0)),
                  pl.BlockSpec((n_mem, 2 * d), lambda b, i: (b, 0))],
        out_specs=pl.BlockSpec((tq, d), lambda b, i: (b * nqt + i, 0)),
        out_shape=jax.ShapeDtypeStruct((T, d), BF16),
        compiler_params=_cparams("parallel", "arbitrary"),
        name="xattn",
    )(q, kv)


def _hyfilter_kernel(z_ref, t_ref, dl_ref, w1_ref, b1_ref, w2_ref, b2_ref, w3_ref, b3_ref,
                     fr_ref, wo_ref, h_ref, sum_ref, *, tl, d):
    i = pl.program_id(0)
    fr = fr_ref[...]
    h = jnp.sin(fr * (_dot_f32(z_ref[...], w1_ref[...]) + b1_ref[...]))
    h = jnp.sin(fr * (_dot_f32(h, w2_ref[...]) + b2_ref[...]))
    h = jnp.sin(fr * (_dot_f32(h, w3_ref[...]) + b3_ref[...]))
    window = jnp.exp(-t_ref[...] * dl_ref[...])
    rows = lax.broadcasted_iota(jnp.int32, (tl, 1), 0) + i * tl

    @pl.when(i == 0)
    def _():
        sum_ref[...] = jnp.zeros_like(sum_ref)

    for part in range(2):
        hp = _dot_f32(h, wo_ref[:, part * d:(part + 1) * d]) * window
        if part == 1:
            hp = jnp.where(rows == 0, 0.0, hp)
        h_ref[:, part * d:(part + 1) * d] = hp.astype(h_ref.dtype)
        sum_ref[:, part * d:(part + 1) * d] += jnp.sum(jnp.abs(hp), axis=0, keepdims=True)


def hyena_filter(z, t_col, deltas, w1, b1, w2, b2, w3, b3, freq, w_out, tl=256):
    L = z.shape[0]
    d2 = w_out.shape[1]
    d = d2 // 2
    fw = w2.shape[0]
    tl = _tile(L, tl)
    full = lambda a: pl.BlockSpec(a.shape, lambda i: (0,) * a.ndim)
    ops = [w1, b1.reshape(1, fw), w2, b2.reshape(1, fw), w3, b3.reshape(1, fw), freq.reshape(1, fw), w_out]
    return pl.pallas_call(
        functools.partial(_hyfilter_kernel, tl=tl, d=d),
        grid=(L // tl,),
        in_specs=[pl.BlockSpec((tl, z.shape[1]), lambda i: (i, 0)),
                  pl.BlockSpec((tl, 1), lambda i: (i, 0)),
                  full(deltas)] + [full(a) for a in ops],
        out_specs=[pl.BlockSpec((tl, d2), lambda i: (i, 0)),
                   pl.BlockSpec((1, d2), lambda i: (0, 0))],
        out_shape=[jax.ShapeDtypeStruct((L, d2), BF16),
                   jax.ShapeDtypeStruct((1, d2), F32)],
        compiler_params=_cparams("arbitrary"),
        name="hyena_filter",
    )(z, t_col, deltas, *ops)


DFT_ROWS = 16


def _dft1_kernel(f_ref, u_ref, o_ref):
    f = f_ref[...]
    for r in range(DFT_ROWS):
        o_ref[:, r, :] = _dot(f, u_ref[:, r, :]).astype(o_ref.dtype)


def dft_stage1(f1, u4, tc=512):
    nb, kh, n2, c = u4.shape
    m = f1.shape[0]
    tc = _tile(c, tc)
    return pl.pallas_call(
        _dft1_kernel,
        grid=(nb, n2 // DFT_ROWS, c // tc),
        in_specs=[pl.BlockSpec((m, kh), lambda b, i, j: (0, 0)),
                  pl.BlockSpec((None, kh, DFT_ROWS, tc), lambda b, i, j: (b, 0, i, j))],
        out_specs=pl.BlockSpec((None, m, DFT_ROWS, tc), lambda b, i, j: (b, 0, i, j)),
        out_shape=jax.ShapeDtypeStruct((nb, m, n2, c), BF16),
        compiler_params=_cparams("parallel", "parallel", "arbitrary"),
        name="dft_stage1",
    )(f1, u4)


def _spectrum_kernel(af_ref, ab_ref, h_ref, sum_ref, o_ref, *, d_cols):
    n2 = DFT_N2
    hm = h_ref[...]
    xf = _dot(hm, jnp.concatenate([af_ref[0], af_ref[1]], axis=0))
    xb = _dot(hm, jnp.concatenate([ab_ref[0], ab_ref[1]], axis=0))
    inv = 1.0 / (sum_ref[:, 0:d_cols] + sum_ref[:, d_cols:2 * d_cols])
    o_ref[0] = (xf[:n2] + xb[:n2]) * inv
    o_ref[1] = (xf[n2:] - xb[n2:]) * inv


def filter_spectrum(a5, hmat, sums, d):
    n1 = a5.shape[1]
    n2 = DFT_N2
    return pl.pallas_call(
        functools.partial(_spectrum_kernel, d_cols=d),
        grid=(n1,),
        in_specs=[pl.BlockSpec((2, None, n2, d), lambda k: (0, k, 0, 0)),
                  pl.BlockSpec((2, None, n2, d), lambda k: (0, k, 0, 1)),
                  pl.BlockSpec((None, 2 * n2, 2 * n2), lambda k: (k, 0, 0)),
                  pl.BlockSpec((1, 2 * d), lambda k: (0, 0))],
        out_specs=pl.BlockSpec((2, None, n2, d), lambda k: (0, k, 0, 0)),
        out_shape=jax.ShapeDtypeStruct((2, n1, n2, d), F32),
        compiler_params=_cparams("arbitrary"),
        name="filter_spectrum",
    )(a5, a5, hmat, sums)


def _dftmid_kernel(a_ref, h_ref, g_ref, k_ref, o_ref):
    n2 = DFT_N2
    x = _dot(h_ref[...], jnp.concatenate([a_ref[0], a_ref[1]], axis=0))
    xr, xi = x[:n2], x[n2:]
    kr, ki = k_ref[0], k_ref[1]
    y = jnp.concatenate([xr * kr - xi * ki, xr * ki + xi * kr], axis=0).astype(BF16)
    zz = _dot(g_ref[...], y)
    o_ref[0] = zz[:n2].astype(o_ref.dtype)
    o_ref[1] = zz[n2:].astype(o_ref.dtype)


def dft_mid(a5, hmat, gmat, kspec):
    nb, _, n1, n2, c = a5.shape
    return pl.pallas_call(
        _dftmid_kernel,
        grid=(n1, nb),
        in_specs=[pl.BlockSpec((None, 2, None, n2, c), lambda k, b: (b, 0, k, 0, 0)),
                  pl.BlockSpec((None, 2 * n2, 2 * n2), lambda k, b: (k, 0, 0)),
                  pl.BlockSpec((None, 2 * n2, 2 * n2), lambda k, b: (k, 0, 0)),
                  pl.BlockSpec((2, None, n2, c), lambda k, b: (0, k, 0, 0))],
        out_specs=pl.BlockSpec((None, 2, None, n2, c), lambda k, b: (b, 0, k, 0, 0)),
        out_shape=jax.ShapeDtypeStruct(a5.shape, BF16),
        compiler_params=_cparams("parallel", "arbitrary"),
        name="dft_mid",
    )(a5, hmat, gmat, kspec)


def _dftout_kernel(f_ref, z_ref, x0_ref, w_ref, skip_ref, o_ref, y_ref):
    f = f_ref[...]
    for r in range(DFT_ROWS):
        y_ref[:, r, :] = _dot(f, z_ref[:, r, :])
    w = w_ref[...].astype(F32)
    o_ref[...] = (x0_ref[...].astype(F32) * (y_ref[...] + w * skip_ref[...])).astype(o_ref.dtype)


def dft_out(f1i, z4, x0, w, skip, tc=512):
    nb, m2, n2, c = z4.shape
    kh = f1i.shape[0]
    tc = _tile(c, tc)
    tok = pl.BlockSpec((None, kh, DFT_ROWS, tc), lambda b, i, j: (b, 0, i, j))
    return pl.pallas_call(
        _dftout_kernel,
        grid=(nb, n2 // DFT_ROWS, c // tc),
        in_specs=[pl.BlockSpec((kh, m2), lambda b, i, j: (0, 0)),
                  pl.BlockSpec((None, m2, DFT_ROWS, tc), lambda b, i, j: (b, 0, i, j)),
                  tok, tok,
                  pl.BlockSpec((1, 1, tc), lambda b, i, j: (0, 0, j))],
        out_specs=tok,
        out_shape=jax.ShapeDtypeStruct((nb, kh, n2, c), BF16),
        scratch_shapes=[pltpu.VMEM((kh, DFT_ROWS, tc), F32)],
        compiler_params=_cparams("parallel", "parallel", "arbitrary"),
        name="dft_out",
    )(f1i, z4, x0, w, skip.reshape(1, 1, c).astype(F32))


def _rope_tables(seq):
    hd = ATTN_HEAD_DIM
    axis_dim = hd // 2
    t = jnp.arange(seq)
    row = (t // GRID_W).astype(F32)
    col = (t % GRID_W).astype(F32)
    inv_freq = ROPE_THETA ** (-jnp.arange(0, axis_dim, 2, dtype=F32) / axis_dim)
    ang = jnp.concatenate([row[:, None] * inv_freq, col[:, None] * inv_freq], axis=-1)
    c, s = jnp.cos(ang), jnp.sin(ang)
    return jnp.concatenate([c, c], axis=-1), jnp.concatenate([-s, s], axis=-1)


def _hyena_features(seq, d):
    t = jnp.linspace(0.0, 1.0, seq, dtype=F32)
    w = 2.0 * math.pi * jnp.arange(seq, dtype=F32) / seq
    f = jnp.linspace(1e-4, HY_BANDS - 1, HY_BANDS, dtype=F32)
    fw = w[:, None] * f[None, :]
    z = jnp.concatenate([t[:, None], jnp.cos(fw), -jnp.sin(fw)], axis=-1)
    z = jnp.pad(z, ((0, 0), (0, LANES - HY_EMB)))
    deltas = jnp.abs(jnp.linspace(math.log(HY_TARGET) / HY_SLOW_PCT,
                                  math.log(HY_TARGET) / HY_FAST_PCT, d, dtype=F32))
    return z, t[:, None], deltas[None, :]


def _dft_tables(seq):
    n = 2 * seq
    n2 = DFT_N2
    n1 = n // n2

    def cs(phase_int, mod):
        ang = (-2.0 * math.pi / mod) * (phase_int % mod).astype(F32)
        return jnp.cos(ang), jnp.sin(ang)

    k1 = jnp.arange(n1)[:, None]
    m1 = jnp.arange(n1 // 2)[None, :]
    fr, fi = cs(k1 * m1, n1)
    f1 = jnp.concatenate([fr, fi], axis=0).astype(BF16)
    f1i = (jnp.concatenate([fr.T, fi.T], axis=1) / n).astype(BF16)
    kk = (jnp.arange(n1)[:, None, None] + n1 * jnp.arange(n2)[None, :, None])
    nn = jnp.arange(n2)[None, None, :]
    hr, hi = cs(kk * nn, n)
    hmat = jnp.concatenate([jnp.concatenate([hr, -hi], axis=2),
                            jnp.concatenate([hi, hr], axis=2)], axis=1).astype(BF16)
    gr, gi = jnp.swapaxes(hr, 1, 2), -jnp.swapaxes(hi, 1, 2)
    gmat = jnp.concatenate([jnp.concatenate([gr, -gi], axis=2),
                            jnp.concatenate([gi, gr], axis=2)], axis=1).astype(BF16)
    return f1, f1i, hmat, gmat


def _deinterleave(nheads):
    hd = ATTN_HEAD_DIM
    one = np.concatenate([np.arange(0, hd, 2), np.arange(1, hd, 2)])
    return np.concatenate([h * hd + one for h in range(nheads)])


def kernel(x_prompt, x_sample, mem_prompt, mem_sample, norm_mix, norm_xa, norm_mem, norm_ffn, xa_wq, xa_wk, xa_wv, xa_wo, ffn_w_in, ffn_conv_w, ffn_conv_b, ffn_w_out, mix_w_in, mix_w_out, ssd_conv_w, ssd_conv_b, ssd_a_log, ssd_dt_bias, ssd_d, ssd_norm, attn_q_norm, attn_k_norm, hy_w_in, hy_conv_w, hy_conv_b, hy_f_w1, hy_f_b1, hy_f_w2, hy_f_b2, hy_f_w3, hy_f_b3, hy_f_freq, hy_f_w_out, hy_skip, hy_w_out, final_norm):
    nbp, seq, d = x_prompt.shape
    nbs = x_sample.shape[0]
    assert x_sample.shape[1] == seq
    nb = nbp + nbs
    T = nb * seq
    depth = norm_mix.shape[0]
    n_mem = mem_prompt.shape[1]
    d_ff = ffn_w_out.shape[1]

    x = jnp.concatenate([x_prompt, x_sample], axis=0).reshape(T, d)
    mem = jnp.concatenate([mem_prompt, mem_sample], axis=0).reshape(nb * n_mem, d)

    d_ssd = d
    nheads = d_ssd // SSD_HEAD_DIM
    gn = SSD_GROUPS * SSD_STATE
    conv_ch = d_ssd + 2 * gn
    n_att = d // ATTN_HEAD_DIM
    d_kv = ATTN_KV_HEADS * ATTN_HEAD_DIM
    o1 = d_ssd
    o2 = o1 + conv_ch
    o3 = o2 + 2 * nheads
    o4 = o3 + d
    o5 = o4 + d_kv
    qcol = o1
    vcol = o1 + d + d_kv
    cos, sin = _rope_tables(seq)
    perm_q = _deinterleave(n_att)
    perm_k = _deinterleave(ATTN_KV_HEADS)
    perm_h = _deinterleave(1)

    n2 = DFT_N2
    n1 = 2 * seq // n2
    hz, t_col, deltas = _hyena_features(seq, d)
    f1, f1i, hmat, gmat = _dft_tables(seq)

    for i in range(depth):
        if i % 2 == 0:
            e = i // 2
            w = mix_w_in[e]
            w_main = jnp.concatenate([w[:, :o1], w[:, o3:o4][:, perm_q], w[:, o4:o5][:, perm_k], w[:, o5:]],
                                     axis=1).astype(BF16)
            w_dt = jnp.pad(w[:, o2:o3], ((0, 0), (0, LANES - 2 * nheads))).astype(BF16)
            proj = normmm(x, norm_mix[i], w_main)
            dtraw = normmm(x, norm_mix[i], w_dt, out_dtype=F32, tn=LANES)
            (xbc,) = normmm_conv(x, norm_mix[i], w[:, o1:o2].astype(BF16), ssd_conv_w[e], ssd_conv_b[e],
                                 [0], conv_ch, _epi_silu, 1, seq)
            pad_row = lambda a: jnp.pad(a.reshape(1, -1).astype(F32), ((0, 0), (0, LANES - 2 * nheads)))
            y_ssd = ssd_scan(xbc, dtraw, pad_row(ssd_dt_bias[e]), pad_row(ssd_a_log[e]),
                             jnp.repeat(ssd_d[e].astype(F32), SSD_HEAD_DIM)[None, :],
                             proj, ssd_norm[e].reshape(1, -1).astype(F32), nb, seq)
            scale = ATTN_HEAD_DIM ** -0.5 * math.log2(math.e)
            gains = jnp.concatenate([jnp.tile(attn_q_norm[e][perm_h][None, :] * scale, (n_att, 1)),
                                     jnp.tile(attn_k_norm[e][perm_h][None, :], (ATTN_KV_HEADS, 1))],
                                    axis=0)[:, None, :].astype(F32)
            qk = qk_prep(proj, qcol, n_att, ATTN_KV_HEADS, gains, cos, sin, seq)
            y_att = flash_attention(qk, proj, vcol, n_att, nb, seq)
            x = mm_res(jnp.concatenate([y_ssd, y_att], axis=1), mix_w_out[e].astype(BF16), x)
        else:
            o = i // 2
            x0, wv = normmm_conv(x, norm_mix[i], hy_w_in[o].astype(BF16), hy_conv_w[o], hy_conv_b[o],
                                 [0, d, 2 * d], d, _epi_hyena, 2, seq)
            w1 = jnp.pad(hy_f_w1[o], ((0, LANES - HY_EMB), (0, 0)))
            hfb, sums = hyena_filter(hz, t_col, deltas, w1, hy_f_b1[o], hy_f_w2[o], hy_f_b2[o],
                                     hy_f_w3[o], hy_f_b3[o], hy_f_freq[o], hy_f_w_out[o])
            a_f = dft_stage1(f1, hfb.reshape(1, n1 // 2, n2, 2 * d))
            kspec = filter_spectrum(a_f.reshape(2, n1, n2, 2 * d), hmat, sums, d)
            a_u = dft_stage1(f1, wv.reshape(nb, n1 // 2, n2, d))
            zz = dft_mid(a_u.reshape(nb, 2, n1, n2, d), hmat, gmat, kspec)
            yh = dft_out(f1i, zz.reshape(nb, 2 * n1, n2, d), x0.reshape(nb, n1 // 2, n2, d),
                         wv.reshape(nb, n1 // 2, n2, d), hy_skip[o])
            x = mm_res(yh.reshape(T, d), hy_w_out[o].astype(BF16), x)
        q = normmm(x, norm_xa[i], xa_wq[i].astype(BF16))
        kv = normmm(mem, norm_mem[i], jnp.concatenate([xa_wk[i], xa_wv[i]], axis=1).astype(BF16))
        x = mm_res(xattn(q, kv, nb, seq), xa_wo[i].astype(BF16), x)
        (act,) = normmm_conv(x, norm_ffn[i], ffn_w_in[i].astype(BF16), ffn_conv_w[i], ffn_conv_b[i],
                             [0, d_ff], d_ff, _epi_glu, 1, seq)
        x = mm_res(act, ffn_w_out[i].astype(BF16), x)

    y = rmsnorm(x, final_norm).reshape(nb, seq, d)
    return (y[:nbp], y[nbp:])
```

```python
import functools
import math

import numpy as np
import jax
import jax.numpy as jnp
from jax import lax
from jax.experimental import pallas as pl
from jax.experimental.pallas import tpu as pltpu

F32 = jnp.float32
BF16 = jnp.bfloat16
EPS = 1e-6

GRID_W = 64
XA_HEADS = 4
SSD_HEAD_DIM = 64
SSD_GROUPS = 4
SSD_STATE = 128
SSD_CHUNK = 128
ATTN_HEAD_DIM = 128
ATTN_KV_HEADS = 4
ROPE_THETA = 10000.0
HY_EMB = 33
HY_BANDS = (HY_EMB - 1) // 2
HY_TARGET = 1e-2
HY_FAST_PCT = 0.3
HY_SLOW_PCT = 1.5

LANES = 128
MXU_WIDTH = 256
DFT_N2 = 128
VMEM_LIMIT = 52 * 1024 * 1024
NEG_BIG = -1e30


def _cparams(*sem):
    return pltpu.CompilerParams(dimension_semantics=sem, vmem_limit_bytes=VMEM_LIMIT)


def _tile(dim, pref):
    t = min(dim, pref)
    while dim % t:
        t //= 2
    return t


def _split3(x):
    hi = x.astype(BF16)
    r1 = x - hi.astype(F32)
    mid = r1.astype(BF16)
    lo = (r1 - mid.astype(F32)).astype(BF16)
    return hi, mid, lo


def _dot(a, b):
    return jnp.dot(a, b, preferred_element_type=F32)


def _dot_exact_rhs(x, e):
    hi, mid, lo = _split3(x)
    return _dot(hi, e) + _dot(mid, e) + _dot(lo, e)


def _dot_exact_lhs(e, x):
    hi, mid, lo = _split3(x)
    return _dot(e, hi) + _dot(e, mid) + _dot(e, lo)


def _dot_f32(a, b):
    ah, am, _ = _split3(a)
    bh, bm, _ = _split3(b)
    return _dot(ah, bh) + _dot(ah, bm) + _dot(am, bh)


def _silu(x):
    return x * (1.0 / (1.0 + jnp.exp(-x)))


def _normmm_kernel(x_ref, g_ref, w_ref, o_ref, xn_ref):
    @pl.when(pl.program_id(1) == 0)
    def _():
        x = x_ref[...].astype(F32)
        ms = jnp.mean(x * x, axis=-1, keepdims=True)
        xn_ref[...] = (x * lax.rsqrt(ms + EPS) * g_ref[...]).astype(BF16)

    o_ref[...] = _dot(xn_ref[...], w_ref[...]).astype(o_ref.dtype)


def normmm(x, g, w, out_dtype=BF16, tm=1024, tn=1024):
    M, K = x.shape
    N = w.shape[1]
    tm = _tile(M, tm)
    tn = _tile(N, tn)
    return pl.pallas_call(
        _normmm_kernel,
        grid=(M // tm, N // tn),
        in_specs=[pl.BlockSpec((tm, K), lambda i, j: (i, 0)),
                  pl.BlockSpec((1, K), lambda i, j: (0, 0)),
                  pl.BlockSpec((K, tn), lambda i, j: (0, j))],
        out_specs=pl.BlockSpec((tm, tn), lambda i, j: (i, j)),
        out_shape=jax.ShapeDtypeStruct((M, N), out_dtype),
        scratch_shapes=[pltpu.VMEM((tm, K), BF16)],
        compiler_params=_cparams("parallel", "arbitrary"),
        name="normmm",
    )(x, g.reshape(1, K).astype(F32), w)


def _mmres_kernel(a_ref, w_ref, r_ref, o_ref):
    o_ref[...] = r_ref[...] + _dot(a_ref[...].astype(BF16), w_ref[...])


def mm_res(a, w, res, tm=1024):
    M, K = a.shape
    N = w.shape[1]
    tm = _tile(M, tm)
    tn = _tile(N, 1024 if K <= 2048 else 512)
    return pl.pallas_call(
        _mmres_kernel,
        grid=(M // tm, N // tn),
        in_specs=[pl.BlockSpec((tm, K), lambda i, j: (i, 0)),
                  pl.BlockSpec((K, tn), lambda i, j: (0, j)),
                  pl.BlockSpec((tm, tn), lambda i, j: (i, j))],
        out_specs=pl.BlockSpec((tm, tn), lambda i, j: (i, j)),
        out_shape=jax.ShapeDtypeStruct((M, N), F32),
        compiler_params=_cparams("parallel", "arbitrary"),
        name="mm_res",
    )(a, w, res)


def _rmsnorm_kernel(x_ref, g_ref, o_ref):
    x = x_ref[...]
    ms = jnp.mean(x * x, axis=-1, keepdims=True)
    o_ref[...] = x * lax.rsqrt(ms + EPS) * g_ref[...]


def rmsnorm(x, g, tm=512):
    M, K = x.shape
    tm = _tile(M, tm)
    return pl.pallas_call(
        _rmsnorm_kernel,
        grid=(M // tm,),
        in_specs=[pl.BlockSpec((tm, K), lambda i: (i, 0)),
                  pl.BlockSpec((1, K), lambda i: (0, 0))],
        out_specs=pl.BlockSpec((tm, K), lambda i: (i, 0)),
        out_shape=jax.ShapeDtypeStruct((M, K), F32),
        compiler_params=_cparams("parallel"),
        name="final_norm",
    )(x, g.reshape(1, K).astype(F32))


CONV_HALO = 16
CONV_ROWS = 64


def _normmm_conv_kernel(*refs, nseg, width, tm, seq, epilogue, nout, ncol, lag):
    xm_ref, xp_ref, xn_ref, g_ref = refs[:4]
    segs = [refs[4 + 3 * s:7 + 3 * s] for s in range(nseg)]
    outs = refs[4 + 3 * nseg:4 + 3 * nseg + nout]
    hn_ref = refs[4 + 3 * nseg + nout]
    exts = refs[5 + 3 * nseg + nout:]
    slots = [exts[k * nseg:(k + 1) * nseg] for k in range(1 + lag)]
    h = CONV_HALO
    half = width // 2
    row0 = pl.program_id(0) * tm
    at_start = (row0 % seq) == 0
    at_end = ((row0 + tm) % seq) == 0
    j = pl.program_id(1)

    @pl.when(j == 0)
    def _():
        def nrm(x):
            ms = jnp.mean(x * x, axis=-1, keepdims=True)
            return (x * lax.rsqrt(ms + EPS) * g_ref[...]).astype(BF16)
        hn_ref[0:h, :] = nrm(xp_ref[...])
        hn_ref[h:h + tm, :] = nrm(xm_ref[...])
        hn_ref[h + tm:h + tm + h, :] = nrm(xn_ref[...])

    tn = slots[0][0].shape[1]
    pieces = [(s, c) for s in range(nseg) for c in range(0, tn, MXU_WIDTH)]
    nchunk = len(pieces) if tm % (CONV_HALO * len(pieces)) == 0 else 4
    rc = tm // nchunk

    def project(slot, s, c):
        ext = slots[slot][s]
        cs = slice(c, c + MXU_WIDTH)
        ext[:, cs] = _dot(hn_ref[...], segs[s][0][:, cs])
        ext[0:h, cs] = jnp.where(at_start, 0.0, ext[0:h, cs])
        ext[h + tm:h + tm + h, cs] = jnp.where(at_end, 0.0, ext[h + tm:h + tm + h, cs])

    def convolve(slot, r0):
        for rb in range(r0, r0 + rc, CONV_ROWS):
            for lc in range(0, tn, LANES):
                ls = slice(lc, lc + LANES)
                vals = []
                for (_, cw_ref, cb_ref), ext in zip(segs, slots[slot]):
                    acc = None
                    for k in range(width):
                        term = ext[h - half + k + rb:h - half + k + rb + CONV_ROWS, ls] * cw_ref[k:k + 1, ls]
                        acc = term if acc is None else acc + term
                    vals.append(acc + cb_ref[:, ls])
                for o, r in zip(outs, epilogue(*vals)):
                    o[rb:rb + CONV_ROWS, ls] = r.astype(o.dtype)

    def step(pslot, cslot):
        for idx in range(max(len(pieces), nchunk)):
            if cslot is not None and idx < nchunk:
                convolve(cslot, idx * rc)
            if pslot is not None and idx < len(pieces):
                project(pslot, *pieces[idx])

    if not lag:
        for piece in pieces:
            project(0, *piece)
        for idx in range(nchunk):
            convolve(0, idx * rc)
        return

    even = (j % 2) == 0

    @pl.when(j == 0)
    def _():
        step(0, None)

    @pl.when(jnp.logical_and(even, jnp.logical_and(j > 0, j < ncol)))
    def _():
        step(0, 1)

    @pl.when(jnp.logical_and(jnp.logical_not(even), j < ncol))
    def _():
        step(1, 0)

    @pl.when(j == ncol)
    def _():
        step(None, (ncol - 1) % 2)


def normmm_conv(x, g, w, conv_w, conv_b, seg_cols, width_cols, epilogue, nout, seq, out_dtype=BF16,
                lag=0, tm=1024, tn=512):
    T, K = x.shape
    width = conv_w.shape[0]
    nseg = len(seg_cols)
    tm = _tile(seq, tm if (1 + lag) * nseg <= 4 else tm // 2)
    tn = _tile(width_cols, tn)
    h = CONV_HALO
    nrb = T // h
    cb = conv_b.reshape(1, -1).astype(F32)
    cw = conv_w.astype(F32)
    in_specs = [pl.BlockSpec((tm, K), lambda i, j: (i, 0)),
                pl.BlockSpec((h, K), lambda i, j: (jnp.maximum(i * (tm // h) - 1, 0), 0)),
                pl.BlockSpec((h, K), lambda i, j: (jnp.minimum((i + 1) * (tm // h), nrb - 1), 0)),
                pl.BlockSpec((1, K), lambda i, j: (0, 0))]
    args = [x, x, x, g.reshape(1, K).astype(F32)]
    ncol = width_cols // tn
    for c0 in seg_cols:
        off = c0 // tn
        in_specs += [pl.BlockSpec((K, tn), lambda i, j, off=off: (0, jnp.minimum(j, ncol - 1) + off)),
                     pl.BlockSpec((width, tn), lambda i, j, off=off: (0, jnp.maximum(j - lag, 0) + off)),
                     pl.BlockSpec((1, tn), lambda i, j, off=off: (0, jnp.maximum(j - lag, 0) + off))]
        args += [w, cw, cb]
    kern = functools.partial(_normmm_conv_kernel, nseg=nseg, width=width, tm=tm, seq=seq,
                             epilogue=epilogue, nout=nout, ncol=ncol, lag=lag)
    return pl.pallas_call(
        kern,
        grid=(T // tm, ncol + lag),
        in_specs=in_specs,
        out_specs=[pl.BlockSpec((tm, tn), lambda i, j: (i, jnp.maximum(j - lag, 0))) for _ in range(nout)],
        out_shape=[jax.ShapeDtypeStruct((T, width_cols), out_dtype) for _ in range(nout)],
        scratch_shapes=[pltpu.VMEM((tm + 2 * h, K), BF16)]
        + [pltpu.VMEM((tm + 2 * h, tn), F32) for _ in range((1 + lag) * nseg)],
        compiler_params=_cparams("parallel", "arbitrary"),
        name="normmm_conv",
    )(*args)


def _epi_silu(c):
    return (_silu(c),)


def _epi_glu(g, up):
    return (_silu(g) * up,)


def _epi_hyena(x0, x1, v):
    return (x0, v * x1)


def _softplus(x):
    return jnp.maximum(x, 0.0) + jnp.log(1.0 + jnp.exp(-jnp.abs(x)))


def _ssd_kernel(*refs, rev, nheads):
    if rev:
        (xs_ref, b_ref, c_ref, dt_ref, bias_ref, alog_ref, e_ref,
         yf_ref, z_ref, gain_ref, o_ref, s_ref, y_ref) = refs
    else:
        (xs_ref, b_ref, c_ref, dt_ref, bias_ref, alog_ref, e_ref,
         dskip_ref, o_ref, s_ref) = refs
        y_ref = o_ref
    Q = SSD_CHUNK
    P = SSD_HEAD_DIM
    hpg = nheads // SSD_GROUPS
    gw = hpg * P
    hoff = nheads if rev else 0

    @pl.when(pl.program_id(1) == 0)
    def _():
        s_ref[...] = jnp.zeros_like(s_ref)

    row = lax.broadcasted_iota(jnp.int32, (Q, Q), 0)
    col = lax.broadcasted_iota(jnp.int32, (Q, Q), 1)
    mask = (col >= row) if rev else (col <= row)
    tri = jnp.where(mask, 1.0, 0.0).astype(BF16)

    dtv = _softplus(dt_ref[...] + bias_ref[...])
    a_row = -jnp.exp(alog_ref[...])
    la = dtv * a_row
    cs = _dot_exact_lhs(tri, la)
    tot = cs[0:1, :] if rev else cs[Q - 1:Q, :]
    cs_t = cs.T
    dt_t = dtv.T
    e = e_ref[...]
    carry_in = _dot_exact_rhs(jnp.exp(cs), e)
    to_end = _dot_exact_rhs(jnp.exp(tot - cs) * dtv, e)
    dec = carry_in[0:1, :] if rev else carry_in[Q - 1:Q, :]

    xs = xs_ref[...]
    x_state = (xs.astype(F32) * to_end).astype(BF16)
    lane = lax.broadcasted_iota(jnp.int32, (Q, LANES), 1)
    low = lane < P

    for g in range(SSD_GROUPS):
        bg = b_ref[:, g * SSD_STATE:(g + 1) * SSD_STATE]
        cg = c_ref[:, g * SSD_STATE:(g + 1) * SSD_STATE]
        cb = lax.dot_general(cg, bg, (((1,), (1,)), ((), ())), preferred_element_type=F32)
        s_old = s_ref[g]
        y_off = _dot(cg, s_old.astype(BF16)) * carry_in[:, g * gw:(g + 1) * gw]
        s_ref[g] = s_old * dec[:, g * gw:(g + 1) * gw] + lax.dot_general(
            bg, x_state[:, g * gw:(g + 1) * gw], (((0,), (0,)), ((), ())), preferred_element_type=F32)
        for j in range(hpg // 2):
            ws = []
            for hh in range(2):
                hc = hoff + g * hpg + 2 * j + hh
                diff = cs[:, hc:hc + 1] - cs_t[hc:hc + 1, :]
                decay = jnp.exp(jnp.where(mask, diff, NEG_BIG))
                ws.append((cb * decay * dt_t[hc:hc + 1, :]).astype(BF16))
            c0 = g * gw + 2 * j * P
            xp = xs[:, c0:c0 + LANES]
            rhs = jnp.concatenate([jnp.where(low, xp, jnp.zeros_like(xp)),
                                   jnp.where(low, jnp.zeros_like(xp), xp)], axis=0)
            y = _dot(jnp.concatenate(ws, axis=1), rhs) + y_off[:, 2 * j * P:2 * j * P + LANES]
            if not rev:
                y = y + xp.astype(F32) * dskip_ref[:, c0:c0 + LANES]
            y_ref[:, c0:c0 + LANES] = y

    if rev:
        y = y_ref[...] + yf_ref[...]
        gated = y * _silu(z_ref[...].astype(F32))
        ms = jnp.mean(gated * gated, axis=-1, keepdims=True)
        o_ref[...] = (gated * lax.rsqrt(ms + EPS) * gain_ref[...]).astype(o_ref.dtype)


def ssd_scan(xbc, dtraw, bias_row, alog_row, d_row, z_src, gain_row, nb, seq):
    T = xbc.shape[0]
    Q = SSD_CHUNK
    nc = seq // Q
    gn = SSD_GROUPS * SSD_STATE
    hp = xbc.shape[1] - 2 * gn
    nheads = hp // SSD_HEAD_DIM
    hpg = nheads // SSD_GROUPS
    gw = hpg * SSD_HEAD_DIM
    assert hp % gn == 0 and 2 * nheads <= LANES

    def e_mat(off):
        r = np.arange(LANES)[:, None]
        c = np.arange(hp)[None, :]
        return jnp.asarray((r == off + c // SSD_HEAD_DIM).astype(np.float32), dtype=BF16)

    def specs(rev):
        def blk(c):
            return (nc - 1 - c) if rev else c
        return [
            pl.BlockSpec((Q, hp), lambda b, c: (b * nc + blk(c), 0)),
            pl.BlockSpec((Q, gn), lambda b, c: (b * nc + blk(c), hp // gn)),
            pl.BlockSpec((Q, gn), lambda b, c: (b * nc + blk(c), hp // gn + 1)),
            pl.BlockSpec((Q, LANES), lambda b, c: (b * nc + blk(c), 0)),
            pl.BlockSpec((1, LANES), lambda b, c: (0, 0)),
            pl.BlockSpec((1, LANES), lambda b, c: (0, 0)),
            pl.BlockSpec((LANES, hp), lambda b, c: (0, 0)),
        ], (lambda b, c: (b * nc + blk(c), 0))

    in_f, omap_f = specs(False)
    yf = pl.pallas_call(
        functools.partial(_ssd_kernel, rev=False, nheads=nheads),
        grid=(nb, nc),
        in_specs=in_f + [pl.BlockSpec((1, hp), lambda b, c: (0, 0))],
        out_specs=pl.BlockSpec((Q, hp), omap_f),
        out_shape=jax.ShapeDtypeStruct((T, hp), F32),
        scratch_shapes=[pltpu.VMEM((SSD_GROUPS, SSD_STATE, gw), F32)],
        compiler_params=_cparams("parallel", "arbitrary"),
        name="ssd_fwd",
    )(xbc, xbc, xbc, dtraw, bias_row, alog_row, e_mat(0), d_row)
    in_b, omap_b = specs(True)
    return pl.pallas_call(
        functools.partial(_ssd_kernel, rev=True, nheads=nheads),
        grid=(nb, nc),
        in_specs=in_b + [pl.BlockSpec((Q, hp), omap_b),
                         pl.BlockSpec((Q, hp), omap_b),
                         pl.BlockSpec((1, hp), lambda b, c: (0, 0))],
        out_specs=pl.BlockSpec((Q, hp), omap_b),
        out_shape=jax.ShapeDtypeStruct((T, hp), BF16),
        scratch_shapes=[pltpu.VMEM((SSD_GROUPS, SSD_STATE, gw), F32),
                        pltpu.VMEM((Q, hp), F32)],
        compiler_params=_cparams("parallel", "arbitrary"),
        name="ssd_bwd",
    )(xbc, xbc, xbc, dtraw, bias_row, alog_row, e_mat(nheads), yf, z_src, gain_row)


def _qkprep_kernel(q_ref, k_ref, g_ref, cos_ref, sin_ref, o_ref, *, nq, nk):
    hd = ATTN_HEAD_DIM
    cos = cos_ref[...]
    sin = sin_ref[...]
    for h in range(nq + nk):
        src, c0 = (q_ref, h * hd) if h < nq else (k_ref, (h - nq) * hd)
        x = src[:, c0:c0 + hd].astype(F32)
        ms = jnp.mean(x * x, axis=-1, keepdims=True)
        xn = x * lax.rsqrt(ms + EPS) * g_ref[h]
        o_ref[:, h * hd:(h + 1) * hd] = (xn * cos + pltpu.roll(xn, hd // 2, 1) * sin).astype(o_ref.dtype)


def qk_prep(proj, qcol, nq, nk, gains, cos, sin, seq, tq=256):
    T = proj.shape[0]
    hd = ATTN_HEAD_DIM
    tq = _tile(seq, tq)
    spt = seq // tq
    assert qcol % (nq * hd) == 0 and (qcol + nq * hd) % (nk * hd) == 0
    return pl.pallas_call(
        functools.partial(_qkprep_kernel, nq=nq, nk=nk),
        grid=(T // tq,),
        in_specs=[pl.BlockSpec((tq, nq * hd), lambda i: (i, qcol // (nq * hd))),
                  pl.BlockSpec((tq, nk * hd), lambda i: (i, (qcol + nq * hd) // (nk * hd))),
                  pl.BlockSpec((nq + nk, 1, hd), lambda i: (0, 0, 0)),
                  pl.BlockSpec((tq, hd), lambda i: (i % spt, 0)),
                  pl.BlockSpec((tq, hd), lambda i: (i % spt, 0))],
        out_specs=pl.BlockSpec((tq, (nq + nk) * hd), lambda i: (i, 0)),
        out_shape=jax.ShapeDtypeStruct((T, (nq + nk) * hd), BF16),
        compiler_params=_cparams("parallel"),
        name="qk_prep",
    )(proj, proj, gains, cos, sin)


FLASH_ROW_BLOCK = 32


def _flash_kernel(q_ref, k_ref, v_ref, o_ref, qs_ref, va_ref, s0_ref, s1_ref, p_ref, acc_ref,
                  m_ref, al_ref, *, tk, group):
    hd = ATTN_HEAD_DIM
    tq = q_ref.shape[0]
    rows = group * tq
    seq = k_ref.shape[0]
    nk = seq // tk

    @pl.when(pl.program_id(2) == 0)
    def _():
        va_ref[:, 0:hd] = v_ref[...]
        va_ref[:, hd:2 * hd] = jnp.ones((seq, hd), BF16)

    for g in range(group):
        qs_ref[g * tq:(g + 1) * tq, :] = q_ref[:, g * hd:(g + 1) * hd]
    m_ref[...] = jnp.full(m_ref.shape, NEG_BIG, F32)
    acc_ref[...] = jnp.zeros(acc_ref.shape, F32)
    nlc = tk // LANES

    def scores(t, s_ref):
        k0 = pl.multiple_of(jnp.minimum(t, nk - 1) * tk, tk)
        s_ref[...] = lax.dot_general(qs_ref[...], k_ref[pl.ds(k0, tk), :], (((1,), (1,)), ((), ())),
                                     preferred_element_type=F32)

    def update(t, s_ref):
        for r0 in range(0, rows, FLASH_ROW_BLOCK):
            rs = slice(r0, r0 + FLASH_ROW_BLOCK)
            ch = [s_ref[rs, c * LANES:(c + 1) * LANES] for c in range(nlc)]
            mx = ch[0]
            for c in range(1, nlc):
                mx = jnp.maximum(mx, ch[c])
            m_old = m_ref[rs, :]
            m_new = jnp.maximum(m_old, jnp.max(mx, axis=-1, keepdims=True))
            m_ref[rs, :] = m_new
            al_ref[rs, :] = jnp.exp2(m_old - m_new)
            for c in range(nlc):
                p_ref[rs, c * LANES:(c + 1) * LANES] = jnp.exp2(ch[c] - m_new).astype(BF16)
        k0 = pl.multiple_of(t * tk, tk)
        pv = _dot(p_ref[...], va_ref[pl.ds(k0, tk), :])
        al = al_ref[...]
        acc_ref[...] = acc_ref[...] * jnp.concatenate([al, al], axis=1) + pv

    scores(0, s0_ref)

    def body(t2, carry):
        scores(2 * t2 + 1, s1_ref)
        update(2 * t2, s0_ref)
        scores(2 * t2 + 2, s0_ref)
        update(2 * t2 + 1, s1_ref)
        return carry

    lax.fori_loop(0, nk // 2, body, 0)
    o = acc_ref[:, 0:hd] / acc_ref[:, hd:2 * hd]
    for g in range(group):
        o_ref[:, g * hd:(g + 1) * hd] = o[g * tq:(g + 1) * tq, :].astype(o_ref.dtype)


def flash_attention(qk, v_src, v_col0, nq_heads, nb, seq, tq=256, tk=1024):
    T = qk.shape[0]
    hd = ATTN_HEAD_DIM
    nkv = ATTN_KV_HEADS
    group = nq_heads // nkv
    tq = _tile(seq, tq)
    tk = _tile(seq // 2, tk)
    nqt = seq // tq
    return pl.pallas_call(
        functools.partial(_flash_kernel, tk=tk, group=group),
        grid=(nb, nkv, nqt),
        in_specs=[pl.BlockSpec((tq, group * hd), lambda b, h, i: (b * nqt + i, h)),
                  pl.BlockSpec((seq, hd), lambda b, h, i: (b, nq_heads + h)),
                  pl.BlockSpec((seq, hd), lambda b, h, i: (b, v_col0 // hd + h))],
        out_specs=pl.BlockSpec((tq, group * hd), lambda b, h, i: (b * nqt + i, h)),
        out_shape=jax.ShapeDtypeStruct((T, nq_heads * hd), BF16),
        scratch_shapes=[pltpu.VMEM((group * tq, hd), BF16),
                        pltpu.VMEM((seq, 2 * hd), BF16),
                        pltpu.VMEM((group * tq, tk), F32),
                        pltpu.VMEM((group * tq, tk), F32),
                        pltpu.VMEM((group * tq, tk), BF16),
                        pltpu.VMEM((group * tq, 2 * hd), F32),
                        pltpu.VMEM((group * tq, LANES), F32),
                        pltpu.VMEM((group * tq, LANES), F32)],
        compiler_params=_cparams("arbitrary", "arbitrary", "arbitrary"),
        name="flash_attn",
    )(qk, qk, v_src)


def _xattn_kernel(q_ref, kv_ref, o_ref, *, heads):
    d = q_ref.shape[1]
    hd = d // heads
    scale = hd ** -0.5
    for h in range(heads):
        q = q_ref[:, h * hd:(h + 1) * hd]
        k = kv_ref[:, h * hd:(h + 1) * hd]
        v = kv_ref[:, d + h * hd:d + (h + 1) * hd]
        s = lax.dot_general(q, k, (((1,), (1,)), ((), ())), preferred_element_type=F32) * scale
        p = jnp.exp(s - jnp.max(s, axis=-1, keepdims=True))
        l = jnp.sum(p, axis=-1, keepdims=True)
        o = _dot(p.astype(BF16), v) / l
        o_ref[:, h * hd:(h + 1) * hd] = o.astype(o_ref.dtype)


def xattn(q, kv, nb, seq, tq=512):
    T, d = q.shape
    n_mem = kv.shape[0] // nb
    tq = _tile(seq, tq)
    nqt = seq // tq
    return pl.pallas_call(
        functools.partial(_xattn_kernel, heads=XA_HEADS),
        grid=(nb, nqt),
        in_specs=[pl.BlockSpec((tq, d), lambda b, i: (b * nqt + i, 0)),
                  pl.BlockSpec((n_mem, 2 * d), lambda b, i: (b, 0))],
        out_specs=pl.BlockSpec((tq, d), lambda b, i: (b * nqt + i, 0)),
        out_shape=jax.ShapeDtypeStruct((T, d), BF16),
        compiler_params=_cparams("parallel", "arbitrary"),
        name="xattn",
    )(q, kv)


def _hyfilter_kernel(z_ref, t_ref, dl_ref, w1_ref, b1_ref, w2_ref, b2_ref, w3_ref, b3_ref,
                     fr_ref, wo_ref, h_ref, sum_ref, *, tl, d):
    i = pl.program_id(0)
    fr = fr_ref[...]
    h = jnp.sin(fr * (_dot_f32(z_ref[...], w1_ref[...]) + b1_ref[...]))
    h = jnp.sin(fr * (_dot_f32(h, w2_ref[...]) + b2_ref[...]))
    h = jnp.sin(fr * (_dot_f32(h, w3_ref[...]) + b3_ref[...]))
    window = jnp.exp(-t_ref[...] * dl_ref[...])
    rows = lax.broadcasted_iota(jnp.int32, (tl, 1), 0) + i * tl

    @pl.when(i == 0)
    def _():
        sum_ref[...] = jnp.zeros_like(sum_ref)

    for part in range(2):
        hp = _dot_f32(h, wo_ref[:, part * d:(part + 1) * d]) * window
        if part == 1:
            hp = jnp.where(rows == 0, 0.0, hp)
        h_ref[:, part * d:(part + 1) * d] = hp.astype(h_ref.dtype)
        sum_ref[:, part * d:(part + 1) * d] += jnp.sum(jnp.abs(hp), axis=0, keepdims=True)


def hyena_filter(z, t_col, deltas, w1, b1, w2, b2, w3, b3, freq, w_out, tl=256):
    L = z.shape[0]
    d2 = w_out.shape[1]
    d = d2 // 2
    fw = w2.shape[0]
    tl = _tile(L, tl)
    full = lambda a: pl.BlockSpec(a.shape, lambda i: (0,) * a.ndim)
    ops = [w1, b1.reshape(1, fw), w2, b2.reshape(1, fw), w3, b3.reshape(1, fw), freq.reshape(1, fw), w_out]
    return pl.pallas_call(
        functools.partial(_hyfilter_kernel, tl=tl, d=d),
        grid=(L // tl,),
        in_specs=[pl.BlockSpec((tl, z.shape[1]), lambda i: (i, 0)),
                  pl.BlockSpec((tl, 1), lambda i: (i, 0)),
                  full(deltas)] + [full(a) for a in ops],
        out_specs=[pl.BlockSpec((tl, d2), lambda i: (i, 0)),
                   pl.BlockSpec((1, d2), lambda i: (0, 0))],
        out_shape=[jax.ShapeDtypeStruct((L, d2), F32),
                   jax.ShapeDtypeStruct((1, d2), F32)],
        compiler_params=_cparams("arbitrary"),
        name="hyena_filter",
    )(z, t_col, deltas, *ops)


DFT_ROWS = 8


def _dft1_kernel(g_ref, u_ref, o_ref):
    kh, rt, tc = u_ref.shape
    u = u_ref[...].reshape(kh * rt, tc).astype(BF16)
    a = _dot(g_ref[...], u).astype(BF16)
    o_ref[...] = pltpu.bitcast(a, jnp.uint32).reshape(o_ref.shape)


def dft_stage1(g1, u4, tc=512):
    nb, kh, n2, c = u4.shape
    rt = DFT_ROWS
    n1 = g1.shape[0] // (2 * rt)
    tc = _tile(c, tc)
    return pl.pallas_call(
        _dft1_kernel,
        grid=(nb, n2 // rt, c // tc),
        in_specs=[pl.BlockSpec(g1.shape, lambda b, i, j: (0, 0)),
                  pl.BlockSpec((None, kh, rt, tc), lambda b, i, j: (b, 0, i, j))],
        out_specs=pl.BlockSpec((None, n1, rt, tc), lambda b, i, j: (b, 0, i, j)),
        out_shape=jax.ShapeDtypeStruct((nb, n1, n2, c), jnp.uint32),
        compiler_params=_cparams("parallel", "parallel", "arbitrary"),
        name="dft_stage1",
    )(g1, u4)


def _unpack_complex(ref):
    return pltpu.bitcast(ref[...], BF16)


def _spectrum_kernel(af_ref, ab_ref, h_ref, sum_ref, o_ref, *, d_cols):
    n2 = DFT_N2
    hm = h_ref[...]
    xf = _dot(hm, _unpack_complex(af_ref))
    xb = _dot(hm, _unpack_complex(ab_ref))
    inv = 1.0 / (sum_ref[:, 0:d_cols] + sum_ref[:, d_cols:2 * d_cols])
    o_ref[0] = (xf[:n2] + xb[:n2]) * inv
    o_ref[1] = (xf[n2:] - xb[n2:]) * inv


def filter_spectrum(a4, hmat, sums, d):
    n1 = a4.shape[1]
    n2 = DFT_N2
    return pl.pallas_call(
        functools.partial(_spectrum_kernel, d_cols=d),
        grid=(n1,),
        in_specs=[pl.BlockSpec((None, None, n2, d), lambda k: (0, k, 0, 0)),
                  pl.BlockSpec((None, None, n2, d), lambda k: (0, k, 0, 1)),
                  pl.BlockSpec((None, 2 * n2, 2 * n2), lambda k: (k, 0, 0)),
                  pl.BlockSpec((1, 2 * d), lambda k: (0, 0))],
        out_specs=pl.BlockSpec((2, None, n2, d), lambda k: (0, k, 0, 0)),
        out_shape=jax.ShapeDtypeStruct((2, n1, n2, d), F32),
        compiler_params=_cparams("arbitrary"),
        name="filter_spectrum",
    )(a4, a4, hmat, sums)


def _dftmid_kernel(a_ref, h_ref, g_ref, k_ref, o_ref):
    n2 = DFT_N2
    x = _dot(h_ref[...], _unpack_complex(a_ref))
    xr, xi = x[:n2], x[n2:]
    kr, ki = k_ref[0], k_ref[1]
    y = jnp.concatenate([xr * kr - xi * ki, xr * ki + xi * kr], axis=0).astype(BF16)
    zz = _dot(g_ref[...], y).astype(BF16)
    o_ref[...] = pltpu.bitcast(zz, jnp.uint32)


def dft_mid(a4, hmat, gmat, kspec):
    nb, n1, n2, c = a4.shape
    return pl.pallas_call(
        _dftmid_kernel,
        grid=(n1, nb),
        in_specs=[pl.BlockSpec((None, None, n2, c), lambda k, b: (b, k, 0, 0)),
                  pl.BlockSpec((None, 2 * n2, 2 * n2), lambda k, b: (k, 0, 0)),
                  pl.BlockSpec((None, 2 * n2, 2 * n2), lambda k, b: (k, 0, 0)),
                  pl.BlockSpec((2, None, n2, c), lambda k, b: (0, k, 0, 0))],
        out_specs=pl.BlockSpec((None, None, n2, c), lambda k, b: (b, k, 0, 0)),
        out_shape=jax.ShapeDtypeStruct(a4.shape, jnp.uint32),
        compiler_params=_cparams("parallel", "arbitrary"),
        name="dft_mid",
    )(a4, hmat, gmat, kspec)


def _dftout_kernel(g_ref, z_ref, x0_ref, w_ref, skip_ref, o_ref):
    n1, rt, tc = z_ref.shape
    z = pltpu.bitcast(z_ref[...].reshape(n1 * rt, tc), BF16)
    y = _dot(g_ref[...], z).reshape(o_ref.shape)
    o_ref[...] = x0_ref[...] * (y + w_ref[...] * skip_ref[...])


def dft_out(g2, z4, x0, w, skip, tc=512):
    nb, n1, n2, c = z4.shape
    rt = DFT_ROWS
    kh = g2.shape[0] // rt
    tc = _tile(c, tc)
    tok = pl.BlockSpec((None, kh, rt, tc), lambda b, i, j: (b, 0, i, j))
    return pl.pallas_call(
        _dftout_kernel,
        grid=(nb, n2 // rt, c // tc),
        in_specs=[pl.BlockSpec(g2.shape, lambda b, i, j: (0, 0)),
                  pl.BlockSpec((None, n1, rt, tc), lambda b, i, j: (b, 0, i, j)),
                  tok, tok,
                  pl.BlockSpec((1, 1, tc), lambda b, i, j: (0, 0, j))],
        out_specs=tok,
        out_shape=jax.ShapeDtypeStruct((nb, kh, n2, c), F32),
        compiler_params=_cparams("parallel", "parallel", "arbitrary"),
        name="dft_out",
    )(g2, z4, x0, w, skip.reshape(1, 1, c).astype(F32))


def _rope_tables(seq):
    hd = ATTN_HEAD_DIM
    axis_dim = hd // 2
    t = jnp.arange(seq)
    row = (t // GRID_W).astype(F32)
    col = (t % GRID_W).astype(F32)
    inv_freq = ROPE_THETA ** (-jnp.arange(0, axis_dim, 2, dtype=F32) / axis_dim)
    ang = jnp.concatenate([row[:, None] * inv_freq, col[:, None] * inv_freq], axis=-1)
    c, s = jnp.cos(ang), jnp.sin(ang)
    return jnp.concatenate([c, c], axis=-1), jnp.concatenate([-s, s], axis=-1)


def _hyena_features(seq, d):
    t = jnp.linspace(0.0, 1.0, seq, dtype=F32)
    w = 2.0 * math.pi * jnp.arange(seq, dtype=F32) / seq
    f = jnp.linspace(1e-4, HY_BANDS - 1, HY_BANDS, dtype=F32)
    fw = w[:, None] * f[None, :]
    z = jnp.concatenate([t[:, None], jnp.cos(fw), -jnp.sin(fw)], axis=-1)
    z = jnp.pad(z, ((0, 0), (0, LANES - HY_EMB)))
    deltas = jnp.abs(jnp.linspace(math.log(HY_TARGET) / HY_SLOW_PCT,
                                  math.log(HY_TARGET) / HY_FAST_PCT, d, dtype=F32))
    return z, t[:, None], deltas[None, :]


def _dft_tables(seq):
    n = 2 * seq
    n2 = DFT_N2
    n1 = n // n2

    def cs(phase_int, mod):
        ang = (-2.0 * math.pi / mod) * (phase_int % mod).astype(F32)
        return jnp.cos(ang), jnp.sin(ang)

    k1 = jnp.arange(n1)[:, None]
    m1 = jnp.arange(n1 // 2)[None, :]
    fr, fi = cs(k1 * m1, n1)
    base = jnp.stack([fr, fi], axis=-1)
    eye = jnp.eye(DFT_ROWS, dtype=F32)
    rows = DFT_ROWS
    g1 = jnp.einsum('knp,rs->krpns', base, eye).reshape(n1 * rows * 2, (n1 // 2) * rows).astype(BF16)
    g2 = (jnp.einsum('knp,rs->nrksp', base, eye).reshape((n1 // 2) * rows, n1 * rows * 2) / n).astype(BF16)
    kk = (jnp.arange(n1)[:, None, None] + n1 * jnp.arange(n2)[None, :, None])
    nn = jnp.arange(n2)[None, None, :]
    hr, hi = cs(kk * nn, n)
    hmat = jnp.concatenate([jnp.stack([hr, -hi], axis=-1).reshape(n1, n2, 2 * n2),
                            jnp.stack([hi, hr], axis=-1).reshape(n1, n2, 2 * n2)], axis=1).astype(BF16)
    gr, gi = jnp.swapaxes(hr, 1, 2), -jnp.swapaxes(hi, 1, 2)
    gmat = jnp.stack([jnp.concatenate([gr, -gi], axis=2),
                      jnp.concatenate([gi, gr], axis=2)], axis=2).reshape(n1, 2 * n2, 2 * n2).astype(BF16)
    return g1, g2, hmat, gmat


def _deinterleave(nheads):
    hd = ATTN_HEAD_DIM
    one = np.concatenate([np.arange(0, hd, 2), np.arange(1, hd, 2)])
    return np.concatenate([h * hd + one for h in range(nheads)])


def kernel(x_prompt, x_sample, mem_prompt, mem_sample, norm_mix, norm_xa, norm_mem, norm_ffn, xa_wq, xa_wk, xa_wv, xa_wo, ffn_w_in, ffn_conv_w, ffn_conv_b, ffn_w_out, mix_w_in, mix_w_out, ssd_conv_w, ssd_conv_b, ssd_a_log, ssd_dt_bias, ssd_d, ssd_norm, attn_q_norm, attn_k_norm, hy_w_in, hy_conv_w, hy_conv_b, hy_f_w1, hy_f_b1, hy_f_w2, hy_f_b2, hy_f_w3, hy_f_b3, hy_f_freq, hy_f_w_out, hy_skip, hy_w_out, final_norm):
    nbp, seq, d = x_prompt.shape
    nbs = x_sample.shape[0]
    assert x_sample.shape[1] == seq
    nb = nbp + nbs
    T = nb * seq
    depth = norm_mix.shape[0]
    n_mem = mem_prompt.shape[1]
    d_ff = ffn_w_out.shape[1]

    x = jnp.concatenate([x_prompt, x_sample], axis=0).reshape(T, d)
    mem = jnp.concatenate([mem_prompt, mem_sample], axis=0).reshape(nb * n_mem, d)

    d_ssd = d
    nheads = d_ssd // SSD_HEAD_DIM
    gn = SSD_GROUPS * SSD_STATE
    conv_ch = d_ssd + 2 * gn
    n_att = d // ATTN_HEAD_DIM
    d_kv = ATTN_KV_HEADS * ATTN_HEAD_DIM
    o1 = d_ssd
    o2 = o1 + conv_ch
    o3 = o2 + 2 * nheads
    o4 = o3 + d
    o5 = o4 + d_kv
    qcol = o1
    vcol = o1 + d + d_kv
    cos, sin = _rope_tables(seq)
    perm_q = _deinterleave(n_att)
    perm_k = _deinterleave(ATTN_KV_HEADS)
    perm_h = _deinterleave(1)

    n2 = DFT_N2
    n1 = 2 * seq // n2
    hz, t_col, deltas = _hyena_features(seq, d)
    g1, g2, hmat, gmat = _dft_tables(seq)

    for i in range(depth):
        if i % 2 == 0:
            e = i // 2
            w = mix_w_in[e]
            w_main = jnp.concatenate([w[:, :o1], w[:, o3:o4][:, perm_q], w[:, o4:o5][:, perm_k], w[:, o5:]],
                                     axis=1).astype(BF16)
            w_dt = jnp.pad(w[:, o2:o3], ((0, 0), (0, LANES - 2 * nheads))).astype(BF16)
            proj = normmm(x, norm_mix[i], w_main)
            dtraw = normmm(x, norm_mix[i], w_dt, out_dtype=F32, tn=LANES)
            (xbc,) = normmm_conv(x, norm_mix[i], w[:, o1:o2].astype(BF16), ssd_conv_w[e], ssd_conv_b[e],
                                 [0], conv_ch, _epi_silu, 1, seq, lag=int(i < 2))
            pad_row = lambda a: jnp.pad(a.reshape(1, -1).astype(F32), ((0, 0), (0, LANES - 2 * nheads)))
            y_ssd = ssd_scan(xbc, dtraw, pad_row(ssd_dt_bias[e]), pad_row(ssd_a_log[e]),
                             jnp.repeat(ssd_d[e].astype(F32), SSD_HEAD_DIM)[None, :],
                             proj, ssd_norm[e].reshape(1, -1).astype(F32), nb, seq)
            scale = ATTN_HEAD_DIM ** -0.5 * math.log2(math.e)
            gains = jnp.concatenate([jnp.tile(attn_q_norm[e][perm_h][None, :] * scale, (n_att, 1)),
                                     jnp.tile(attn_k_norm[e][perm_h][None, :], (ATTN_KV_HEADS, 1))],
                                    axis=0)[:, None, :].astype(F32)
            qk = qk_prep(proj, qcol, n_att, ATTN_KV_HEADS, gains, cos, sin, seq)
            y_att = flash_attention(qk, proj, vcol, n_att, nb, seq)
            x = mm_res(jnp.concatenate([y_ssd, y_att], axis=1), mix_w_out[e].astype(BF16), x)
        else:
            o = i // 2
            x0, wv = normmm_conv(x, norm_mix[i], hy_w_in[o].astype(BF16), hy_conv_w[o], hy_conv_b[o],
                                 [0, d, 2 * d], d, _epi_hyena, 2, seq, out_dtype=F32, lag=int(i >= 2))
            w1 = jnp.pad(hy_f_w1[o], ((0, LANES - HY_EMB), (0, 0)))
            hfb, sums = hyena_filter(hz, t_col, deltas, w1, hy_f_b1[o], hy_f_w2[o], hy_f_b2[o],
                                     hy_f_w3[o], hy_f_b3[o], hy_f_freq[o], hy_f_w_out[o])
            a_f = dft_stage1(g1, hfb.reshape(1, n1 // 2, n2, 2 * d))
            kspec = filter_spectrum(a_f, hmat, sums, d)
            a_u = dft_stage1(g1, wv.reshape(nb, n1 // 2, n2, d))
            zz = dft_mid(a_u, hmat, gmat, kspec)
            yh = dft_out(g2, zz, x0.reshape(nb, n1 // 2, n2, d), wv.reshape(nb, n1 // 2, n2, d), hy_skip[o])
            x = mm_res(yh.reshape(T, d), hy_w_out[o].astype(BF16), x)
        q = normmm(x, norm_xa[i], xa_wq[i].astype(BF16))
        kv = normmm(mem, norm_mem[i], jnp.concatenate([xa_wk[i], xa_wv[i]], axis=1).astype(BF16))
        x = mm_res(xattn(q, kv, nb, seq), xa_wo[i].astype(BF16), x)
        (act,) = normmm_conv(x, norm_ffn[i], ffn_w_in[i].astype(BF16), ffn_conv_w[i], ffn_conv_b[i],
                             [0, d_ff], d_ff, _epi_glu, 1, seq, lag=int(i < 2))
        x = mm_res(act, ffn_w_out[i].astype(BF16), x)

    y = rmsnorm(x, final_norm).reshape(nb, seq, d)
    return (y[:nbp], y[nbp:])
```

```python
import functools
import math

import numpy as np
import jax
import jax.numpy as jnp
from jax import lax
from jax.experimental import pallas as pl
from jax.experimental.pallas import tpu as pltpu

F32 = jnp.float32
BF16 = jnp.bfloat16
EPS = 1e-6

GRID_W = 64
XA_HEADS = 4
SSD_HEAD_DIM = 64
SSD_GROUPS = 4
SSD_STATE = 128
SSD_CHUNK = 128
ATTN_HEAD_DIM = 128
ATTN_KV_HEADS = 4
ROPE_THETA = 10000.0
HY_EMB = 33
HY_BANDS = (HY_EMB - 1) // 2
HY_TARGET = 1e-2
HY_FAST_PCT = 0.3
HY_SLOW_PCT = 1.5

LANES = 128
MXU_WIDTH = 256
DFT_N2 = 128
VMEM_LIMIT = 52 * 1024 * 1024
NEG_BIG = -1e30


def _cparams(*sem):
    return pltpu.CompilerParams(dimension_semantics=sem, vmem_limit_bytes=VMEM_LIMIT)


def _tile(dim, pref):
    t = min(dim, pref)
    while dim % t:
        t //= 2
    return t


def _split3(x):
    hi = x.astype(BF16)
    r1 = x - hi.astype(F32)
    mid = r1.astype(BF16)
    lo = (r1 - mid.astype(F32)).astype(BF16)
    return hi, mid, lo


def _dot(a, b):
    return jnp.dot(a, b, preferred_element_type=F32)


def _dot_exact_rhs(x, e):
    hi, mid, lo = _split3(x)
    return _dot(hi, e) + _dot(mid, e) + _dot(lo, e)


def _dot_exact_lhs(e, x):
    hi, mid, lo = _split3(x)
    return _dot(e, hi) + _dot(e, mid) + _dot(e, lo)


def _dot_f32(a, b):
    ah, am, _ = _split3(a)
    bh, bm, _ = _split3(b)
    return _dot(ah, bh) + _dot(ah, bm) + _dot(am, bh)


def _silu(x):
    return x * (1.0 / (1.0 + jnp.exp(-x)))


def _normmm_kernel(x_ref, g_ref, w_ref, o_ref, xn_ref):
    @pl.when(pl.program_id(1) == 0)
    def _():
        x = x_ref[...].astype(F32)
        ms = jnp.mean(x * x, axis=-1, keepdims=True)
        xn_ref[...] = (x * lax.rsqrt(ms + EPS) * g_ref[...]).astype(BF16)

    o_ref[...] = _dot(xn_ref[...], w_ref[...]).astype(o_ref.dtype)


def normmm(x, g, w, out_dtype=BF16, tm=1024, tn=1024):
    M, K = x.shape
    N = w.shape[1]
    tm = _tile(M, tm)
    tn = _tile(N, tn)
    return pl.pallas_call(
        _normmm_kernel,
        grid=(M // tm, N // tn),
        in_specs=[pl.BlockSpec((tm, K), lambda i, j: (i, 0)),
                  pl.BlockSpec((1, K), lambda i, j: (0, 0)),
                  pl.BlockSpec((K, tn), lambda i, j: (0, j))],
        out_specs=pl.BlockSpec((tm, tn), lambda i, j: (i, j)),
        out_shape=jax.ShapeDtypeStruct((M, N), out_dtype),
        scratch_shapes=[pltpu.VMEM((tm, K), BF16)],
        compiler_params=_cparams("parallel", "arbitrary"),
        name="normmm",
    )(x, g.reshape(1, K).astype(F32), w)


def _mmres_kernel(a_ref, w_ref, r_ref, o_ref):
    o_ref[...] = r_ref[...] + _dot(a_ref[...].astype(BF16), w_ref[...])


def mm_res(a, w, res, tm=1024):
    M, K = a.shape
    N = w.shape[1]
    tm = _tile(M, tm)
    tn = _tile(N, 1024 if K <= 2048 else 512)
    return pl.pallas_call(
        _mmres_kernel,
        grid=(M // tm, N // tn),
        in_specs=[pl.BlockSpec((tm, K), lambda i, j: (i, 0)),
                  pl.BlockSpec((K, tn), lambda i, j: (0, j)),
                  pl.BlockSpec((tm, tn), lambda i, j: (i, j))],
        out_specs=pl.BlockSpec((tm, tn), lambda i, j: (i, j)),
        out_shape=jax.ShapeDtypeStruct((M, N), F32),
        compiler_params=_cparams("parallel", "arbitrary"),
        name="mm_res",
    )(a, w, res)


def _rmsnorm_kernel(x_ref, g_ref, o_ref):
    x = x_ref[...]
    ms = jnp.mean(x * x, axis=-1, keepdims=True)
    o_ref[...] = x * lax.rsqrt(ms + EPS) * g_ref[...]


def rmsnorm(x, g, row0, rows, tm=512):
    K = x.shape[1]
    tm = _tile(math.gcd(row0, rows) if row0 else rows, tm)
    return pl.pallas_call(
        _rmsnorm_kernel,
        grid=(rows // tm,),
        in_specs=[pl.BlockSpec((tm, K), lambda i: (i + row0 // tm, 0)),
                  pl.BlockSpec((1, K), lambda i: (0, 0))],
        out_specs=pl.BlockSpec((tm, K), lambda i: (i, 0)),
        out_shape=jax.ShapeDtypeStruct((rows, K), F32),
        compiler_params=_cparams("parallel"),
        name="final_norm",
    )(x, g.reshape(1, K).astype(F32))


CONV_HALO = 16
CONV_ROWS = 64


def _normmm_conv_kernel(*refs, nseg, width, tm, seq, epilogue, nout):
    xm_ref, xp_ref, xn_ref, g_ref = refs[:4]
    segs = [refs[4 + 3 * s:7 + 3 * s] for s in range(nseg)]
    outs = refs[4 + 3 * nseg:4 + 3 * nseg + nout]
    hn_ref = refs[4 + 3 * nseg + nout]
    exts = refs[5 + 3 * nseg + nout:]
    h = CONV_HALO
    half = width // 2
    row0 = pl.program_id(0) * tm
    at_start = (row0 % seq) == 0
    at_end = ((row0 + tm) % seq) == 0

    @pl.when(pl.program_id(1) == 0)
    def _():
        def nrm(x):
            ms = jnp.mean(x * x, axis=-1, keepdims=True)
            return (x * lax.rsqrt(ms + EPS) * g_ref[...]).astype(BF16)
        hn_ref[0:h, :] = nrm(xp_ref[...])
        hn_ref[h:h + tm, :] = nrm(xm_ref[...])
        hn_ref[h + tm:h + tm + h, :] = nrm(xn_ref[...])

    tn = exts[0].shape[1]
    for (w_ref, _, _), ext in zip(segs, exts):
        ext[...] = _dot(hn_ref[...], w_ref[...])
        ext[0:h, :] = jnp.where(at_start, 0.0, ext[0:h, :])
        ext[h + tm:h + tm + h, :] = jnp.where(at_end, 0.0, ext[h + tm:h + tm + h, :])

    for rb in range(0, tm, CONV_ROWS):
        for lc in range(0, tn, LANES):
            ls = slice(lc, lc + LANES)
            vals = []
            for (_, cw_ref, cb_ref), ext in zip(segs, exts):
                acc = None
                for k in range(width):
                    term = ext[h - half + k + rb:h - half + k + rb + CONV_ROWS, ls] * cw_ref[k:k + 1, ls]
                    acc = term if acc is None else acc + term
                vals.append(acc + cb_ref[:, ls])
            for o, r in zip(outs, epilogue(*vals)):
                o[rb:rb + CONV_ROWS, ls] = r.astype(o.dtype)


def normmm_conv(x, g, w, conv_w, conv_b, seg_cols, width_cols, epilogue, nout, seq, out_dtype=BF16,
                tm=1024, tn=512):
    T, K = x.shape
    width = conv_w.shape[0]
    nseg = len(seg_cols)
    tm = _tile(seq, tm)
    tn = _tile(width_cols, tn)
    assert tm % CONV_ROWS == 0 and tn % LANES == 0
    h = CONV_HALO
    nrb = T // h
    cb = conv_b.reshape(1, -1).astype(F32)
    cw = conv_w.astype(F32)
    in_specs = [pl.BlockSpec((tm, K), lambda i, j: (i, 0)),
                pl.BlockSpec((h, K), lambda i, j: (jnp.maximum(i * (tm // h) - 1, 0), 0)),
                pl.BlockSpec((h, K), lambda i, j: (jnp.minimum((i + 1) * (tm // h), nrb - 1), 0)),
                pl.BlockSpec((1, K), lambda i, j: (0, 0))]
    args = [x, x, x, g.reshape(1, K).astype(F32)]
    for c0 in seg_cols:
        off = c0 // tn
        in_specs += [pl.BlockSpec((K, tn), lambda i, j, off=off: (0, j + off)),
                     pl.BlockSpec((width, tn), lambda i, j, off=off: (0, j + off)),
                     pl.BlockSpec((1, tn), lambda i, j, off=off: (0, j + off))]
        args += [w, cw, cb]
    kern = functools.partial(_normmm_conv_kernel, nseg=nseg, width=width, tm=tm, seq=seq,
                             epilogue=epilogue, nout=nout)
    return pl.pallas_call(
        kern,
        grid=(T // tm, width_cols // tn),
        in_specs=in_specs,
        out_specs=[pl.BlockSpec((tm, tn), lambda i, j: (i, j)) for _ in range(nout)],
        out_shape=[jax.ShapeDtypeStruct((T, width_cols), out_dtype) for _ in range(nout)],
        scratch_shapes=[pltpu.VMEM((tm + 2 * h, K), BF16)]
        + [pltpu.VMEM((tm + 2 * h, tn), F32) for _ in range(nseg)],
        compiler_params=_cparams("parallel", "arbitrary"),
        name="normmm_conv",
    )(*args)


def _epi_silu(c):
    return (_silu(c),)


def _epi_glu(g, up):
    return (_silu(g) * up,)


def _epi_hyena(x0, x1, v):
    return (x0, v * x1)


def _softplus(x):
    return jnp.maximum(x, 0.0) + jnp.log(1.0 + jnp.exp(-jnp.abs(x)))


def _ssd_kernel(*refs, rev, nheads):
    if rev:
        (xs_ref, b_ref, c_ref, dt_ref, bias_ref, alog_ref, e_ref,
         yf_ref, z_ref, gain_ref, o_ref, s_ref, y_ref) = refs
    else:
        (xs_ref, b_ref, c_ref, dt_ref, bias_ref, alog_ref, e_ref,
         dskip_ref, o_ref, s_ref) = refs
        y_ref = o_ref
    Q = SSD_CHUNK
    P = SSD_HEAD_DIM
    hpg = nheads // SSD_GROUPS
    gw = hpg * P
    hoff = nheads if rev else 0

    @pl.when(pl.program_id(1) == 0)
    def _():
        s_ref[...] = jnp.zeros_like(s_ref)

    row = lax.broadcasted_iota(jnp.int32, (Q, Q), 0)
    col = lax.broadcasted_iota(jnp.int32, (Q, Q), 1)
    mask = (col >= row) if rev else (col <= row)
    tri = jnp.where(mask, 1.0, 0.0).astype(BF16)

    dtv = _softplus(dt_ref[...] + bias_ref[...])
    a_row = -jnp.exp(alog_ref[...])
    la = dtv * a_row
    cs = _dot_exact_lhs(tri, la)
    tot = cs[0:1, :] if rev else cs[Q - 1:Q, :]
    cs_t = cs.T
    dt_t = dtv.T
    e = e_ref[...]
    ecs_hi, ecs_lo, _ = _split3(jnp.exp(cs))
    carry_in = _dot(ecs_hi, e) + _dot(ecs_lo, e)
    to_end = _dot((jnp.exp(tot - cs) * dtv).astype(BF16), e)
    dec = _dot_exact_rhs(jnp.broadcast_to(jnp.exp(tot), (8, LANES)), e)[0:1, :]

    xs = xs_ref[...]
    x_state = (xs.astype(F32) * to_end).astype(BF16)
    lane = lax.broadcasted_iota(jnp.int32, (Q, LANES), 1)
    low = lane < P

    for g in range(SSD_GROUPS):
        bg = b_ref[:, g * SSD_STATE:(g + 1) * SSD_STATE]
        cg = c_ref[:, g * SSD_STATE:(g + 1) * SSD_STATE]
        cb = lax.dot_general(cg, bg, (((1,), (1,)), ((), ())), preferred_element_type=F32)
        s_old = s_ref[g]
        y_off = _dot(cg, s_old.astype(BF16)) * carry_in[:, g * gw:(g + 1) * gw]
        s_ref[g] = s_old * dec[:, g * gw:(g + 1) * gw] + lax.dot_general(
            bg, x_state[:, g * gw:(g + 1) * gw], (((0,), (0,)), ((), ())), preferred_element_type=F32)
        for j in range(hpg // 2):
            ws = []
            for hh in range(2):
                hc = hoff + g * hpg + 2 * j + hh
                diff = cs[:, hc:hc + 1] - cs_t[hc:hc + 1, :]
                decay = jnp.exp(jnp.where(mask, diff, NEG_BIG))
                ws.append((cb * decay * dt_t[hc:hc + 1, :]).astype(BF16))
            c0 = g * gw + 2 * j * P
            xp = xs[:, c0:c0 + LANES]
            rhs = jnp.concatenate([jnp.where(low, xp, jnp.zeros_like(xp)),
                                   jnp.where(low, jnp.zeros_like(xp), xp)], axis=0)
            y = _dot(jnp.concatenate(ws, axis=1), rhs) + y_off[:, 2 * j * P:2 * j * P + LANES]
            if not rev:
                y = y + xp.astype(F32) * dskip_ref[:, c0:c0 + LANES]
            y_ref[:, c0:c0 + LANES] = y

    if rev:
        y = y_ref[...] + yf_ref[...]
        gated = y * _silu(z_ref[...].astype(F32))
        ms = jnp.mean(gated * gated, axis=-1, keepdims=True)
        o_ref[...] = (gated * lax.rsqrt(ms + EPS) * gain_ref[...]).astype(o_ref.dtype)


def ssd_scan(xbc, dtraw, bias_row, alog_row, d_row, z_src, gain_row, nb, seq):
    T = xbc.shape[0]
    Q = SSD_CHUNK
    nc = seq // Q
    gn = SSD_GROUPS * SSD_STATE
    hp = xbc.shape[1] - 2 * gn
    nheads = hp // SSD_HEAD_DIM
    hpg = nheads // SSD_GROUPS
    gw = hpg * SSD_HEAD_DIM
    assert hp % gn == 0 and 2 * nheads <= LANES

    def e_mat(off):
        r = np.arange(LANES)[:, None]
        c = np.arange(hp)[None, :]
        return jnp.asarray((r == off + c // SSD_HEAD_DIM).astype(np.float32), dtype=BF16)

    def specs(rev):
        def blk(c):
            return (nc - 1 - c) if rev else c
        return [
            pl.BlockSpec((Q, hp), lambda b, c: (b * nc + blk(c), 0)),
            pl.BlockSpec((Q, gn), lambda b, c: (b * nc + blk(c), hp // gn)),
            pl.BlockSpec((Q, gn), lambda b, c: (b * nc + blk(c), hp // gn + 1)),
            pl.BlockSpec((Q, LANES), lambda b, c: (b * nc + blk(c), 0)),
            pl.BlockSpec((1, LANES), lambda b, c: (0, 0)),
            pl.BlockSpec((1, LANES), lambda b, c: (0, 0)),
            pl.BlockSpec((LANES, hp), lambda b, c: (0, 0)),
        ], (lambda b, c: (b * nc + blk(c), 0))

    in_f, omap_f = specs(False)
    yf = pl.pallas_call(
        functools.partial(_ssd_kernel, rev=False, nheads=nheads),
        grid=(nb, nc),
        in_specs=in_f + [pl.BlockSpec((1, hp), lambda b, c: (0, 0))],
        out_specs=pl.BlockSpec((Q, hp), omap_f),
        out_shape=jax.ShapeDtypeStruct((T, hp), F32),
        scratch_shapes=[pltpu.VMEM((SSD_GROUPS, SSD_STATE, gw), F32)],
        compiler_params=_cparams("parallel", "arbitrary"),
        name="ssd_fwd",
    )(xbc, xbc, xbc, dtraw, bias_row, alog_row, e_mat(0), d_row)
    in_b, omap_b = specs(True)
    return pl.pallas_call(
        functools.partial(_ssd_kernel, rev=True, nheads=nheads),
        grid=(nb, nc),
        in_specs=in_b + [pl.BlockSpec((Q, hp), omap_b),
                         pl.BlockSpec((Q, hp), omap_b),
                         pl.BlockSpec((1, hp), lambda b, c: (0, 0))],
        out_specs=pl.BlockSpec((Q, hp), omap_b),
        out_shape=jax.ShapeDtypeStruct((T, hp), BF16),
        scratch_shapes=[pltpu.VMEM((SSD_GROUPS, SSD_STATE, gw), F32),
                        pltpu.VMEM((Q, hp), F32)],
        compiler_params=_cparams("parallel", "arbitrary"),
        name="ssd_bwd",
    )(xbc, xbc, xbc, dtraw, bias_row, alog_row, e_mat(nheads), yf, z_src, gain_row)


def _qkprep_kernel(q_ref, k_ref, g_ref, cos_ref, sin_ref, o_ref, *, nq, nk):
    hd = ATTN_HEAD_DIM
    cos = cos_ref[...]
    sin = sin_ref[...]
    for h in range(nq + nk):
        src, c0 = (q_ref, h * hd) if h < nq else (k_ref, (h - nq) * hd)
        x = src[:, c0:c0 + hd].astype(F32)
        ms = jnp.mean(x * x, axis=-1, keepdims=True)
        xn = x * lax.rsqrt(ms + EPS) * g_ref[h]
        o_ref[:, h * hd:(h + 1) * hd] = (xn * cos + pltpu.roll(xn, hd // 2, 1) * sin).astype(o_ref.dtype)


def qk_prep(proj, qcol, nq, nk, gains, cos, sin, seq, tq=256):
    T = proj.shape[0]
    hd = ATTN_HEAD_DIM
    tq = _tile(seq, tq)
    spt = seq // tq
    assert qcol % (nq * hd) == 0 and (qcol + nq * hd) % (nk * hd) == 0
    return pl.pallas_call(
        functools.partial(_qkprep_kernel, nq=nq, nk=nk),
        grid=(T // tq,),
        in_specs=[pl.BlockSpec((tq, nq * hd), lambda i: (i, qcol // (nq * hd))),
                  pl.BlockSpec((tq, nk * hd), lambda i: (i, (qcol + nq * hd) // (nk * hd))),
                  pl.BlockSpec((nq + nk, 1, hd), lambda i: (0, 0, 0)),
                  pl.BlockSpec((tq, hd), lambda i: (i % spt, 0)),
                  pl.BlockSpec((tq, hd), lambda i: (i % spt, 0))],
        out_specs=pl.BlockSpec((tq, (nq + nk) * hd), lambda i: (i, 0)),
        out_shape=jax.ShapeDtypeStruct((T, (nq + nk) * hd), BF16),
        compiler_params=_cparams("parallel"),
        name="qk_prep",
    )(proj, proj, gains, cos, sin)


FLASH_ROW_BLOCK = 32


def _flash_kernel(q_ref, k_ref, v_ref, o_ref, qs_ref, va_ref, s0_ref, s1_ref, p_ref, acc_ref,
                  m_ref, al_ref, *, tk, group):
    hd = ATTN_HEAD_DIM
    tq = q_ref.shape[0]
    rows = group * tq
    seq = k_ref.shape[0]
    nk = seq // tk

    @pl.when(pl.program_id(2) == 0)
    def _():
        va_ref[:, 0:hd] = v_ref[...]
        va_ref[:, hd:2 * hd] = jnp.ones((seq, hd), BF16)

    for g in range(group):
        qs_ref[g * tq:(g + 1) * tq, :] = q_ref[:, g * hd:(g + 1) * hd]
    m_ref[...] = jnp.full(m_ref.shape, NEG_BIG, F32)
    acc_ref[...] = jnp.zeros(acc_ref.shape, F32)
    nlc = tk // LANES

    def scores(t, s_ref):
        k0 = pl.multiple_of(t * tk, tk)
        s_ref[...] = lax.dot_general(qs_ref[...], k_ref[pl.ds(k0, tk), :], (((1,), (1,)), ((), ())),
                                     preferred_element_type=F32)

    def update(t, s_ref):
        for r0 in range(0, rows, FLASH_ROW_BLOCK):
            rs = slice(r0, r0 + FLASH_ROW_BLOCK)
            ch = [s_ref[rs, c * LANES:(c + 1) * LANES] for c in range(nlc)]
            mx = ch[0]
            for c in range(1, nlc):
                mx = jnp.maximum(mx, ch[c])
            m_old = m_ref[rs, :]
            m_new = jnp.maximum(m_old, jnp.max(mx, axis=-1, keepdims=True))
            m_ref[rs, :] = m_new
            al_ref[rs, :] = jnp.exp2(m_old - m_new)
            for c in range(nlc):
                p_ref[rs, c * LANES:(c + 1) * LANES] = jnp.exp2(ch[c] - m_new).astype(BF16)
        k0 = pl.multiple_of(t * tk, tk)
        pv = _dot(p_ref[...], va_ref[pl.ds(k0, tk), :])
        al = al_ref[...]
        acc_ref[...] = acc_ref[...] * jnp.concatenate([al, al], axis=1) + pv

    scores(0, s0_ref)

    def body(t2, carry):
        scores(2 * t2 + 1, s1_ref)
        update(2 * t2, s0_ref)
        scores(2 * t2 + 2, s0_ref)
        update(2 * t2 + 1, s1_ref)
        return carry

    lax.fori_loop(0, nk // 2 - 1, body, 0)
    scores(nk - 1, s1_ref)
    update(nk - 2, s0_ref)
    update(nk - 1, s1_ref)
    o = acc_ref[:, 0:hd] / acc_ref[:, hd:2 * hd]
    for g in range(group):
        o_ref[:, g * hd:(g + 1) * hd] = o[g * tq:(g + 1) * tq, :].astype(o_ref.dtype)


def flash_attention(qk, v_src, v_col0, nq_heads, nb, seq, tq=256, tk=1024):
    T = qk.shape[0]
    hd = ATTN_HEAD_DIM
    nkv = ATTN_KV_HEADS
    group = nq_heads // nkv
    tq = _tile(seq, tq)
    tk = _tile(seq // 2, tk)
    nqt = seq // tq
    return pl.pallas_call(
        functools.partial(_flash_kernel, tk=tk, group=group),
        grid=(nb, nkv, nqt),
        in_specs=[pl.BlockSpec((tq, group * hd), lambda b, h, i: (b * nqt + i, h)),
                  pl.BlockSpec((seq, hd), lambda b, h, i: (b, nq_heads + h)),
                  pl.BlockSpec((seq, hd), lambda b, h, i: (b, v_col0 // hd + h))],
        out_specs=pl.BlockSpec((tq, group * hd), lambda b, h, i: (b * nqt + i, h)),
        out_shape=jax.ShapeDtypeStruct((T, nq_heads * hd), BF16),
        scratch_shapes=[pltpu.VMEM((group * tq, hd), BF16),
                        pltpu.VMEM((seq, 2 * hd), BF16),
                        pltpu.VMEM((group * tq, tk), F32),
                        pltpu.VMEM((group * tq, tk), F32),
                        pltpu.VMEM((group * tq, tk), BF16),
                        pltpu.VMEM((group * tq, 2 * hd), F32),
                        pltpu.VMEM((group * tq, LANES), F32),
                        pltpu.VMEM((group * tq, LANES), F32)],
        compiler_params=_cparams("arbitrary", "arbitrary", "arbitrary"),
        name="flash_attn",
    )(qk, qk, v_src)


def _xattn_kernel(q_ref, kv_ref, o_ref, *, heads):
    d = q_ref.shape[1]
    hd = d // heads
    scale = hd ** -0.5
    for h in range(heads):
        q = q_ref[:, h * hd:(h + 1) * hd]
        k = kv_ref[:, h * hd:(h + 1) * hd]
        v = kv_ref[:, d + h * hd:d + (h + 1) * hd]
        s = lax.dot_general(q, k, (((1,), (1,)), ((), ())), preferred_element_type=F32) * scale
        p = jnp.exp(s - jnp.max(s, axis=-1, keepdims=True))
        l = jnp.sum(p, axis=-1, keepdims=True)
        o = _dot(p.astype(BF16), v) / l
        o_ref[:, h * hd:(h + 1) * hd] = o.astype(o_ref.dtype)


def xattn(q, kv, nb, seq, tq=512):
    T, d = q.shape
    n_mem = kv.shape[0] // nb
    tq = _tile(seq, tq)
    nqt = seq // tq
    return pl.pallas_call(
        functools.partial(_xattn_kernel, heads=XA_HEADS),
        grid=(nb, nqt),
        in_specs=[pl.BlockSpec((tq, d), lambda b, i: (b * nqt + i, 0)),
                  pl.BlockSpec((n_mem, 2 * d), lambda b, i: (b, 0))],
        out_specs=pl.BlockSpec((tq, d), lambda b, i: (b * nqt + i, 0)),
        out_shape=jax.ShapeDtypeStruct((T, d), BF16),
        compiler_params=_cparams("parallel", "arbitrary"),
        name="xattn",
    )(q, kv)


def _hyfilter_kernel(z_ref, t_ref, dl_ref, w1_ref, b1_ref, w2_ref, b2_ref, w3_ref, b3_ref,
                     fr_ref, wo_ref, h_ref, sum_ref, *, tl, d):
    i = pl.program_id(0)
    fr = fr_ref[...]
    h = jnp.sin(fr * (_dot_f32(z_ref[...], w1_ref[...]) + b1_ref[...]))
    h = jnp.sin(fr * (_dot_f32(h, w2_ref[...]) + b2_ref[...]))
    h = jnp.sin(fr * (_dot_f32(h, w3_ref[...]) + b3_ref[...]))
    window = jnp.exp(-t_ref[...] * dl_ref[...])
    rows = lax.broadcasted_iota(jnp.int32, (tl, 1), 0) + i * tl

    @pl.when(i == 0)
    def _():
        sum_ref[...] = jnp.zeros_like(sum_ref)

    for part in range(2):
        hp = _dot_f32(h, wo_ref[:, part * d:(part + 1) * d]) * window
        if part == 1:
            hp = jnp.where(rows == 0, 0.0, hp)
        h_ref[:, part * d:(part + 1) * d] = hp.astype(h_ref.dtype)
        sum_ref[:, part * d:(part + 1) * d] += jnp.sum(jnp.abs(hp), axis=0, keepdims=True)


def hyena_filter(z, t_col, deltas, w1, b1, w2, b2, w3, b3, freq, w_out, tl=256):
    L = z.shape[0]
    d2 = w_out.shape[1]
    d = d2 // 2
    fw = w2.shape[0]
    tl = _tile(L, tl)
    full = lambda a: pl.BlockSpec(a.shape, lambda i: (0,) * a.ndim)
    ops = [w1, b1.reshape(1, fw), w2, b2.reshape(1, fw), w3, b3.reshape(1, fw), freq.reshape(1, fw), w_out]
    return pl.pallas_call(
        functools.partial(_hyfilter_kernel, tl=tl, d=d),
        grid=(L // tl,),
        in_specs=[pl.BlockSpec((tl, z.shape[1]), lambda i: (i, 0)),
                  pl.BlockSpec((tl, 1), lambda i: (i, 0)),
                  full(deltas)] + [full(a) for a in ops],
        out_specs=[pl.BlockSpec((tl, d2), lambda i: (i, 0)),
                   pl.BlockSpec((1, d2), lambda i: (0, 0))],
        out_shape=[jax.ShapeDtypeStruct((L, d2), F32),
                   jax.ShapeDtypeStruct((1, d2), F32)],
        compiler_params=_cparams("arbitrary"),
        name="hyena_filter",
    )(z, t_col, deltas, *ops)


DFT_ROWS = 8


def _dft1_kernel(g_ref, u_ref, o_ref):
    kh, rt, tc = u_ref.shape
    u = u_ref[...].reshape(kh * rt, tc).astype(BF16)
    a = _dot(g_ref[...], u).astype(BF16)
    o_ref[...] = pltpu.bitcast(a, jnp.uint32).reshape(o_ref.shape)


def dft_stage1(g1, u4, tc=512):
    nb, kh, n2, c = u4.shape
    rt = DFT_ROWS
    n1 = g1.shape[0] // (2 * rt)
    tc = _tile(c, tc)
    return pl.pallas_call(
        _dft1_kernel,
        grid=(nb, n2 // rt, c // tc),
        in_specs=[pl.BlockSpec(g1.shape, lambda b, i, j: (0, 0)),
                  pl.BlockSpec((None, kh, rt, tc), lambda b, i, j: (b, 0, i, j))],
        out_specs=pl.BlockSpec((None, n1, rt, tc), lambda b, i, j: (b, 0, i, j)),
        out_shape=jax.ShapeDtypeStruct((nb, n1, n2, c), jnp.uint32),
        compiler_params=_cparams("parallel", "parallel", "arbitrary"),
        name="dft_stage1",
    )(g1, u4)


def _unpack_complex(ref):
    return pltpu.bitcast(ref[...], BF16)


def _spectrum_kernel(af_ref, ab_ref, h_ref, sum_ref, o_ref, *, d_cols):
    n2 = DFT_N2
    hm = h_ref[...]
    xf = _dot(hm, _unpack_complex(af_ref))
    xb = _dot(hm, _unpack_complex(ab_ref))
    inv = 1.0 / (sum_ref[:, 0:d_cols] + sum_ref[:, d_cols:2 * d_cols])
    o_ref[0] = (xf[:n2] + xb[:n2]) * inv
    o_ref[1] = (xf[n2:] - xb[n2:]) * inv


def filter_spectrum(a4, hmat, sums, d):
    n1 = a4.shape[1]
    n2 = DFT_N2
    return pl.pallas_call(
        functools.partial(_spectrum_kernel, d_cols=d),
        grid=(n1,),
        in_specs=[pl.BlockSpec((None, None, n2, d), lambda k: (0, k, 0, 0)),
                  pl.BlockSpec((None, None, n2, d), lambda k: (0, k, 0, 1)),
                  pl.BlockSpec((None, 2 * n2, 2 * n2), lambda k: (k, 0, 0)),
                  pl.BlockSpec((1, 2 * d), lambda k: (0, 0))],
        out_specs=pl.BlockSpec((2, None, n2, d), lambda k: (0, k, 0, 0)),
        out_shape=jax.ShapeDtypeStruct((2, n1, n2, d), F32),
        compiler_params=_cparams("arbitrary"),
        name="filter_spectrum",
    )(a4, a4, hmat, sums)


def _dftmid_kernel(a_ref, h_ref, g_ref, k_ref, o_ref):
    n2 = DFT_N2
    x = _dot(h_ref[...], _unpack_complex(a_ref))
    xr, xi = x[:n2], x[n2:]
    kr, ki = k_ref[0], k_ref[1]
    y = jnp.concatenate([xr * kr - xi * ki, xr * ki + xi * kr], axis=0).astype(BF16)
    zz = _dot(g_ref[...], y).astype(BF16)
    o_ref[...] = pltpu.bitcast(zz, jnp.uint32)


def dft_mid(a4, hmat, gmat, kspec):
    nb, n1, n2, c = a4.shape
    return pl.pallas_call(
        _dftmid_kernel,
        grid=(n1, nb),
        in_specs=[pl.BlockSpec((None, None, n2, c), lambda k, b: (b, k, 0, 0)),
                  pl.BlockSpec((None, 2 * n2, 2 * n2), lambda k, b: (k, 0, 0)),
                  pl.BlockSpec((None, 2 * n2, 2 * n2), lambda k, b: (k, 0, 0)),
                  pl.BlockSpec((2, None, n2, c), lambda k, b: (0, k, 0, 0))],
        out_specs=pl.BlockSpec((None, None, n2, c), lambda k, b: (b, k, 0, 0)),
        out_shape=jax.ShapeDtypeStruct(a4.shape, jnp.uint32),
        compiler_params=_cparams("parallel", "arbitrary"),
        name="dft_mid",
    )(a4, hmat, gmat, kspec)


def _dftout_kernel(g_ref, z_ref, x0_ref, w_ref, skip_ref, o_ref):
    n1, rt, tc = z_ref.shape
    z = pltpu.bitcast(z_ref[...].reshape(n1 * rt, tc), BF16)
    y = _dot(g_ref[...], z).reshape(o_ref.shape)
    o_ref[...] = x0_ref[...] * (y + w_ref[...] * skip_ref[...])


def dft_out(g2, z4, x0, w, skip, tc=512):
    nb, n1, n2, c = z4.shape
    rt = DFT_ROWS
    kh = g2.shape[0] // rt
    tc = _tile(c, tc)
    tok = pl.BlockSpec((None, kh, rt, tc), lambda b, i, j: (b, 0, i, j))
    return pl.pallas_call(
        _dftout_kernel,
        grid=(nb, n2 // rt, c // tc),
        in_specs=[pl.BlockSpec(g2.shape, lambda b, i, j: (0, 0)),
                  pl.BlockSpec((None, n1, rt, tc), lambda b, i, j: (b, 0, i, j)),
                  tok, tok,
                  pl.BlockSpec((1, 1, tc), lambda b, i, j: (0, 0, j))],
        out_specs=tok,
        out_shape=jax.ShapeDtypeStruct((nb, kh, n2, c), F32),
        compiler_params=_cparams("parallel", "parallel", "arbitrary"),
        name="dft_out",
    )(g2, z4, x0, w, skip.reshape(1, 1, c).astype(F32))


def _rope_tables(seq):
    hd = ATTN_HEAD_DIM
    axis_dim = hd // 2
    t = jnp.arange(seq)
    row = (t // GRID_W).astype(F32)
    col = (t % GRID_W).astype(F32)
    inv_freq = ROPE_THETA ** (-jnp.arange(0, axis_dim, 2, dtype=F32) / axis_dim)
    ang = jnp.concatenate([row[:, None] * inv_freq, col[:, None] * inv_freq], axis=-1)
    c, s = jnp.cos(ang), jnp.sin(ang)
    return jnp.concatenate([c, c], axis=-1), jnp.concatenate([-s, s], axis=-1)


def _hyena_features(seq, d):
    t = jnp.linspace(0.0, 1.0, seq, dtype=F32)
    w = 2.0 * math.pi * jnp.arange(seq, dtype=F32) / seq
    f = jnp.linspace(1e-4, HY_BANDS - 1, HY_BANDS, dtype=F32)
    fw = w[:, None] * f[None, :]
    z = jnp.concatenate([t[:, None], jnp.cos(fw), -jnp.sin(fw)], axis=-1)
    z = jnp.pad(z, ((0, 0), (0, LANES - HY_EMB)))
    deltas = jnp.abs(jnp.linspace(math.log(HY_TARGET) / HY_SLOW_PCT,
                                  math.log(HY_TARGET) / HY_FAST_PCT, d, dtype=F32))
    return z, t[:, None], deltas[None, :]


def _dft_tables(seq):
    n = 2 * seq
    n2 = DFT_N2
    n1 = n // n2

    def cs(phase_int, mod):
        ang = (-2.0 * math.pi / mod) * (phase_int % mod).astype(F32)
        return jnp.cos(ang), jnp.sin(ang)

    k1 = jnp.arange(n1)[:, None]
    m1 = jnp.arange(n1 // 2)[None, :]
    fr, fi = cs(k1 * m1, n1)
    base = jnp.stack([fr, fi], axis=-1)
    eye = jnp.eye(DFT_ROWS, dtype=F32)
    rows = DFT_ROWS
    g1 = jnp.einsum('knp,rs->krpns', base, eye).reshape(n1 * rows * 2, (n1 // 2) * rows).astype(BF16)
    g2 = (jnp.einsum('knp,rs->nrksp', base, eye).reshape((n1 // 2) * rows, n1 * rows * 2) / n).astype(BF16)
    kk = (jnp.arange(n1)[:, None, None] + n1 * jnp.arange(n2)[None, :, None])
    nn = jnp.arange(n2)[None, None, :]
    hr, hi = cs(kk * nn, n)
    hmat = jnp.concatenate([jnp.stack([hr, -hi], axis=-1).reshape(n1, n2, 2 * n2),
                            jnp.stack([hi, hr], axis=-1).reshape(n1, n2, 2 * n2)], axis=1).astype(BF16)
    gr, gi = jnp.swapaxes(hr, 1, 2), -jnp.swapaxes(hi, 1, 2)
    gmat = jnp.stack([jnp.concatenate([gr, -gi], axis=2),
                      jnp.concatenate([gi, gr], axis=2)], axis=2).reshape(n1, 2 * n2, 2 * n2).astype(BF16)
    return g1, g2, hmat, gmat


def _deinterleave(nheads):
    hd = ATTN_HEAD_DIM
    one = np.concatenate([np.arange(0, hd, 2), np.arange(1, hd, 2)])
    return np.concatenate([h * hd + one for h in range(nheads)])


def kernel(x_prompt, x_sample, mem_prompt, mem_sample, norm_mix, norm_xa, norm_mem, norm_ffn, xa_wq, xa_wk, xa_wv, xa_wo, ffn_w_in, ffn_conv_w, ffn_conv_b, ffn_w_out, mix_w_in, mix_w_out, ssd_conv_w, ssd_conv_b, ssd_a_log, ssd_dt_bias, ssd_d, ssd_norm, attn_q_norm, attn_k_norm, hy_w_in, hy_conv_w, hy_conv_b, hy_f_w1, hy_f_b1, hy_f_w2, hy_f_b2, hy_f_w3, hy_f_b3, hy_f_freq, hy_f_w_out, hy_skip, hy_w_out, final_norm):
    nbp, seq, d = x_prompt.shape
    nbs = x_sample.shape[0]
    assert x_sample.shape[1] == seq
    nb = nbp + nbs
    T = nb * seq
    depth = norm_mix.shape[0]
    n_mem = mem_prompt.shape[1]
    d_ff = ffn_w_out.shape[1]

    x = jnp.concatenate([x_prompt, x_sample], axis=0).reshape(T, d)
    mem = jnp.concatenate([mem_prompt, mem_sample], axis=0).reshape(nb * n_mem, d)

    d_ssd = d
    nheads = d_ssd // SSD_HEAD_DIM
    gn = SSD_GROUPS * SSD_STATE
    conv_ch = d_ssd + 2 * gn
    n_att = d // ATTN_HEAD_DIM
    d_kv = ATTN_KV_HEADS * ATTN_HEAD_DIM
    o1 = d_ssd
    o2 = o1 + conv_ch
    o3 = o2 + 2 * nheads
    o4 = o3 + d
    o5 = o4 + d_kv
    qcol = o1
    vcol = o1 + d + d_kv
    cos, sin = _rope_tables(seq)
    perm_q = _deinterleave(n_att)
    perm_k = _deinterleave(ATTN_KV_HEADS)
    perm_h = _deinterleave(1)

    n2 = DFT_N2
    n1 = 2 * seq // n2
    hz, t_col, deltas = _hyena_features(seq, d)
    g1, g2, hmat, gmat = _dft_tables(seq)

    for i in range(depth):
        if i % 2 == 0:
            e = i // 2
            w = mix_w_in[e]
            w_main = jnp.concatenate([w[:, :o1], w[:, o3:o4][:, perm_q], w[:, o4:o5][:, perm_k], w[:, o5:]],
                                     axis=1).astype(BF16)
            w_dt = jnp.pad(w[:, o2:o3], ((0, 0), (0, LANES - 2 * nheads))).astype(BF16)
            proj = normmm(x, norm_mix[i], w_main)
            dtraw = normmm(x, norm_mix[i], w_dt, out_dtype=F32, tn=LANES)
            (xbc,) = normmm_conv(x, norm_mix[i], w[:, o1:o2].astype(BF16), ssd_conv_w[e], ssd_conv_b[e],
                                 [0], conv_ch, _epi_silu, 1, seq)
            pad_row = lambda a: jnp.pad(a.reshape(1, -1).astype(F32), ((0, 0), (0, LANES - 2 * nheads)))
            y_ssd = ssd_scan(xbc, dtraw, pad_row(ssd_dt_bias[e]), pad_row(ssd_a_log[e]),
                             jnp.repeat(ssd_d[e].astype(F32), SSD_HEAD_DIM)[None, :],
                             proj, ssd_norm[e].reshape(1, -1).astype(F32), nb, seq)
            scale = ATTN_HEAD_DIM ** -0.5 * math.log2(math.e)
            gains = jnp.concatenate([jnp.tile(attn_q_norm[e][perm_h][None, :] * scale, (n_att, 1)),
                                     jnp.tile(attn_k_norm[e][perm_h][None, :], (ATTN_KV_HEADS, 1))],
                                    axis=0)[:, None, :].astype(F32)
            qk = qk_prep(proj, qcol, n_att, ATTN_KV_HEADS, gains, cos, sin, seq)
            y_att = flash_attention(qk, proj, vcol, n_att, nb, seq)
            x = mm_res(jnp.concatenate([y_ssd, y_att], axis=1), mix_w_out[e].astype(BF16), x)
        else:
            o = i // 2
            x0, wv = normmm_conv(x, norm_mix[i], hy_w_in[o].astype(BF16), hy_conv_w[o], hy_conv_b[o],
                                 [0, d, 2 * d], d, _epi_hyena, 2, seq, out_dtype=F32)
            w1 = jnp.pad(hy_f_w1[o], ((0, LANES - HY_EMB), (0, 0)))
            hfb, sums = hyena_filter(hz, t_col, deltas, w1, hy_f_b1[o], hy_f_w2[o], hy_f_b2[o],
                                     hy_f_w3[o], hy_f_b3[o], hy_f_freq[o], hy_f_w_out[o])
            a_f = dft_stage1(g1, hfb.reshape(1, n1 // 2, n2, 2 * d))
            kspec = filter_spectrum(a_f, hmat, sums, d)
            a_u = dft_stage1(g1, wv.reshape(nb, n1 // 2, n2, d))
            zz = dft_mid(a_u, hmat, gmat, kspec)
            yh = dft_out(g2, zz, x0.reshape(nb, n1 // 2, n2, d), wv.reshape(nb, n1 // 2, n2, d), hy_skip[o])
            x = mm_res(yh.reshape(T, d), hy_w_out[o].astype(BF16), x)
        q = normmm(x, norm_xa[i], xa_wq[i].astype(BF16))
        kv = normmm(mem, norm_mem[i], jnp.concatenate([xa_wk[i], xa_wv[i]], axis=1).astype(BF16))
        x = mm_res(xattn(q, kv, nb, seq), xa_wo[i].astype(BF16), x)
        (act,) = normmm_conv(x, norm_ffn[i], ffn_w_in[i].astype(BF16), ffn_conv_w[i], ffn_conv_b[i],
                             [0, d_ff], d_ff, _epi_glu, 1, seq)
        x = mm_res(act, ffn_w_out[i].astype(BF16), x)

    y_prompt = rmsnorm(x, final_norm, 0, nbp * seq).reshape(nbp, seq, d)
    y_sample = rmsnorm(x, final_norm, nbp * seq, nbs * seq).reshape(nbs, seq, d)
    return (y_prompt, y_sample)
```

```python
import functools
import math

import numpy as np
import jax
import jax.numpy as jnp
from jax import lax
from jax.experimental import pallas as pl
from jax.experimental.pallas import tpu as pltpu

F32 = jnp.float32
BF16 = jnp.bfloat16
EPS = 1e-6

GRID_W = 64
XA_HEADS = 4
SSD_HEAD_DIM = 64
SSD_GROUPS = 4
SSD_STATE = 128
SSD_CHUNK = 128
ATTN_HEAD_DIM = 128
ATTN_KV_HEADS = 4
ROPE_THETA = 10000.0
HY_EMB = 33
HY_BANDS = (HY_EMB - 1) // 2
HY_TARGET = 1e-2
HY_FAST_PCT = 0.3
HY_SLOW_PCT = 1.5

LANES = 128
MXU_WIDTH = 256
DFT_N2 = 128
VMEM_LIMIT = 52 * 1024 * 1024
NEG_BIG = -1e30


def _cparams(*sem):
    return pltpu.CompilerParams(dimension_semantics=sem, vmem_limit_bytes=VMEM_LIMIT)


def _tile(dim, pref):
    t = min(dim, pref)
    while dim % t:
        t //= 2
    return t


def _split3(x):
    hi = x.astype(BF16)
    r1 = x - hi.astype(F32)
    mid = r1.astype(BF16)
    lo = (r1 - mid.astype(F32)).astype(BF16)
    return hi, mid, lo


def _dot(a, b):
    return jnp.dot(a, b, preferred_element_type=F32)


def _dot_exact_rhs(x, e):
    hi, mid, lo = _split3(x)
    return _dot(hi, e) + _dot(mid, e) + _dot(lo, e)


def _dot_exact_lhs(e, x):
    hi, mid, lo = _split3(x)
    return _dot(e, hi) + _dot(e, mid) + _dot(e, lo)


def _dot_f32(a, b):
    ah, am, _ = _split3(a)
    bh, bm, _ = _split3(b)
    return _dot(ah, bh) + _dot(ah, bm) + _dot(am, bh)


def _silu(x):
    return x * (1.0 / (1.0 + jnp.exp(-x)))


def _normmm_kernel(x_ref, g_ref, w_ref, o_ref, xn_ref):
    @pl.when(pl.program_id(1) == 0)
    def _():
        x = x_ref[...].astype(F32)
        ms = jnp.mean(x * x, axis=-1, keepdims=True)
        xn_ref[...] = (x * lax.rsqrt(ms + EPS) * g_ref[...]).astype(BF16)

    o_ref[...] = _dot(xn_ref[...], w_ref[...]).astype(o_ref.dtype)


def normmm(x, g, w, out_dtype=BF16, tm=1024, tn=1024):
    M, K = x.shape
    N = w.shape[1]
    tm = _tile(M, tm)
    tn = _tile(N, tn)
    return pl.pallas_call(
        _normmm_kernel,
        grid=(M // tm, N // tn),
        in_specs=[pl.BlockSpec((tm, K), lambda i, j: (i, 0)),
                  pl.BlockSpec((1, K), lambda i, j: (0, 0)),
                  pl.BlockSpec((K, tn), lambda i, j: (0, j))],
        out_specs=pl.BlockSpec((tm, tn), lambda i, j: (i, j)),
        out_shape=jax.ShapeDtypeStruct((M, N), out_dtype),
        scratch_shapes=[pltpu.VMEM((tm, K), BF16)],
        compiler_params=_cparams("parallel", "arbitrary"),
        name="normmm",
    )(x, g.reshape(1, K).astype(F32), w)


def _mmres_kernel(a_ref, w_ref, r_ref, o_ref):
    o_ref[...] = r_ref[...] + _dot(a_ref[...].astype(BF16), w_ref[...])


def mm_res(a, w, res, tm=1024, tn=None):
    M, K = a.shape
    N = w.shape[1]
    tm = _tile(M, tm)
    tn = _tile(N, tn or (1024 if K <= 2048 else 512))
    return pl.pallas_call(
        _mmres_kernel,
        grid=(M // tm, N // tn),
        in_specs=[pl.BlockSpec((tm, K), lambda i, j: (i, 0)),
                  pl.BlockSpec((K, tn), lambda i, j: (0, j)),
                  pl.BlockSpec((tm, tn), lambda i, j: (i, j))],
        out_specs=pl.BlockSpec((tm, tn), lambda i, j: (i, j)),
        out_shape=jax.ShapeDtypeStruct((M, N), F32),
        compiler_params=_cparams("parallel", "arbitrary"),
        name="mm_res",
    )(a, w, res)


def _rmsnorm_kernel(x_ref, g_ref, o_ref):
    x = x_ref[...]
    ms = jnp.mean(x * x, axis=-1, keepdims=True)
    o_ref[...] = x * lax.rsqrt(ms + EPS) * g_ref[...]


def rmsnorm(x, g, row0, rows, tm=512):
    K = x.shape[1]
    tm = _tile(math.gcd(row0, rows) if row0 else rows, tm)
    return pl.pallas_call(
        _rmsnorm_kernel,
        grid=(rows // tm,),
        in_specs=[pl.BlockSpec((tm, K), lambda i: (i + row0 // tm, 0)),
                  pl.BlockSpec((1, K), lambda i: (0, 0))],
        out_specs=pl.BlockSpec((tm, K), lambda i: (i, 0)),
        out_shape=jax.ShapeDtypeStruct((rows, K), F32),
        compiler_params=_cparams("parallel"),
        name="final_norm",
    )(x, g.reshape(1, K).astype(F32))


CONV_HALO = 16
CONV_ROWS = 64


def _normmm_conv_kernel(*refs, nseg, width, tm, seq, epilogue, nout):
    xm_ref, xp_ref, xn_ref, g_ref = refs[:4]
    segs = [refs[4 + 3 * s:7 + 3 * s] for s in range(nseg)]
    outs = refs[4 + 3 * nseg:4 + 3 * nseg + nout]
    hn_ref = refs[4 + 3 * nseg + nout]
    exts = refs[5 + 3 * nseg + nout:]
    h = CONV_HALO
    half = width // 2
    row0 = pl.program_id(0) * tm
    at_start = (row0 % seq) == 0
    at_end = ((row0 + tm) % seq) == 0

    @pl.when(pl.program_id(1) == 0)
    def _():
        def nrm(x):
            ms = jnp.mean(x * x, axis=-1, keepdims=True)
            return (x * lax.rsqrt(ms + EPS) * g_ref[...]).astype(BF16)
        hn_ref[0:h, :] = nrm(xp_ref[...])
        hn_ref[h:h + tm, :] = nrm(xm_ref[...])
        hn_ref[h + tm:h + tm + h, :] = nrm(xn_ref[...])

    tn = exts[0].shape[1]
    for (w_ref, _, _), ext in zip(segs, exts):
        ext[...] = _dot(hn_ref[...], w_ref[...])
        ext[0:h, :] = jnp.where(at_start, 0.0, ext[0:h, :])
        ext[h + tm:h + tm + h, :] = jnp.where(at_end, 0.0, ext[h + tm:h + tm + h, :])

    for rb in range(0, tm, CONV_ROWS):
        for lc in range(0, tn, LANES):
            ls = slice(lc, lc + LANES)
            vals = []
            for (_, cw_ref, cb_ref), ext in zip(segs, exts):
                acc = None
                for k in range(width):
                    term = ext[h - half + k + rb:h - half + k + rb + CONV_ROWS, ls] * cw_ref[k:k + 1, ls]
                    acc = term if acc is None else acc + term
                vals.append(acc + cb_ref[:, ls])
            for o, r in zip(outs, epilogue(*vals)):
                o[rb:rb + CONV_ROWS, ls] = r.astype(o.dtype)


def normmm_conv(x, g, w, conv_w, conv_b, seg_cols, width_cols, epilogue, nout, seq, out_dtype=BF16,
                tm=1024, tn=512):
    T, K = x.shape
    width = conv_w.shape[0]
    nseg = len(seg_cols)
    tm = _tile(seq, tm)
    tn = _tile(width_cols, tn)
    assert tm % CONV_ROWS == 0 and tn % LANES == 0
    h = CONV_HALO
    nrb = T // h
    cb = conv_b.reshape(1, -1).astype(F32)
    cw = conv_w.astype(F32)
    in_specs = [pl.BlockSpec((tm, K), lambda i, j: (i, 0)),
                pl.BlockSpec((h, K), lambda i, j: (jnp.maximum(i * (tm // h) - 1, 0), 0)),
                pl.BlockSpec((h, K), lambda i, j: (jnp.minimum((i + 1) * (tm // h), nrb - 1), 0)),
                pl.BlockSpec((1, K), lambda i, j: (0, 0))]
    args = [x, x, x, g.reshape(1, K).astype(F32)]
    for c0 in seg_cols:
        off = c0 // tn
        in_specs += [pl.BlockSpec((K, tn), lambda i, j, off=off: (0, j + off)),
                     pl.BlockSpec((width, tn), lambda i, j, off=off: (0, j + off)),
                     pl.BlockSpec((1, tn), lambda i, j, off=off: (0, j + off))]
        args += [w, cw, cb]
    kern = functools.partial(_normmm_conv_kernel, nseg=nseg, width=width, tm=tm, seq=seq,
                             epilogue=epilogue, nout=nout)
    return pl.pallas_call(
        kern,
        grid=(T // tm, width_cols // tn),
        in_specs=in_specs,
        out_specs=[pl.BlockSpec((tm, tn), lambda i, j: (i, j)) for _ in range(nout)],
        out_shape=[jax.ShapeDtypeStruct((T, width_cols), out_dtype) for _ in range(nout)],
        scratch_shapes=[pltpu.VMEM((tm + 2 * h, K), BF16)]
        + [pltpu.VMEM((tm + 2 * h, tn), F32) for _ in range(nseg)],
        compiler_params=_cparams("parallel", "arbitrary"),
        name="normmm_conv",
    )(*args)


def _epi_silu(c):
    return (_silu(c),)


def _epi_glu(g, up):
    return (_silu(g) * up,)


def _epi_hyena(x0, x1, v):
    return (x0, v * x1)


def _softplus(x):
    return jnp.maximum(x, 0.0) + jnp.log(1.0 + jnp.exp(-jnp.abs(x)))


def _ssd_kernel(*refs, rev, nheads):
    if rev:
        (xs_ref, b_ref, c_ref, dt_ref, bias_ref, alog_ref, e_ref,
         yf_ref, z_ref, gain_ref, o_ref, s_ref, y_ref) = refs
    else:
        (xs_ref, b_ref, c_ref, dt_ref, bias_ref, alog_ref, e_ref,
         dskip_ref, o_ref, s_ref) = refs
        y_ref = o_ref
    Q = SSD_CHUNK
    P = SSD_HEAD_DIM
    hpg = nheads // SSD_GROUPS
    gw = hpg * P
    hoff = nheads if rev else 0

    @pl.when(pl.program_id(1) == 0)
    def _():
        s_ref[...] = jnp.zeros_like(s_ref)

    row = lax.broadcasted_iota(jnp.int32, (Q, Q), 0)
    col = lax.broadcasted_iota(jnp.int32, (Q, Q), 1)
    mask = (col >= row) if rev else (col <= row)
    tri = jnp.where(mask, 1.0, 0.0).astype(BF16)

    dtv = _softplus(dt_ref[...] + bias_ref[...])
    a_row = -jnp.exp(alog_ref[...])
    la = dtv * a_row
    cs = _dot_exact_lhs(tri, la)
    tot = cs[0:1, :] if rev else cs[Q - 1:Q, :]
    cs_t = cs.T
    dt_t = dtv.T
    e = e_ref[...]
    ecs_hi, ecs_lo, _ = _split3(jnp.exp(cs))
    carry_in = _dot(ecs_hi, e) + _dot(ecs_lo, e)
    to_end = _dot((jnp.exp(tot - cs) * dtv).astype(BF16), e)
    dec = _dot_exact_rhs(jnp.broadcast_to(jnp.exp(tot), (8, LANES)), e)[0:1, :]

    xs = xs_ref[...]
    x_state = (xs.astype(F32) * to_end).astype(BF16)
    lane = lax.broadcasted_iota(jnp.int32, (Q, LANES), 1)
    low = lane < P

    for g in range(SSD_GROUPS):
        bg = b_ref[:, g * SSD_STATE:(g + 1) * SSD_STATE]
        cg = c_ref[:, g * SSD_STATE:(g + 1) * SSD_STATE]
        cb = lax.dot_general(cg, bg, (((1,), (1,)), ((), ())), preferred_element_type=F32)
        s_old = s_ref[g]
        y_off = _dot(cg, s_old.astype(BF16)) * carry_in[:, g * gw:(g + 1) * gw]
        s_ref[g] = s_old * dec[:, g * gw:(g + 1) * gw] + lax.dot_general(
            bg, x_state[:, g * gw:(g + 1) * gw], (((0,), (0,)), ((), ())), preferred_element_type=F32)
        for j in range(hpg // 2):
            ws = []
            for hh in range(2):
                hc = hoff + g * hpg + 2 * j + hh
                diff = cs[:, hc:hc + 1] - cs_t[hc:hc + 1, :]
                decay = jnp.exp(jnp.where(mask, diff, NEG_BIG))
                ws.append((cb * decay * dt_t[hc:hc + 1, :]).astype(BF16))
            c0 = g * gw + 2 * j * P
            xp = xs[:, c0:c0 + LANES]
            rhs = jnp.concatenate([jnp.where(low, xp, jnp.zeros_like(xp)),
                                   jnp.where(low, jnp.zeros_like(xp), xp)], axis=0)
            y = _dot(jnp.concatenate(ws, axis=1), rhs) + y_off[:, 2 * j * P:2 * j * P + LANES]
            if not rev:
                y = y + xp.astype(F32) * dskip_ref[:, c0:c0 + LANES]
            y_ref[:, c0:c0 + LANES] = y

    if rev:
        y = y_ref[...] + yf_ref[...]
        gated = y * _silu(z_ref[...].astype(F32))
        ms = jnp.mean(gated * gated, axis=-1, keepdims=True)
        o_ref[...] = (gated * lax.rsqrt(ms + EPS) * gain_ref[...]).astype(o_ref.dtype)


def ssd_scan(xbc, dtraw, bias_row, alog_row, d_row, z_src, gain_row, nb, seq):
    T = xbc.shape[0]
    Q = SSD_CHUNK
    nc = seq // Q
    gn = SSD_GROUPS * SSD_STATE
    hp = xbc.shape[1] - 2 * gn
    nheads = hp // SSD_HEAD_DIM
    hpg = nheads // SSD_GROUPS
    gw = hpg * SSD_HEAD_DIM
    assert hp % gn == 0 and 2 * nheads <= LANES

    def e_mat(off):
        r = np.arange(LANES)[:, None]
        c = np.arange(hp)[None, :]
        return jnp.asarray((r == off + c // SSD_HEAD_DIM).astype(np.float32), dtype=BF16)

    def specs(rev):
        def blk(c):
            return (nc - 1 - c) if rev else c
        return [
            pl.BlockSpec((Q, hp), lambda b, c: (b * nc + blk(c), 0)),
            pl.BlockSpec((Q, gn), lambda b, c: (b * nc + blk(c), hp // gn)),
            pl.BlockSpec((Q, gn), lambda b, c: (b * nc + blk(c), hp // gn + 1)),
            pl.BlockSpec((Q, LANES), lambda b, c: (b * nc + blk(c), 0)),
            pl.BlockSpec((1, LANES), lambda b, c: (0, 0)),
            pl.BlockSpec((1, LANES), lambda b, c: (0, 0)),
            pl.BlockSpec((LANES, hp), lambda b, c: (0, 0)),
        ], (lambda b, c: (b * nc + blk(c), 0))

    in_f, omap_f = specs(False)
    yf = pl.pallas_call(
        functools.partial(_ssd_kernel, rev=False, nheads=nheads),
        grid=(nb, nc),
        in_specs=in_f + [pl.BlockSpec((1, hp), lambda b, c: (0, 0))],
        out_specs=pl.BlockSpec((Q, hp), omap_f),
        out_shape=jax.ShapeDtypeStruct((T, hp), F32),
        scratch_shapes=[pltpu.VMEM((SSD_GROUPS, SSD_STATE, gw), F32)],
        compiler_params=_cparams("parallel", "arbitrary"),
        name="ssd_fwd",
    )(xbc, xbc, xbc, dtraw, bias_row, alog_row, e_mat(0), d_row)
    in_b, omap_b = specs(True)
    return pl.pallas_call(
        functools.partial(_ssd_kernel, rev=True, nheads=nheads),
        grid=(nb, nc),
        in_specs=in_b + [pl.BlockSpec((Q, hp), omap_b),
                         pl.BlockSpec((Q, hp), omap_b),
                         pl.BlockSpec((1, hp), lambda b, c: (0, 0))],
        out_specs=pl.BlockSpec((Q, hp), omap_b),
        out_shape=jax.ShapeDtypeStruct((T, hp), BF16),
        scratch_shapes=[pltpu.VMEM((SSD_GROUPS, SSD_STATE, gw), F32),
                        pltpu.VMEM((Q, hp), F32)],
        compiler_params=_cparams("parallel", "arbitrary"),
        name="ssd_bwd",
    )(xbc, xbc, xbc, dtraw, bias_row, alog_row, e_mat(nheads), yf, z_src, gain_row)


def _qkprep_kernel(q_ref, k_ref, g_ref, cos_ref, sin_ref, o_ref, *, nq, nk):
    hd = ATTN_HEAD_DIM
    cos = cos_ref[...]
    sin = sin_ref[...]
    for h in range(nq + nk):
        src, c0 = (q_ref, h * hd) if h < nq else (k_ref, (h - nq) * hd)
        x = src[:, c0:c0 + hd].astype(F32)
        ms = jnp.mean(x * x, axis=-1, keepdims=True)
        xn = x * lax.rsqrt(ms + EPS) * g_ref[h]
        o_ref[:, h * hd:(h + 1) * hd] = (xn * cos + pltpu.roll(xn, hd // 2, 1) * sin).astype(o_ref.dtype)


def qk_prep(proj, qcol, nq, nk, gains, cos, sin, seq, tq=256):
    T = proj.shape[0]
    hd = ATTN_HEAD_DIM
    tq = _tile(seq, tq)
    spt = seq // tq
    assert qcol % (nq * hd) == 0 and (qcol + nq * hd) % (nk * hd) == 0
    return pl.pallas_call(
        functools.partial(_qkprep_kernel, nq=nq, nk=nk),
        grid=(T // tq,),
        in_specs=[pl.BlockSpec((tq, nq * hd), lambda i: (i, qcol // (nq * hd))),
                  pl.BlockSpec((tq, nk * hd), lambda i: (i, (qcol + nq * hd) // (nk * hd))),
                  pl.BlockSpec((nq + nk, 1, hd), lambda i: (0, 0, 0)),
                  pl.BlockSpec((tq, hd), lambda i: (i % spt, 0)),
                  pl.BlockSpec((tq, hd), lambda i: (i % spt, 0))],
        out_specs=pl.BlockSpec((tq, (nq + nk) * hd), lambda i: (i, 0)),
        out_shape=jax.ShapeDtypeStruct((T, (nq + nk) * hd), BF16),
        compiler_params=_cparams("parallel"),
        name="qk_prep",
    )(proj, proj, gains, cos, sin)


FLASH_ROW_BLOCK = 32


def _flash_kernel(q_ref, k_ref, v_ref, o_ref, qs_ref, va_ref, s0_ref, s1_ref, p_ref, acc_ref,
                  m_ref, al_ref, *, tk, group):
    hd = ATTN_HEAD_DIM
    tq = q_ref.shape[0]
    rows = group * tq
    seq = k_ref.shape[0]
    nk = seq // tk

    @pl.when(pl.program_id(2) == 0)
    def _():
        va_ref[:, 0:hd] = v_ref[...]
        va_ref[:, hd:2 * hd] = jnp.ones((seq, hd), BF16)

    for g in range(group):
        qs_ref[g * tq:(g + 1) * tq, :] = q_ref[:, g * hd:(g + 1) * hd]
    m_ref[...] = jnp.full(m_ref.shape, NEG_BIG, F32)
    acc_ref[...] = jnp.zeros(acc_ref.shape, F32)
    nlc = tk // LANES

    def scores(t, s_ref):
        k0 = pl.multiple_of(t * tk, tk)
        s_ref[...] = lax.dot_general(qs_ref[...], k_ref[pl.ds(k0, tk), :], (((1,), (1,)), ((), ())),
                                     preferred_element_type=F32)

    def update(t, s_ref):
        for r0 in range(0, rows, FLASH_ROW_BLOCK):
            rs = slice(r0, r0 + FLASH_ROW_BLOCK)
            ch = [s_ref[rs, c * LANES:(c + 1) * LANES] for c in range(nlc)]
            mx = ch[0]
            for c in range(1, nlc):
                mx = jnp.maximum(mx, ch[c])
            m_old = m_ref[rs, :]
            m_new = jnp.maximum(m_old, jnp.max(mx, axis=-1, keepdims=True))
            m_ref[rs, :] = m_new
            al_ref[rs, :] = jnp.exp2(m_old - m_new)
            for c in range(nlc):
                p_ref[rs, c * LANES:(c + 1) * LANES] = jnp.exp2(ch[c] - m_new).astype(BF16)
        k0 = pl.multiple_of(t * tk, tk)
        pv = _dot(p_ref[...], va_ref[pl.ds(k0, tk), :])
        al = al_ref[...]
        acc_ref[...] = acc_ref[...] * jnp.concatenate([al, al], axis=1) + pv

    scores(0, s0_ref)

    def body(t2, carry):
        scores(2 * t2 + 1, s1_ref)
        update(2 * t2, s0_ref)
        scores(2 * t2 + 2, s0_ref)
        update(2 * t2 + 1, s1_ref)
        return carry

    lax.fori_loop(0, nk // 2 - 1, body, 0)
    scores(nk - 1, s1_ref)
    update(nk - 2, s0_ref)
    update(nk - 1, s1_ref)
    o = acc_ref[:, 0:hd] / acc_ref[:, hd:2 * hd]
    for g in range(group):
        o_ref[:, g * hd:(g + 1) * hd] = o[g * tq:(g + 1) * tq, :].astype(o_ref.dtype)


def flash_attention(qk, v_src, v_col0, nq_heads, nb, seq, tq=256, tk=1024):
    T = qk.shape[0]
    hd = ATTN_HEAD_DIM
    nkv = ATTN_KV_HEADS
    group = nq_heads // nkv
    tq = _tile(seq, tq)
    tk = _tile(seq // 2, tk)
    nqt = seq // tq
    return pl.pallas_call(
        functools.partial(_flash_kernel, tk=tk, group=group),
        grid=(nb, nkv, nqt),
        in_specs=[pl.BlockSpec((tq, group * hd), lambda b, h, i: (b * nqt + i, h)),
                  pl.BlockSpec((seq, hd), lambda b, h, i: (b, nq_heads + h)),
                  pl.BlockSpec((seq, hd), lambda b, h, i: (b, v_col0 // hd + h))],
        out_specs=pl.BlockSpec((tq, group * hd), lambda b, h, i: (b * nqt + i, h)),
        out_shape=jax.ShapeDtypeStruct((T, nq_heads * hd), BF16),
        scratch_shapes=[pltpu.VMEM((group * tq, hd), BF16),
                        pltpu.VMEM((seq, 2 * hd), BF16),
                        pltpu.VMEM((group * tq, tk), F32),
                        pltpu.VMEM((group * tq, tk), F32),
                        pltpu.VMEM((group * tq, tk), BF16),
                        pltpu.VMEM((group * tq, 2 * hd), F32),
                        pltpu.VMEM((group * tq, LANES), F32),
                        pltpu.VMEM((group * tq, LANES), F32)],
        compiler_params=_cparams("arbitrary", "arbitrary", "arbitrary"),
        name="flash_attn",
    )(qk, qk, v_src)


def _xattn_kernel(q_ref, kv_ref, o_ref, *, heads):
    d = q_ref.shape[1]
    hd = d // heads
    scale = hd ** -0.5
    for h in range(heads):
        q = q_ref[:, h * hd:(h + 1) * hd]
        k = kv_ref[:, h * hd:(h + 1) * hd]
        v = kv_ref[:, d + h * hd:d + (h + 1) * hd]
        s = lax.dot_general(q, k, (((1,), (1,)), ((), ())), preferred_element_type=F32) * scale
        p = jnp.exp(s - jnp.max(s, axis=-1, keepdims=True))
        l = jnp.sum(p, axis=-1, keepdims=True)
        o = _dot(p.astype(BF16), v) / l
        o_ref[:, h * hd:(h + 1) * hd] = o.astype(o_ref.dtype)


def xattn(q, kv, nb, seq, tq=512):
    T, d = q.shape
    n_mem = kv.shape[0] // nb
    tq = _tile(seq, tq)
    nqt = seq // tq
    return pl.pallas_call(
        functools.partial(_xattn_kernel, heads=XA_HEADS),
        grid=(nb, nqt),
        in_specs=[pl.BlockSpec((tq, d), lambda b, i: (b * nqt + i, 0)),
                  pl.BlockSpec((n_mem, 2 * d), lambda b, i: (b, 0))],
        out_specs=pl.BlockSpec((tq, d), lambda b, i: (b * nqt + i, 0)),
        out_shape=jax.ShapeDtypeStruct((T, d), BF16),
        compiler_params=_cparams("parallel", "arbitrary"),
        name="xattn",
    )(q, kv)


def _hyfilter_kernel(z_ref, t_ref, dl_ref, w1_ref, b1_ref, w2_ref, b2_ref, w3_ref, b3_ref,
                     fr_ref, wo_ref, h_ref, sum_ref, *, tl, d):
    i = pl.program_id(0)
    fr = fr_ref[...]
    h = jnp.sin(fr * (_dot_f32(z_ref[...], w1_ref[...]) + b1_ref[...]))
    h = jnp.sin(fr * (_dot_f32(h, w2_ref[...]) + b2_ref[...]))
    h = jnp.sin(fr * (_dot_f32(h, w3_ref[...]) + b3_ref[...]))
    window = jnp.exp(-t_ref[...] * dl_ref[...])
    rows = lax.broadcasted_iota(jnp.int32, (tl, 1), 0) + i * tl

    @pl.when(i == 0)
    def _():
        sum_ref[...] = jnp.zeros_like(sum_ref)

    for part in range(2):
        hp = _dot_f32(h, wo_ref[:, part * d:(part + 1) * d]) * window
        if part == 1:
            hp = jnp.where(rows == 0, 0.0, hp)
        h_ref[:, part * d:(part + 1) * d] = hp.astype(h_ref.dtype)
        sum_ref[:, part * d:(part + 1) * d] += jnp.sum(jnp.abs(hp), axis=0, keepdims=True)


def hyena_filter(z, t_col, deltas, w1, b1, w2, b2, w3, b3, freq, w_out, tl=256):
    L = z.shape[0]
    d2 = w_out.shape[1]
    d = d2 // 2
    fw = w2.shape[0]
    tl = _tile(L, tl)
    full = lambda a: pl.BlockSpec(a.shape, lambda i: (0,) * a.ndim)
    ops = [w1, b1.reshape(1, fw), w2, b2.reshape(1, fw), w3, b3.reshape(1, fw), freq.reshape(1, fw), w_out]
    return pl.pallas_call(
        functools.partial(_hyfilter_kernel, tl=tl, d=d),
        grid=(L // tl,),
        in_specs=[pl.BlockSpec((tl, z.shape[1]), lambda i: (i, 0)),
                  pl.BlockSpec((tl, 1), lambda i: (i, 0)),
                  full(deltas)] + [full(a) for a in ops],
        out_specs=[pl.BlockSpec((tl, d2), lambda i: (i, 0)),
                   pl.BlockSpec((1, d2), lambda i: (0, 0))],
        out_shape=[jax.ShapeDtypeStruct((L, d2), F32),
                   jax.ShapeDtypeStruct((1, d2), F32)],
        compiler_params=_cparams("arbitrary"),
        name="hyena_filter",
    )(z, t_col, deltas, *ops)


DFT_ROWS = 8


def _dft1_kernel(g_ref, u_ref, o_ref):
    kh, rt, tc = u_ref.shape
    u = u_ref[...].reshape(kh * rt, tc).astype(BF16)
    a = _dot(g_ref[...], u).astype(BF16)
    o_ref[...] = pltpu.bitcast(a, jnp.uint32).reshape(o_ref.shape)


def dft_stage1(g1, u4, tc=512):
    nb, kh, n2, c = u4.shape
    rt = DFT_ROWS
    n1 = g1.shape[0] // (2 * rt)
    tc = _tile(c, tc)
    return pl.pallas_call(
        _dft1_kernel,
        grid=(nb, n2 // rt, c // tc),
        in_specs=[pl.BlockSpec(g1.shape, lambda b, i, j: (0, 0)),
                  pl.BlockSpec((None, kh, rt, tc), lambda b, i, j: (b, 0, i, j))],
        out_specs=pl.BlockSpec((None, n1, rt, tc), lambda b, i, j: (b, 0, i, j)),
        out_shape=jax.ShapeDtypeStruct((nb, n1, n2, c), jnp.uint32),
        compiler_params=_cparams("parallel", "parallel", "arbitrary"),
        name="dft_stage1",
    )(g1, u4)


def _unpack_complex(ref):
    return pltpu.bitcast(ref[...], BF16)


def _spectrum_kernel(af_ref, ab_ref, h_ref, sum_ref, o_ref, *, d_cols):
    n2 = DFT_N2
    hm = h_ref[...]
    xf = _dot(hm, _unpack_complex(af_ref))
    xb = _dot(hm, _unpack_complex(ab_ref))
    inv = 1.0 / (sum_ref[:, 0:d_cols] + sum_ref[:, d_cols:2 * d_cols])
    o_ref[0] = (xf[:n2] + xb[:n2]) * inv
    o_ref[1] = (xf[n2:] - xb[n2:]) * inv


def filter_spectrum(a4, hmat, sums, d):
    n1 = a4.shape[1]
    n2 = DFT_N2
    return pl.pallas_call(
        functools.partial(_spectrum_kernel, d_cols=d),
        grid=(n1,),
        in_specs=[pl.BlockSpec((None, None, n2, d), lambda k: (0, k, 0, 0)),
                  pl.BlockSpec((None, None, n2, d), lambda k: (0, k, 0, 1)),
                  pl.BlockSpec((None, 2 * n2, 2 * n2), lambda k: (k, 0, 0)),
                  pl.BlockSpec((1, 2 * d), lambda k: (0, 0))],
        out_specs=pl.BlockSpec((2, None, n2, d), lambda k: (0, k, 0, 0)),
        out_shape=jax.ShapeDtypeStruct((2, n1, n2, d), F32),
        compiler_params=_cparams("arbitrary"),
        name="filter_spectrum",
    )(a4, a4, hmat, sums)


def _dftmid_kernel(a_ref, h_ref, g_ref, k_ref, o_ref):
    n2 = DFT_N2
    x = _dot(h_ref[...], _unpack_complex(a_ref))
    xr, xi = x[:n2], x[n2:]
    kr, ki = k_ref[0], k_ref[1]
    y = jnp.concatenate([xr * kr - xi * ki, xr * ki + xi * kr], axis=0).astype(BF16)
    zz = _dot(g_ref[...], y).astype(BF16)
    o_ref[...] = pltpu.bitcast(zz, jnp.uint32)


def dft_mid(a4, hmat, gmat, kspec):
    nb, n1, n2, c = a4.shape
    return pl.pallas_call(
        _dftmid_kernel,
        grid=(n1, nb),
        in_specs=[pl.BlockSpec((None, None, n2, c), lambda k, b: (b, k, 0, 0)),
                  pl.BlockSpec((None, 2 * n2, 2 * n2), lambda k, b: (k, 0, 0)),
                  pl.BlockSpec((None, 2 * n2, 2 * n2), lambda k, b: (k, 0, 0)),
                  pl.BlockSpec((2, None, n2, c), lambda k, b: (0, k, 0, 0))],
        out_specs=pl.BlockSpec((None, None, n2, c), lambda k, b: (b, k, 0, 0)),
        out_shape=jax.ShapeDtypeStruct(a4.shape, jnp.uint32),
        compiler_params=_cparams("parallel", "arbitrary"),
        name="dft_mid",
    )(a4, hmat, gmat, kspec)


def _dftout_kernel(g_ref, z_ref, x0_ref, w_ref, skip_ref, o_ref):
    n1, rt, tc = z_ref.shape
    z = pltpu.bitcast(z_ref[...].reshape(n1 * rt, tc), BF16)
    y = _dot(g_ref[...], z).reshape(o_ref.shape)
    o_ref[...] = x0_ref[...] * (y + w_ref[...] * skip_ref[...])


def dft_out(g2, z4, x0, w, skip, tc=512):
    nb, n1, n2, c = z4.shape
    rt = DFT_ROWS
    kh = g2.shape[0] // rt
    tc = _tile(c, tc)
    tok = pl.BlockSpec((None, kh, rt, tc), lambda b, i, j: (b, 0, i, j))
    return pl.pallas_call(
        _dftout_kernel,
        grid=(nb, n2 // rt, c // tc),
        in_specs=[pl.BlockSpec(g2.shape, lambda b, i, j: (0, 0)),
                  pl.BlockSpec((None, n1, rt, tc), lambda b, i, j: (b, 0, i, j)),
                  tok, tok,
                  pl.BlockSpec((1, 1, tc), lambda b, i, j: (0, 0, j))],
        out_specs=tok,
        out_shape=jax.ShapeDtypeStruct((nb, kh, n2, c), F32),
        compiler_params=_cparams("parallel", "parallel", "arbitrary"),
        name="dft_out",
    )(g2, z4, x0, w, skip.reshape(1, 1, c).astype(F32))


def _rope_tables(seq):
    hd = ATTN_HEAD_DIM
    axis_dim = hd // 2
    t = jnp.arange(seq)
    row = (t // GRID_W).astype(F32)
    col = (t % GRID_W).astype(F32)
    inv_freq = ROPE_THETA ** (-jnp.arange(0, axis_dim, 2, dtype=F32) / axis_dim)
    ang = jnp.concatenate([row[:, None] * inv_freq, col[:, None] * inv_freq], axis=-1)
    c, s = jnp.cos(ang), jnp.sin(ang)
    return jnp.concatenate([c, c], axis=-1), jnp.concatenate([-s, s], axis=-1)


def _hyena_features(seq, d):
    t = jnp.linspace(0.0, 1.0, seq, dtype=F32)
    w = 2.0 * math.pi * jnp.arange(seq, dtype=F32) / seq
    f = jnp.linspace(1e-4, HY_BANDS - 1, HY_BANDS, dtype=F32)
    fw = w[:, None] * f[None, :]
    z = jnp.concatenate([t[:, None], jnp.cos(fw), -jnp.sin(fw)], axis=-1)
    z = jnp.pad(z, ((0, 0), (0, LANES - HY_EMB)))
    deltas = jnp.abs(jnp.linspace(math.log(HY_TARGET) / HY_SLOW_PCT,
                                  math.log(HY_TARGET) / HY_FAST_PCT, d, dtype=F32))
    return z, t[:, None], deltas[None, :]


def _dft_tables(seq):
    n = 2 * seq
    n2 = DFT_N2
    n1 = n // n2

    def cs(phase_int, mod):
        ang = (-2.0 * math.pi / mod) * (phase_int % mod).astype(F32)
        return jnp.cos(ang), jnp.sin(ang)

    k1 = jnp.arange(n1)[:, None]
    m1 = jnp.arange(n1 // 2)[None, :]
    fr, fi = cs(k1 * m1, n1)
    base = jnp.stack([fr, fi], axis=-1)
    eye = jnp.eye(DFT_ROWS, dtype=F32)
    rows = DFT_ROWS
    g1 = jnp.einsum('knp,rs->krpns', base, eye).reshape(n1 * rows * 2, (n1 // 2) * rows).astype(BF16)
    g2 = (jnp.einsum('knp,rs->nrksp', base, eye).reshape((n1 // 2) * rows, n1 * rows * 2) / n).astype(BF16)
    kk = (jnp.arange(n1)[:, None, None] + n1 * jnp.arange(n2)[None, :, None])
    nn = jnp.arange(n2)[None, None, :]
    hr, hi = cs(kk * nn, n)
    hmat = jnp.concatenate([jnp.stack([hr, -hi], axis=-1).reshape(n1, n2, 2 * n2),
                            jnp.stack([hi, hr], axis=-1).reshape(n1, n2, 2 * n2)], axis=1).astype(BF16)
    gr, gi = jnp.swapaxes(hr, 1, 2), -jnp.swapaxes(hi, 1, 2)
    gmat = jnp.stack([jnp.concatenate([gr, -gi], axis=2),
                      jnp.concatenate([gi, gr], axis=2)], axis=2).reshape(n1, 2 * n2, 2 * n2).astype(BF16)
    return g1, g2, hmat, gmat


def _deinterleave(nheads):
    hd = ATTN_HEAD_DIM
    one = np.concatenate([np.arange(0, hd, 2), np.arange(1, hd, 2)])
    return np.concatenate([h * hd + one for h in range(nheads)])


def kernel(x_prompt, x_sample, mem_prompt, mem_sample, norm_mix, norm_xa, norm_mem, norm_ffn, xa_wq, xa_wk, xa_wv, xa_wo, ffn_w_in, ffn_conv_w, ffn_conv_b, ffn_w_out, mix_w_in, mix_w_out, ssd_conv_w, ssd_conv_b, ssd_a_log, ssd_dt_bias, ssd_d, ssd_norm, attn_q_norm, attn_k_norm, hy_w_in, hy_conv_w, hy_conv_b, hy_f_w1, hy_f_b1, hy_f_w2, hy_f_b2, hy_f_w3, hy_f_b3, hy_f_freq, hy_f_w_out, hy_skip, hy_w_out, final_norm):
    nbp, seq, d = x_prompt.shape
    nbs = x_sample.shape[0]
    assert x_sample.shape[1] == seq
    nb = nbp + nbs
    T = nb * seq
    depth = norm_mix.shape[0]
    n_mem = mem_prompt.shape[1]
    d_ff = ffn_w_out.shape[1]

    x = jnp.concatenate([x_prompt, x_sample], axis=0).reshape(T, d)
    mem = jnp.concatenate([mem_prompt, mem_sample], axis=0).reshape(nb * n_mem, d)

    d_ssd = d
    nheads = d_ssd // SSD_HEAD_DIM
    gn = SSD_GROUPS * SSD_STATE
    conv_ch = d_ssd + 2 * gn
    n_att = d // ATTN_HEAD_DIM
    d_kv = ATTN_KV_HEADS * ATTN_HEAD_DIM
    o1 = d_ssd
    o2 = o1 + conv_ch
    o3 = o2 + 2 * nheads
    o4 = o3 + d
    o5 = o4 + d_kv
    qcol = o1
    vcol = o1 + d + d_kv
    cos, sin = _rope_tables(seq)
    perm_q = _deinterleave(n_att)
    perm_k = _deinterleave(ATTN_KV_HEADS)
    perm_h = _deinterleave(1)

    n2 = DFT_N2
    n1 = 2 * seq // n2
    hz, t_col, deltas = _hyena_features(seq, d)
    g1, g2, hmat, gmat = _dft_tables(seq)

    for i in range(depth):
        if i % 2 == 0:
            e = i // 2
            w = mix_w_in[e].astype(BF16)
            w_main = jnp.concatenate([w[:, :o1], w[:, o3:o4][:, perm_q], w[:, o4:o5][:, perm_k], w[:, o5:]],
                                     axis=1)
            w_dt = jnp.pad(w[:, o2:o3], ((0, 0), (0, LANES - 2 * nheads)))
            proj = normmm(x, norm_mix[i], w_main)
            dtraw = normmm(x, norm_mix[i], w_dt, out_dtype=F32, tn=LANES)
            (xbc,) = normmm_conv(x, norm_mix[i], w[:, o1:o2], ssd_conv_w[e], ssd_conv_b[e],
                                 [0], conv_ch, _epi_silu, 1, seq)
            pad_row = lambda a: jnp.pad(a.reshape(1, -1).astype(F32), ((0, 0), (0, LANES - 2 * nheads)))
            y_ssd = ssd_scan(xbc, dtraw, pad_row(ssd_dt_bias[e]), pad_row(ssd_a_log[e]),
                             jnp.repeat(ssd_d[e].astype(F32), SSD_HEAD_DIM)[None, :],
                             proj, ssd_norm[e].reshape(1, -1).astype(F32), nb, seq)
            scale = ATTN_HEAD_DIM ** -0.5 * math.log2(math.e)
            gains = jnp.concatenate([jnp.tile(attn_q_norm[e][perm_h][None, :] * scale, (n_att, 1)),
                                     jnp.tile(attn_k_norm[e][perm_h][None, :], (ATTN_KV_HEADS, 1))],
                                    axis=0)[:, None, :].astype(F32)
            qk = qk_prep(proj, qcol, n_att, ATTN_KV_HEADS, gains, cos, sin, seq)
            y_att = flash_attention(qk, proj, vcol, n_att, nb, seq)
            x = mm_res(jnp.concatenate([y_ssd, y_att], axis=1), mix_w_out[e].astype(BF16), x)
        else:
            o = i // 2
            x0, wv = normmm_conv(x, norm_mix[i], hy_w_in[o].astype(BF16), hy_conv_w[o], hy_conv_b[o],
                                 [0, d, 2 * d], d, _epi_hyena, 2, seq, out_dtype=F32)
            w1 = jnp.pad(hy_f_w1[o], ((0, LANES - HY_EMB), (0, 0)))
            hfb, sums = hyena_filter(hz, t_col, deltas, w1, hy_f_b1[o], hy_f_w2[o], hy_f_b2[o],
                                     hy_f_w3[o], hy_f_b3[o], hy_f_freq[o], hy_f_w_out[o])
            a_f = dft_stage1(g1, hfb.reshape(1, n1 // 2, n2, 2 * d))
            kspec = filter_spectrum(a_f, hmat, sums, d)
            a_u = dft_stage1(g1, wv.reshape(nb, n1 // 2, n2, d))
            zz = dft_mid(a_u, hmat, gmat, kspec)
            yh = dft_out(g2, zz, x0.reshape(nb, n1 // 2, n2, d), wv.reshape(nb, n1 // 2, n2, d), hy_skip[o])
            x = mm_res(yh.reshape(T, d), hy_w_out[o].astype(BF16), x)
        q_tiles = [(1024, 1024), (1024, 2048), (512, 2048), (1024, 512)][i % 4]
        o_tiles = [(1024, 1024), (512, 2048), (1024, 512), (2048, 512)][i % 4]
        q = normmm(x, norm_xa[i], xa_wq[i].astype(BF16), tm=q_tiles[0], tn=q_tiles[1])
        kv = normmm(mem, norm_mem[i], jnp.concatenate([xa_wk[i].astype(BF16), xa_wv[i].astype(BF16)], axis=1))
        x = mm_res(xattn(q, kv, nb, seq), xa_wo[i].astype(BF16), x, tm=o_tiles[0], tn=o_tiles[1])
        (act,) = normmm_conv(x, norm_ffn[i], ffn_w_in[i].astype(BF16), ffn_conv_w[i], ffn_conv_b[i],
                             [0, d_ff], d_ff, _epi_glu, 1, seq)
        f_tiles = [(1024, 512), (512, 1024), (1024, 256), (512, 512)][i % 4]
        x = mm_res(act, ffn_w_out[i].astype(BF16), x, tm=f_tiles[0], tn=f_tiles[1])

    y_prompt = rmsnorm(x, final_norm, 0, nbp * seq).reshape(nbp, seq, d)
    y_sample = rmsnorm(x, final_norm, nbp * seq, nbs * seq).reshape(nbs, seq, d)
    return (y_prompt, y_sample)
```

```python
import functools
import math

import numpy as np
import jax
import jax.numpy as jnp
from jax import lax
from jax.experimental import pallas as pl
from jax.experimental.pallas import tpu as pltpu

F32 = jnp.float32
BF16 = jnp.bfloat16
EPS = 1e-6

GRID_W = 64
XA_HEADS = 4
SSD_HEAD_DIM = 64
SSD_GROUPS = 4
SSD_STATE = 128
SSD_CHUNK = 128
ATTN_HEAD_DIM = 128
ATTN_KV_HEADS = 4
ROPE_THETA = 10000.0
HY_EMB = 33
HY_BANDS = (HY_EMB - 1) // 2
HY_TARGET = 1e-2
HY_FAST_PCT = 0.3
HY_SLOW_PCT = 1.5

LANES = 128
MXU_WIDTH = 256
DFT_N2 = 128
VMEM_LIMIT = 56 * 1024 * 1024
NEG_BIG = -1e30


def _cparams(*sem):
    return pltpu.CompilerParams(dimension_semantics=sem, vmem_limit_bytes=VMEM_LIMIT)


def _tile(dim, pref):
    t = min(dim, pref)
    while dim % t:
        t //= 2
    return t


def _split3(x):
    hi = x.astype(BF16)
    r1 = x - hi.astype(F32)
    mid = r1.astype(BF16)
    lo = (r1 - mid.astype(F32)).astype(BF16)
    return hi, mid, lo


def _dot(a, b):
    return jnp.dot(a, b, preferred_element_type=F32)


def _dot_exact_rhs(x, e):
    hi, mid, lo = _split3(x)
    return _dot(hi, e) + _dot(mid, e) + _dot(lo, e)


def _dot_exact_lhs(e, x):
    hi, mid, lo = _split3(x)
    return _dot(e, hi) + _dot(e, mid) + _dot(e, lo)


def _dot_f32(a, b):
    ah, am, _ = _split3(a)
    bh, bm, _ = _split3(b)
    return _dot(ah, bh) + _dot(ah, bm) + _dot(am, bh)


def _silu(x):
    return x * (1.0 / (1.0 + jnp.exp(-x)))


def _normmm_kernel(x_ref, g_ref, w_ref, o_ref, xn_ref):
    @pl.when(pl.program_id(1) == 0)
    def _():
        x = x_ref[...].astype(F32)
        ms = jnp.mean(x * x, axis=-1, keepdims=True)
        xn_ref[...] = (x * lax.rsqrt(ms + EPS) * g_ref[...]).astype(BF16)

    o_ref[...] = _dot(xn_ref[...], w_ref[...]).astype(o_ref.dtype)


def normmm(x, g, w, out_dtype=BF16, tm=1024, tn=1024):
    M, K = x.shape
    N = w.shape[1]
    tm = _tile(M, tm)
    tn = _tile(N, tn)
    return pl.pallas_call(
        _normmm_kernel,
        grid=(M // tm, N // tn),
        in_specs=[pl.BlockSpec((tm, K), lambda i, j: (i, 0)),
                  pl.BlockSpec((1, K), lambda i, j: (0, 0)),
                  pl.BlockSpec((K, tn), lambda i, j: (0, j),
                               pipeline_mode=pl.Buffered(1) if tn == N else None)],
        out_specs=pl.BlockSpec((tm, tn), lambda i, j: (i, j)),
        out_shape=jax.ShapeDtypeStruct((M, N), out_dtype),
        scratch_shapes=[pltpu.VMEM((tm, K), BF16)],
        compiler_params=_cparams("parallel", "arbitrary"),
        name="normmm",
    )(x, g.reshape(1, K).astype(F32), w)


def _mmres_kernel(a_ref, w_ref, r_ref, o_ref):
    o_ref[...] = r_ref[...] + _dot(a_ref[...].astype(BF16), w_ref[...])


def mm_res(a, w, res, tm=1024, tn=None):
    M, K = a.shape
    N = w.shape[1]
    tm = _tile(M, tm)
    tn = _tile(N, tn or (1024 if K <= 2048 else 512))
    w_mode = pl.Buffered(1) if tn == N else None
    return pl.pallas_call(
        _mmres_kernel,
        grid=(M // tm, N // tn),
        in_specs=[pl.BlockSpec((tm, K), lambda i, j: (i, 0)),
                  pl.BlockSpec((K, tn), lambda i, j: (0, j), pipeline_mode=w_mode),
                  pl.BlockSpec((tm, tn), lambda i, j: (i, j))],
        out_specs=pl.BlockSpec((tm, tn), lambda i, j: (i, j)),
        out_shape=jax.ShapeDtypeStruct((M, N), F32),
        compiler_params=_cparams("parallel", "arbitrary"),
        name="mm_res",
    )(a, w, res)


def _rmsnorm_kernel(x_ref, g_ref, o_ref):
    x = x_ref[...]
    ms = jnp.mean(x * x, axis=-1, keepdims=True)
    o_ref[...] = x * lax.rsqrt(ms + EPS) * g_ref[...]


def rmsnorm(x, g, row0, rows, tm=512):
    K = x.shape[1]
    tm = _tile(math.gcd(row0, rows) if row0 else rows, tm)
    return pl.pallas_call(
        _rmsnorm_kernel,
        grid=(rows // tm,),
        in_specs=[pl.BlockSpec((tm, K), lambda i: (i + row0 // tm, 0)),
                  pl.BlockSpec((1, K), lambda i: (0, 0))],
        out_specs=pl.BlockSpec((tm, K), lambda i: (i, 0)),
        out_shape=jax.ShapeDtypeStruct((rows, K), F32),
        compiler_params=_cparams("parallel"),
        name="final_norm",
    )(x, g.reshape(1, K).astype(F32))


CONV_HALO = 16
CONV_ROWS = 64


def _normmm_conv_kernel(*refs, nseg, width, tm, seq, epilogue, nout):
    xm_ref, xp_ref, xn_ref, g_ref = refs[:4]
    segs = [refs[4 + 3 * s:7 + 3 * s] for s in range(nseg)]
    outs = refs[4 + 3 * nseg:4 + 3 * nseg + nout]
    hn_ref = refs[4 + 3 * nseg + nout]
    exts = refs[5 + 3 * nseg + nout:]
    h = CONV_HALO
    half = width // 2
    row0 = pl.program_id(0) * tm
    at_start = (row0 % seq) == 0
    at_end = ((row0 + tm) % seq) == 0

    @pl.when(pl.program_id(1) == 0)
    def _():
        def nrm(x):
            ms = jnp.mean(x * x, axis=-1, keepdims=True)
            return (x * lax.rsqrt(ms + EPS) * g_ref[...]).astype(BF16)
        hn_ref[0:h, :] = nrm(xp_ref[...])
        hn_ref[h:h + tm, :] = nrm(xm_ref[...])
        hn_ref[h + tm:h + tm + h, :] = nrm(xn_ref[...])

    tn = exts[0].shape[1]
    for (w_ref, _, _), ext in zip(segs, exts):
        ext[...] = _dot(hn_ref[...], w_ref[...])
        ext[0:h, :] = jnp.where(at_start, 0.0, ext[0:h, :])
        ext[h + tm:h + tm + h, :] = jnp.where(at_end, 0.0, ext[h + tm:h + tm + h, :])

    for rb in range(0, tm, CONV_ROWS):
        for lc in range(0, tn, LANES):
            ls = slice(lc, lc + LANES)
            vals = []
            for (_, cw_ref, cb_ref), ext in zip(segs, exts):
                acc = None
                for k in range(width):
                    term = ext[h - half + k + rb:h - half + k + rb + CONV_ROWS, ls] * cw_ref[k:k + 1, ls]
                    acc = term if acc is None else acc + term
                vals.append(acc + cb_ref[:, ls])
            for o, r in zip(outs, epilogue(*vals)):
                o[rb:rb + CONV_ROWS, ls] = r.astype(o.dtype)


def normmm_conv(x, g, w, conv_w, conv_b, seg_cols, width_cols, epilogue, nout, seq, out_dtype=BF16,
                tm=1024, tn=512):
    T, K = x.shape
    width = conv_w.shape[0]
    nseg = len(seg_cols)
    tm = _tile(seq, tm)
    tn = _tile(width_cols, tn)
    assert tm % CONV_ROWS == 0 and tn % LANES == 0
    h = CONV_HALO
    nrb = T // h
    cb = conv_b.reshape(1, -1).astype(F32)
    cw = conv_w.astype(F32)
    in_specs = [pl.BlockSpec((tm, K), lambda i, j: (i, 0)),
                pl.BlockSpec((h, K), lambda i, j: (jnp.maximum(i * (tm // h) - 1, 0), 0)),
                pl.BlockSpec((h, K), lambda i, j: (jnp.minimum((i + 1) * (tm // h), nrb - 1), 0)),
                pl.BlockSpec((1, K), lambda i, j: (0, 0))]
    args = [x, x, x, g.reshape(1, K).astype(F32)]
    for c0 in seg_cols:
        off = c0 // tn
        in_specs += [pl.BlockSpec((K, tn), lambda i, j, off=off: (0, j + off)),
                     pl.BlockSpec((width, tn), lambda i, j, off=off: (0, j + off)),
                     pl.BlockSpec((1, tn), lambda i, j, off=off: (0, j + off))]
        args += [w, cw, cb]
    kern = functools.partial(_normmm_conv_kernel, nseg=nseg, width=width, tm=tm, seq=seq,
                             epilogue=epilogue, nout=nout)
    return pl.pallas_call(
        kern,
        grid=(T // tm, width_cols // tn),
        in_specs=in_specs,
        out_specs=[pl.BlockSpec((tm, tn), lambda i, j: (i, j)) for _ in range(nout)],
        out_shape=[jax.ShapeDtypeStruct((T, width_cols), out_dtype) for _ in range(nout)],
        scratch_shapes=[pltpu.VMEM((tm + 2 * h, K), BF16)]
        + [pltpu.VMEM((tm + 2 * h, tn), F32) for _ in range(nseg)],
        compiler_params=_cparams("parallel", "arbitrary"),
        name="normmm_conv",
    )(*args)


def _epi_silu(c):
    return (_silu(c),)


def _epi_glu(g, up):
    return (_silu(g) * up,)


def _epi_hyena(x0, x1, v):
    return (x0, v * x1)


def _softplus(x):
    return jnp.maximum(x, 0.0) + jnp.log(1.0 + jnp.exp(-jnp.abs(x)))


def _ssd_kernel(*refs, rev, nheads):
    if rev:
        (xs_ref, b_ref, c_ref, dt_ref, bias_ref, alog_ref, e_ref,
         yf_ref, z_ref, gain_ref, o_ref, s_ref, y_ref) = refs
    else:
        (xs_ref, b_ref, c_ref, dt_ref, bias_ref, alog_ref, e_ref,
         dskip_ref, o_ref, s_ref) = refs
        y_ref = o_ref
    Q = SSD_CHUNK
    P = SSD_HEAD_DIM
    hpg = nheads // SSD_GROUPS
    gw = hpg * P
    hoff = nheads if rev else 0

    @pl.when(pl.program_id(1) == 0)
    def _():
        s_ref[...] = jnp.zeros_like(s_ref)

    row = lax.broadcasted_iota(jnp.int32, (Q, Q), 0)
    col = lax.broadcasted_iota(jnp.int32, (Q, Q), 1)
    mask = (col >= row) if rev else (col <= row)
    tri = jnp.where(mask, 1.0, 0.0).astype(BF16)

    dtv = _softplus(dt_ref[...] + bias_ref[...])
    a_row = -jnp.exp(alog_ref[...])
    la = dtv * a_row
    cs = _dot_exact_lhs(tri, la)
    tot = cs[0:1, :] if rev else cs[Q - 1:Q, :]
    cs_t = cs.T
    dt_t = dtv.T
    e = e_ref[...]
    ecs_hi, ecs_lo, _ = _split3(jnp.exp(cs))
    carry_in = _dot(ecs_hi, e) + _dot(ecs_lo, e)
    to_end = _dot((jnp.exp(tot - cs) * dtv).astype(BF16), e)
    dec = _dot_exact_rhs(jnp.broadcast_to(jnp.exp(tot), (8, LANES)), e)[0:1, :]

    xs = xs_ref[...]
    x_state = (xs.astype(F32) * to_end).astype(BF16)
    lane = lax.broadcasted_iota(jnp.int32, (Q, LANES), 1)
    low = lane < P

    for g in range(SSD_GROUPS):
        bg = b_ref[:, g * SSD_STATE:(g + 1) * SSD_STATE]
        cg = c_ref[:, g * SSD_STATE:(g + 1) * SSD_STATE]
        cb = lax.dot_general(cg, bg, (((1,), (1,)), ((), ())), preferred_element_type=F32)
        s_old = s_ref[g]
        y_off = _dot(cg, s_old.astype(BF16)) * carry_in[:, g * gw:(g + 1) * gw]
        s_ref[g] = s_old * dec[:, g * gw:(g + 1) * gw] + lax.dot_general(
            bg, x_state[:, g * gw:(g + 1) * gw], (((0,), (0,)), ((), ())), preferred_element_type=F32)
        for j in range(hpg // 2):
            ws = []
            for hh in range(2):
                hc = hoff + g * hpg + 2 * j + hh
                diff = cs[:, hc:hc + 1] - cs_t[hc:hc + 1, :]
                decay = jnp.exp(jnp.where(mask, diff, NEG_BIG))
                ws.append((cb * decay * dt_t[hc:hc + 1, :]).astype(BF16))
            c0 = g * gw + 2 * j * P
            xp = xs[:, c0:c0 + LANES]
            rhs = jnp.concatenate([jnp.where(low, xp, jnp.zeros_like(xp)),
                                   jnp.where(low, jnp.zeros_like(xp), xp)], axis=0)
            y = _dot(jnp.concatenate(ws, axis=1), rhs) + y_off[:, 2 * j * P:2 * j * P + LANES]
            if not rev:
                y = y + xp.astype(F32) * dskip_ref[:, c0:c0 + LANES]
            y_ref[:, c0:c0 + LANES] = y

    if rev:
        y = y_ref[...] + yf_ref[...]
        gated = y * _silu(z_ref[...].astype(F32))
        ms = jnp.mean(gated * gated, axis=-1, keepdims=True)
        o_ref[...] = (gated * lax.rsqrt(ms + EPS) * gain_ref[...]).astype(o_ref.dtype)


def ssd_scan(xbc, dtraw, bias_row, alog_row, d_row, z_src, gain_row, nb, seq):
    T = xbc.shape[0]
    Q = SSD_CHUNK
    nc = seq // Q
    gn = SSD_GROUPS * SSD_STATE
    hp = xbc.shape[1] - 2 * gn
    nheads = hp // SSD_HEAD_DIM
    hpg = nheads // SSD_GROUPS
    gw = hpg * SSD_HEAD_DIM
    assert hp % gn == 0 and 2 * nheads <= LANES

    def e_mat(off):
        r = np.arange(LANES)[:, None]
        c = np.arange(hp)[None, :]
        return jnp.asarray((r == off + c // SSD_HEAD_DIM).astype(np.float32), dtype=BF16)

    def specs(rev):
        def blk(c):
            return (nc - 1 - c) if rev else c
        return [
            pl.BlockSpec((Q, hp), lambda b, c: (b * nc + blk(c), 0)),
            pl.BlockSpec((Q, gn), lambda b, c: (b * nc + blk(c), hp // gn)),
            pl.BlockSpec((Q, gn), lambda b, c: (b * nc + blk(c), hp // gn + 1)),
            pl.BlockSpec((Q, LANES), lambda b, c: (b * nc + blk(c), 0)),
            pl.BlockSpec((1, LANES), lambda b, c: (0, 0)),
            pl.BlockSpec((1, LANES), lambda b, c: (0, 0)),
            pl.BlockSpec((LANES, hp), lambda b, c: (0, 0)),
        ], (lambda b, c: (b * nc + blk(c), 0))

    in_f, omap_f = specs(False)
    yf = pl.pallas_call(
        functools.partial(_ssd_kernel, rev=False, nheads=nheads),
        grid=(nb, nc),
        in_specs=in_f + [pl.BlockSpec((1, hp), lambda b, c: (0, 0))],
        out_specs=pl.BlockSpec((Q, hp), omap_f),
        out_shape=jax.ShapeDtypeStruct((T, hp), F32),
        scratch_shapes=[pltpu.VMEM((SSD_GROUPS, SSD_STATE, gw), F32)],
        compiler_params=_cparams("parallel", "arbitrary"),
        name="ssd_fwd",
    )(xbc, xbc, xbc, dtraw, bias_row, alog_row, e_mat(0), d_row)
    in_b, omap_b = specs(True)
    return pl.pallas_call(
        functools.partial(_ssd_kernel, rev=True, nheads=nheads),
        grid=(nb, nc),
        in_specs=in_b + [pl.BlockSpec((Q, hp), omap_b),
                         pl.BlockSpec((Q, hp), omap_b),
                         pl.BlockSpec((1, hp), lambda b, c: (0, 0))],
        out_specs=pl.BlockSpec((Q, hp), omap_b),
        out_shape=jax.ShapeDtypeStruct((T, hp), BF16),
        scratch_shapes=[pltpu.VMEM((SSD_GROUPS, SSD_STATE, gw), F32),
                        pltpu.VMEM((Q, hp), F32)],
        compiler_params=_cparams("parallel", "arbitrary"),
        name="ssd_bwd",
    )(xbc, xbc, xbc, dtraw, bias_row, alog_row, e_mat(nheads), yf, z_src, gain_row)


def _qkprep_kernel(q_ref, k_ref, g_ref, cos_ref, sin_ref, o_ref, *, nq, nk):
    hd = ATTN_HEAD_DIM
    cos = cos_ref[...]
    sin = sin_ref[...]
    for h in range(nq + nk):
        src, c0 = (q_ref, h * hd) if h < nq else (k_ref, (h - nq) * hd)
        x = src[:, c0:c0 + hd].astype(F32)
        ms = jnp.mean(x * x, axis=-1, keepdims=True)
        xn = x * lax.rsqrt(ms + EPS) * g_ref[h]
        o_ref[:, h * hd:(h + 1) * hd] = (xn * cos + pltpu.roll(xn, hd // 2, 1) * sin).astype(o_ref.dtype)


def qk_prep(proj, qcol, nq, nk, gains, cos, sin, seq, tq=256):
    T = proj.shape[0]
    hd = ATTN_HEAD_DIM
    tq = _tile(seq, tq)
    spt = seq // tq
    assert qcol % (nq * hd) == 0 and (qcol + nq * hd) % (nk * hd) == 0
    return pl.pallas_call(
        functools.partial(_qkprep_kernel, nq=nq, nk=nk),
        grid=(T // tq,),
        in_specs=[pl.BlockSpec((tq, nq * hd), lambda i: (i, qcol // (nq * hd))),
                  pl.BlockSpec((tq, nk * hd), lambda i: (i, (qcol + nq * hd) // (nk * hd))),
                  pl.BlockSpec((nq + nk, 1, hd), lambda i: (0, 0, 0)),
                  pl.BlockSpec((tq, hd), lambda i: (i % spt, 0)),
                  pl.BlockSpec((tq, hd), lambda i: (i % spt, 0))],
        out_specs=pl.BlockSpec((tq, (nq + nk) * hd), lambda i: (i, 0)),
        out_shape=jax.ShapeDtypeStruct((T, (nq + nk) * hd), BF16),
        compiler_params=_cparams("parallel"),
        name="qk_prep",
    )(proj, proj, gains, cos, sin)


FLASH_ROW_BLOCK = 32


def _flash_kernel(q_ref, k_ref, v_ref, o_ref, qs_ref, va_ref, s0_ref, s1_ref, p_ref, acc_ref,
                  m_ref, al_ref, *, tk, group):
    hd = ATTN_HEAD_DIM
    tq = q_ref.shape[0]
    rows = group * tq
    seq = k_ref.shape[0]
    nk = seq // tk

    @pl.when(pl.program_id(2) == 0)
    def _():
        va_ref[:, 0:hd] = v_ref[...]
        va_ref[:, hd:2 * hd] = jnp.ones((seq, hd), BF16)

    for g in range(group):
        qs_ref[g * tq:(g + 1) * tq, :] = q_ref[:, g * hd:(g + 1) * hd]
    m_ref[...] = jnp.full(m_ref.shape, NEG_BIG, F32)
    acc_ref[...] = jnp.zeros(acc_ref.shape, F32)
    nlc = tk // LANES

    def scores(t, s_ref):
        k0 = pl.multiple_of(t * tk, tk)
        s_ref[...] = lax.dot_general(qs_ref[...], k_ref[pl.ds(k0, tk), :], (((1,), (1,)), ((), ())),
                                     preferred_element_type=F32)

    def update(t, s_ref):
        for r0 in range(0, rows, FLASH_ROW_BLOCK):
            rs = slice(r0, r0 + FLASH_ROW_BLOCK)
            ch = [s_ref[rs, c * LANES:(c + 1) * LANES] for c in range(nlc)]
            mx = ch[0]
            for c in range(1, nlc):
                mx = jnp.maximum(mx, ch[c])
            m_old = m_ref[rs, :]
            m_new = jnp.maximum(m_old, jnp.max(mx, axis=-1, keepdims=True))
            m_ref[rs, :] = m_new
            al_ref[rs, :] = jnp.exp2(m_old - m_new)
            for c in range(nlc):
                p_ref[rs, c * LANES:(c + 1) * LANES] = jnp.exp2(ch[c] - m_new).astype(BF16)
        k0 = pl.multiple_of(t * tk, tk)
        pv = _dot(p_ref[...], va_ref[pl.ds(k0, tk), :])
        al = al_ref[...]
        acc_ref[...] = acc_ref[...] * jnp.concatenate([al, al], axis=1) + pv

    scores(0, s0_ref)

    def body(t2, carry):
        scores(2 * t2 + 1, s1_ref)
        update(2 * t2, s0_ref)
        scores(2 * t2 + 2, s0_ref)
        update(2 * t2 + 1, s1_ref)
        return carry

    lax.fori_loop(0, nk // 2 - 1, body, 0)
    scores(nk - 1, s1_ref)
    update(nk - 2, s0_ref)
    update(nk - 1, s1_ref)
    o = acc_ref[:, 0:hd] / acc_ref[:, hd:2 * hd]
    for g in range(group):
        o_ref[:, g * hd:(g + 1) * hd] = o[g * tq:(g + 1) * tq, :].astype(o_ref.dtype)


def flash_attention(qk, v_src, v_col0, nq_heads, nb, seq, tq=256, tk=1024):
    T = qk.shape[0]
    hd = ATTN_HEAD_DIM
    nkv = ATTN_KV_HEADS
    group = nq_heads // nkv
    tq = _tile(seq, tq)
    tk = _tile(seq // 2, tk)
    nqt = seq // tq
    return pl.pallas_call(
        functools.partial(_flash_kernel, tk=tk, group=group),
        grid=(nb, nkv, nqt),
        in_specs=[pl.BlockSpec((tq, group * hd), lambda b, h, i: (b * nqt + i, h)),
                  pl.BlockSpec((seq, hd), lambda b, h, i: (b, nq_heads + h)),
                  pl.BlockSpec((seq, hd), lambda b, h, i: (b, v_col0 // hd + h))],
        out_specs=pl.BlockSpec((tq, group * hd), lambda b, h, i: (b * nqt + i, h)),
        out_shape=jax.ShapeDtypeStruct((T, nq_heads * hd), BF16),
        scratch_shapes=[pltpu.VMEM((group * tq, hd), BF16),
                        pltpu.VMEM((seq, 2 * hd), BF16),
                        pltpu.VMEM((group * tq, tk), F32),
                        pltpu.VMEM((group * tq, tk), F32),
                        pltpu.VMEM((group * tq, tk), BF16),
                        pltpu.VMEM((group * tq, 2 * hd), F32),
                        pltpu.VMEM((group * tq, LANES), F32),
                        pltpu.VMEM((group * tq, LANES), F32)],
        compiler_params=_cparams("arbitrary", "arbitrary", "arbitrary"),
        name="flash_attn",
    )(qk, qk, v_src)


def _xattn_kernel(q_ref, kv_ref, o_ref, *, heads):
    d = q_ref.shape[1]
    hd = d // heads
    scale = hd ** -0.5
    for h in range(heads):
        q = q_ref[:, h * hd:(h + 1) * hd]
        k = kv_ref[:, h * hd:(h + 1) * hd]
        v = kv_ref[:, d + h * hd:d + (h + 1) * hd]
        s = lax.dot_general(q, k, (((1,), (1,)), ((), ())), preferred_element_type=F32) * scale
        p = jnp.exp(s - jnp.max(s, axis=-1, keepdims=True))
        l = jnp.sum(p, axis=-1, keepdims=True)
        o = _dot(p.astype(BF16), v) / l
        o_ref[:, h * hd:(h + 1) * hd] = o.astype(o_ref.dtype)


def xattn(q, kv, nb, seq, tq=512):
    T, d = q.shape
    n_mem = kv.shape[0] // nb
    tq = _tile(seq, tq)
    nqt = seq // tq
    return pl.pallas_call(
        functools.partial(_xattn_kernel, heads=XA_HEADS),
        grid=(nb, nqt),
        in_specs=[pl.BlockSpec((tq, d), lambda b, i: (b * nqt + i, 0)),
                  pl.BlockSpec((n_mem, 2 * d), lambda b, i: (b, 0))],
        out_specs=pl.BlockSpec((tq, d), lambda b, i: (b * nqt + i, 0)),
        out_shape=jax.ShapeDtypeStruct((T, d), BF16),
        compiler_params=_cparams("parallel", "arbitrary"),
        name="xattn",
    )(q, kv)


def _hyfilter_kernel(z_ref, t_ref, dl_ref, w1_ref, b1_ref, w2_ref, b2_ref, w3_ref, b3_ref,
                     fr_ref, wo_ref, h_ref, sum_ref, *, tl, d):
    i = pl.program_id(0)
    fr = fr_ref[...]
    h = jnp.sin(fr * (_dot_f32(z_ref[...], w1_ref[...]) + b1_ref[...]))
    h = jnp.sin(fr * (_dot_f32(h, w2_ref[...]) + b2_ref[...]))
    h = jnp.sin(fr * (_dot_f32(h, w3_ref[...]) + b3_ref[...]))
    window = jnp.exp(-t_ref[...] * dl_ref[...])
    rows = lax.broadcasted_iota(jnp.int32, (tl, 1), 0) + i * tl

    @pl.when(i == 0)
    def _():
        sum_ref[...] = jnp.zeros_like(sum_ref)

    for part in range(2):
        hp = _dot_f32(h, wo_ref[:, part * d:(part + 1) * d]) * window
        if part == 1:
            hp = jnp.where(rows == 0, 0.0, hp)
        h_ref[:, part * d:(part + 1) * d] = hp.astype(h_ref.dtype)
        sum_ref[:, part * d:(part + 1) * d] += jnp.sum(jnp.abs(hp), axis=0, keepdims=True)


def hyena_filter(z, t_col, deltas, w1, b1, w2, b2, w3, b3, freq, w_out, tl=256):
    L = z.shape[0]
    d2 = w_out.shape[1]
    d = d2 // 2
    fw = w2.shape[0]
    tl = _tile(L, tl)
    full = lambda a: pl.BlockSpec(a.shape, lambda i: (0,) * a.ndim)
    ops = [w1, b1.reshape(1, fw), w2, b2.reshape(1, fw), w3, b3.reshape(1, fw), freq.reshape(1, fw), w_out]
    return pl.pallas_call(
        functools.partial(_hyfilter_kernel, tl=tl, d=d),
        grid=(L // tl,),
        in_specs=[pl.BlockSpec((tl, z.shape[1]), lambda i: (i, 0)),
                  pl.BlockSpec((tl, 1), lambda i: (i, 0)),
                  full(deltas)] + [full(a) for a in ops],
        out_specs=[pl.BlockSpec((tl, d2), lambda i: (i, 0)),
                   pl.BlockSpec((1, d2), lambda i: (0, 0))],
        out_shape=[jax.ShapeDtypeStruct((L, d2), F32),
                   jax.ShapeDtypeStruct((1, d2), F32)],
        compiler_params=_cparams("arbitrary"),
        name="hyena_filter",
    )(z, t_col, deltas, *ops)


DFT_ROWS = 8


def _dft1_kernel(g_ref, u_ref, o_ref):
    kh, rt, tc = u_ref.shape
    u = u_ref[...].reshape(kh * rt, tc).astype(BF16)
    a = _dot(g_ref[...], u).astype(BF16)
    o_ref[...] = pltpu.bitcast(a, jnp.uint32).reshape(o_ref.shape)


def dft_stage1(g1, u4, tc=512):
    nb, kh, n2, c = u4.shape
    rt = DFT_ROWS
    n1 = g1.shape[0] // (2 * rt)
    tc = _tile(c, tc)
    return pl.pallas_call(
        _dft1_kernel,
        grid=(nb, n2 // rt, c // tc),
        in_specs=[pl.BlockSpec(g1.shape, lambda b, i, j: (0, 0)),
                  pl.BlockSpec((None, kh, rt, tc), lambda b, i, j: (b, 0, i, j))],
        out_specs=pl.BlockSpec((None, n1, rt, tc), lambda b, i, j: (b, 0, i, j)),
        out_shape=jax.ShapeDtypeStruct((nb, n1, n2, c), jnp.uint32),
        compiler_params=_cparams("parallel", "parallel", "arbitrary"),
        name="dft_stage1",
    )(g1, u4)


def _unpack_complex(ref):
    return pltpu.bitcast(ref[...], BF16)


def _spectrum_kernel(af_ref, ab_ref, h_ref, sum_ref, o_ref, *, d_cols):
    n2 = DFT_N2
    hm = h_ref[...]
    xf = _dot(hm, _unpack_complex(af_ref))
    xb = _dot(hm, _unpack_complex(ab_ref))
    inv = 1.0 / (sum_ref[:, 0:d_cols] + sum_ref[:, d_cols:2 * d_cols])
    o_ref[0] = (xf[:n2] + xb[:n2]) * inv
    o_ref[1] = (xf[n2:] - xb[n2:]) * inv


def filter_spectrum(a4, hmat, sums, d):
    n1 = a4.shape[1]
    n2 = DFT_N2
    return pl.pallas_call(
        functools.partial(_spectrum_kernel, d_cols=d),
        grid=(n1,),
        in_specs=[pl.BlockSpec((None, None, n2, d), lambda k: (0, k, 0, 0)),
                  pl.BlockSpec((None, None, n2, d), lambda k: (0, k, 0, 1)),
                  pl.BlockSpec((None, 2 * n2, 2 * n2), lambda k: (k, 0, 0)),
                  pl.BlockSpec((1, 2 * d), lambda k: (0, 0))],
        out_specs=pl.BlockSpec((2, None, n2, d), lambda k: (0, k, 0, 0)),
        out_shape=jax.ShapeDtypeStruct((2, n1, n2, d), F32),
        compiler_params=_cparams("arbitrary"),
        name="filter_spectrum",
    )(a4, a4, hmat, sums)


def _dftmid_kernel(a_ref, h_ref, g_ref, k_ref, o_ref):
    n2 = DFT_N2
    x = _dot(h_ref[...], _unpack_complex(a_ref))
    xr, xi = x[:n2], x[n2:]
    kr, ki = k_ref[0], k_ref[1]
    y = jnp.concatenate([xr * kr - xi * ki, xr * ki + xi * kr], axis=0).astype(BF16)
    zz = _dot(g_ref[...], y).astype(BF16)
    o_ref[...] = pltpu.bitcast(zz, jnp.uint32)


def dft_mid(a4, hmat, gmat, kspec):
    nb, n1, n2, c = a4.shape
    return pl.pallas_call(
        _dftmid_kernel,
        grid=(n1, nb),
        in_specs=[pl.BlockSpec((None, None, n2, c), lambda k, b: (b, k, 0, 0)),
                  pl.BlockSpec((None, 2 * n2, 2 * n2), lambda k, b: (k, 0, 0)),
                  pl.BlockSpec((None, 2 * n2, 2 * n2), lambda k, b: (k, 0, 0)),
                  pl.BlockSpec((2, None, n2, c), lambda k, b: (0, k, 0, 0))],
        out_specs=pl.BlockSpec((None, None, n2, c), lambda k, b: (b, k, 0, 0)),
        out_shape=jax.ShapeDtypeStruct(a4.shape, jnp.uint32),
        compiler_params=_cparams("parallel", "arbitrary"),
        name="dft_mid",
    )(a4, hmat, gmat, kspec)


def _dftout_kernel(g_ref, z_ref, x0_ref, w_ref, skip_ref, o_ref):
    n1, rt, tc = z_ref.shape
    z = pltpu.bitcast(z_ref[...].reshape(n1 * rt, tc), BF16)
    y = _dot(g_ref[...], z).reshape(o_ref.shape)
    o_ref[...] = x0_ref[...] * (y + w_ref[...] * skip_ref[...])


def dft_out(g2, z4, x0, w, skip, tc=512):
    nb, n1, n2, c = z4.shape
    rt = DFT_ROWS
    kh = g2.shape[0] // rt
    tc = _tile(c, tc)
    tok = pl.BlockSpec((None, kh, rt, tc), lambda b, i, j: (b, 0, i, j))
    return pl.pallas_call(
        _dftout_kernel,
        grid=(nb, n2 // rt, c // tc),
        in_specs=[pl.BlockSpec(g2.shape, lambda b, i, j: (0, 0)),
                  pl.BlockSpec((None, n1, rt, tc), lambda b, i, j: (b, 0, i, j)),
                  tok, tok,
                  pl.BlockSpec((1, 1, tc), lambda b, i, j: (0, 0, j))],
        out_specs=tok,
        out_shape=jax.ShapeDtypeStruct((nb, kh, n2, c), F32),
        compiler_params=_cparams("parallel", "parallel", "arbitrary"),
        name="dft_out",
    )(g2, z4, x0, w, skip.reshape(1, 1, c).astype(F32))


def _rope_tables(seq):
    hd = ATTN_HEAD_DIM
    axis_dim = hd // 2
    t = jnp.arange(seq)
    row = (t // GRID_W).astype(F32)
    col = (t % GRID_W).astype(F32)
    inv_freq = ROPE_THETA ** (-jnp.arange(0, axis_dim, 2, dtype=F32) / axis_dim)
    ang = jnp.concatenate([row[:, None] * inv_freq, col[:, None] * inv_freq], axis=-1)
    c, s = jnp.cos(ang), jnp.sin(ang)
    return jnp.concatenate([c, c], axis=-1), jnp.concatenate([-s, s], axis=-1)


def _hyena_features(seq, d):
    t = jnp.linspace(0.0, 1.0, seq, dtype=F32)
    w = 2.0 * math.pi * jnp.arange(seq, dtype=F32) / seq
    f = jnp.linspace(1e-4, HY_BANDS - 1, HY_BANDS, dtype=F32)
    fw = w[:, None] * f[None, :]
    z = jnp.concatenate([t[:, None], jnp.cos(fw), -jnp.sin(fw)], axis=-1)
    z = jnp.pad(z, ((0, 0), (0, LANES - HY_EMB)))
    deltas = jnp.abs(jnp.linspace(math.log(HY_TARGET) / HY_SLOW_PCT,
                                  math.log(HY_TARGET) / HY_FAST_PCT, d, dtype=F32))
    return z, t[:, None], deltas[None, :]


def _dft_tables(seq):
    n = 2 * seq
    n2 = DFT_N2
    n1 = n // n2

    def cs(phase_int, mod):
        ang = (-2.0 * math.pi / mod) * (phase_int % mod).astype(F32)
        return jnp.cos(ang), jnp.sin(ang)

    k1 = jnp.arange(n1)[:, None]
    m1 = jnp.arange(n1 // 2)[None, :]
    fr, fi = cs(k1 * m1, n1)
    base = jnp.stack([fr, fi], axis=-1)
    eye = jnp.eye(DFT_ROWS, dtype=F32)
    rows = DFT_ROWS
    g1 = jnp.einsum('knp,rs->krpns', base, eye).reshape(n1 * rows * 2, (n1 // 2) * rows).astype(BF16)
    g2 = (jnp.einsum('knp,rs->nrksp', base, eye).reshape((n1 // 2) * rows, n1 * rows * 2) / n).astype(BF16)
    kk = (jnp.arange(n1)[:, None, None] + n1 * jnp.arange(n2)[None, :, None])
    nn = jnp.arange(n2)[None, None, :]
    hr, hi = cs(kk * nn, n)
    hmat = jnp.concatenate([jnp.stack([hr, -hi], axis=-1).reshape(n1, n2, 2 * n2),
                            jnp.stack([hi, hr], axis=-1).reshape(n1, n2, 2 * n2)], axis=1).astype(BF16)
    gr, gi = jnp.swapaxes(hr, 1, 2), -jnp.swapaxes(hi, 1, 2)
    gmat = jnp.stack([jnp.concatenate([gr, -gi], axis=2),
                      jnp.concatenate([gi, gr], axis=2)], axis=2).reshape(n1, 2 * n2, 2 * n2).astype(BF16)
    return g1, g2, hmat, gmat


def _deinterleave(nheads):
    hd = ATTN_HEAD_DIM
    one = np.concatenate([np.arange(0, hd, 2), np.arange(1, hd, 2)])
    return np.concatenate([h * hd + one for h in range(nheads)])


def kernel(x_prompt, x_sample, mem_prompt, mem_sample, norm_mix, norm_xa, norm_mem, norm_ffn, xa_wq, xa_wk, xa_wv, xa_wo, ffn_w_in, ffn_conv_w, ffn_conv_b, ffn_w_out, mix_w_in, mix_w_out, ssd_conv_w, ssd_conv_b, ssd_a_log, ssd_dt_bias, ssd_d, ssd_norm, attn_q_norm, attn_k_norm, hy_w_in, hy_conv_w, hy_conv_b, hy_f_w1, hy_f_b1, hy_f_w2, hy_f_b2, hy_f_w3, hy_f_b3, hy_f_freq, hy_f_w_out, hy_skip, hy_w_out, final_norm):
    nbp, seq, d = x_prompt.shape
    nbs = x_sample.shape[0]
    assert x_sample.shape[1] == seq
    nb = nbp + nbs
    T = nb * seq
    depth = norm_mix.shape[0]
    n_mem = mem_prompt.shape[1]
    d_ff = ffn_w_out.shape[1]

    x = jnp.concatenate([x_prompt, x_sample], axis=0).reshape(T, d)
    mem = jnp.concatenate([mem_prompt, mem_sample], axis=0).reshape(nb * n_mem, d)

    d_ssd = d
    nheads = d_ssd // SSD_HEAD_DIM
    gn = SSD_GROUPS * SSD_STATE
    conv_ch = d_ssd + 2 * gn
    n_att = d // ATTN_HEAD_DIM
    d_kv = ATTN_KV_HEADS * ATTN_HEAD_DIM
    o1 = d_ssd
    o2 = o1 + conv_ch
    o3 = o2 + 2 * nheads
    o4 = o3 + d
    o5 = o4 + d_kv
    qcol = o1
    vcol = o1 + d + d_kv
    cos, sin = _rope_tables(seq)
    perm_q = _deinterleave(n_att)
    perm_k = _deinterleave(ATTN_KV_HEADS)
    perm_h = _deinterleave(1)

    n2 = DFT_N2
    n1 = 2 * seq // n2
    hz, t_col, deltas = _hyena_features(seq, d)
    g1, g2, hmat, gmat = _dft_tables(seq)

    for i in range(depth):
        if i % 2 == 0:
            e = i // 2
            w = mix_w_in[e].astype(BF16)
            w_main = jnp.concatenate([w[:, :o1], w[:, o3:o4][:, perm_q], w[:, o4:o5][:, perm_k], w[:, o5:]],
                                     axis=1)
            w_dt = jnp.pad(w[:, o2:o3], ((0, 0), (0, LANES - 2 * nheads)))
            proj = normmm(x, norm_mix[i], w_main, **({} if e == 0 else dict(tm=512, tn=w_main.shape[1])))
            dtraw = normmm(x, norm_mix[i], w_dt, out_dtype=F32, tn=LANES)
            (xbc,) = normmm_conv(x, norm_mix[i], w[:, o1:o2], ssd_conv_w[e], ssd_conv_b[e],
                                 [0], conv_ch, _epi_silu, 1, seq)
            pad_row = lambda a: jnp.pad(a.reshape(1, -1).astype(F32), ((0, 0), (0, LANES - 2 * nheads)))
            y_ssd = ssd_scan(xbc, dtraw, pad_row(ssd_dt_bias[e]), pad_row(ssd_a_log[e]),
                             jnp.repeat(ssd_d[e].astype(F32), SSD_HEAD_DIM)[None, :],
                             proj, ssd_norm[e].reshape(1, -1).astype(F32), nb, seq)
            scale = ATTN_HEAD_DIM ** -0.5 * math.log2(math.e)
            gains = jnp.concatenate([jnp.tile(attn_q_norm[e][perm_h][None, :] * scale, (n_att, 1)),
                                     jnp.tile(attn_k_norm[e][perm_h][None, :], (ATTN_KV_HEADS, 1))],
                                    axis=0)[:, None, :].astype(F32)
            qk = qk_prep(proj, qcol, n_att, ATTN_KV_HEADS, gains, cos, sin, seq)
            y_att = flash_attention(qk, proj, vcol, n_att, nb, seq)
            x = mm_res(jnp.concatenate([y_ssd, y_att], axis=1), mix_w_out[e].astype(BF16), x,
                       **({} if e == 0 else dict(tm=512, tn=d)))
        else:
            o = i // 2
            x0, wv = normmm_conv(x, norm_mix[i], hy_w_in[o].astype(BF16), hy_conv_w[o], hy_conv_b[o],
                                 [0, d, 2 * d], d, _epi_hyena, 2, seq, out_dtype=F32)
            w1 = jnp.pad(hy_f_w1[o], ((0, LANES - HY_EMB), (0, 0)))
            hfb, sums = hyena_filter(hz, t_col, deltas, w1, hy_f_b1[o], hy_f_w2[o], hy_f_b2[o],
                                     hy_f_w3[o], hy_f_b3[o], hy_f_freq[o], hy_f_w_out[o])
            a_f = dft_stage1(g1, hfb.reshape(1, n1 // 2, n2, 2 * d))
            kspec = filter_spectrum(a_f, hmat, sums, d)
            a_u = dft_stage1(g1, wv.reshape(nb, n1 // 2, n2, d))
            zz = dft_mid(a_u, hmat, gmat, kspec)
            yh = dft_out(g2, zz, x0.reshape(nb, n1 // 2, n2, d), wv.reshape(nb, n1 // 2, n2, d), hy_skip[o])
            x = mm_res(yh.reshape(T, d), hy_w_out[o].astype(BF16), x, tm=512, tn=d)
        q = normmm(x, norm_xa[i], xa_wq[i].astype(BF16), tn=d)
        kv = normmm(mem, norm_mem[i], jnp.concatenate([xa_wk[i].astype(BF16), xa_wv[i].astype(BF16)], axis=1))
        x = mm_res(xattn(q, kv, nb, seq), xa_wo[i].astype(BF16), x, tm=512, tn=d)
        (act,) = normmm_conv(x, norm_ffn[i], ffn_w_in[i].astype(BF16), ffn_conv_w[i], ffn_conv_b[i],
                             [0, d_ff], d_ff, _epi_glu, 1, seq)
        x = mm_res(act, ffn_w_out[i].astype(BF16), x, **({} if i < 2 else dict(tm=512, tn=d)))

    y_prompt = rmsnorm(x, final_norm, 0, nbp * seq).reshape(nbp, seq, d)
    y_sample = rmsnorm(x, final_norm, nbp * seq, nbs * seq).reshape(nbs, seq, d)
    return (y_prompt, y_sample)
```

```python
import functools
import math

import numpy as np
import jax
import jax.numpy as jnp
from jax import lax
from jax.experimental import pallas as pl
from jax.experimental.pallas import tpu as pltpu

F32 = jnp.float32
BF16 = jnp.bfloat16
EPS = 1e-6

GRID_W = 64
XA_HEADS = 4
SSD_HEAD_DIM = 64
SSD_GROUPS = 4
SSD_STATE = 128
SSD_CHUNK = 128
ATTN_HEAD_DIM = 128
ATTN_KV_HEADS = 4
ROPE_THETA = 10000.0
HY_EMB = 33
HY_BANDS = (HY_EMB - 1) // 2
HY_TARGET = 1e-2
HY_FAST_PCT = 0.3
HY_SLOW_PCT = 1.5

LANES = 128
MXU_WIDTH = 256
DFT_N2 = 128
VMEM_LIMIT = 56 * 1024 * 1024
NEG_BIG = -1e30


def _cparams(*sem):
    return pltpu.CompilerParams(dimension_semantics=sem, vmem_limit_bytes=VMEM_LIMIT)


def _tile(dim, pref):
    t = min(dim, pref)
    while dim % t:
        t //= 2
    return t


def _split3(x):
    hi = x.astype(BF16)
    r1 = x - hi.astype(F32)
    mid = r1.astype(BF16)
    lo = (r1 - mid.astype(F32)).astype(BF16)
    return hi, mid, lo


def _dot(a, b):
    return jnp.dot(a, b, preferred_element_type=F32)


def _dot_exact_rhs(x, e):
    hi, mid, lo = _split3(x)
    return _dot(hi, e) + _dot(mid, e) + _dot(lo, e)


def _dot_exact_lhs(e, x):
    hi, mid, lo = _split3(x)
    return _dot(e, hi) + _dot(e, mid) + _dot(e, lo)


def _dot_f32(a, b):
    ah, am, _ = _split3(a)
    bh, bm, _ = _split3(b)
    return _dot(ah, bh) + _dot(ah, bm) + _dot(am, bh)


def _silu(x):
    return x * (1.0 / (1.0 + jnp.exp(-x)))


def _normmm_kernel(x_ref, g_ref, w_ref, o_ref, xn_ref):
    @pl.when(pl.program_id(1) == 0)
    def _():
        x = x_ref[...].astype(F32)
        ms = jnp.mean(x * x, axis=-1, keepdims=True)
        xn_ref[...] = (x * lax.rsqrt(ms + EPS) * g_ref[...]).astype(BF16)

    o_ref[...] = _dot(xn_ref[...], w_ref[...]).astype(o_ref.dtype)


def normmm(x, g, w, out_dtype=BF16, tm=1024, tn=1024):
    M, K = x.shape
    N = w.shape[1]
    tm = _tile(M, tm)
    tn = _tile(N, tn)
    return pl.pallas_call(
        _normmm_kernel,
        grid=(M // tm, N // tn),
        in_specs=[pl.BlockSpec((tm, K), lambda i, j: (i, 0)),
                  pl.BlockSpec((1, K), lambda i, j: (0, 0)),
                  pl.BlockSpec((K, tn), lambda i, j: (0, j),
                               pipeline_mode=pl.Buffered(1) if tn == N else None)],
        out_specs=pl.BlockSpec((tm, tn), lambda i, j: (i, j)),
        out_shape=jax.ShapeDtypeStruct((M, N), out_dtype),
        scratch_shapes=[pltpu.VMEM((tm, K), BF16)],
        compiler_params=_cparams("parallel", "arbitrary"),
        name="normmm",
    )(x, g.reshape(1, K).astype(F32), w)


def _mmres_kernel(a_ref, w_ref, r_ref, o_ref):
    o_ref[...] = r_ref[...] + _dot(a_ref[...].astype(BF16), w_ref[...])


def mm_res(a, w, res, tm=1024, tn=None):
    M, K = a.shape
    N = w.shape[1]
    tm = _tile(M, tm)
    tn = _tile(N, tn or (1024 if K <= 2048 else 512))
    w_mode = pl.Buffered(1) if tn == N else None
    return pl.pallas_call(
        _mmres_kernel,
        grid=(M // tm, N // tn),
        in_specs=[pl.BlockSpec((tm, K), lambda i, j: (i, 0)),
                  pl.BlockSpec((K, tn), lambda i, j: (0, j), pipeline_mode=w_mode),
                  pl.BlockSpec((tm, tn), lambda i, j: (i, j))],
        out_specs=pl.BlockSpec((tm, tn), lambda i, j: (i, j)),
        out_shape=jax.ShapeDtypeStruct((M, N), F32),
        compiler_params=_cparams("parallel", "arbitrary"),
        name="mm_res",
    )(a, w, res)


def _rmsnorm_kernel(x_ref, g_ref, o_ref):
    x = x_ref[...]
    ms = jnp.mean(x * x, axis=-1, keepdims=True)
    o_ref[...] = x * lax.rsqrt(ms + EPS) * g_ref[...]


def rmsnorm(x, g, row0, rows, tm=512):
    K = x.shape[1]
    tm = _tile(math.gcd(row0, rows) if row0 else rows, tm)
    return pl.pallas_call(
        _rmsnorm_kernel,
        grid=(rows // tm,),
        in_specs=[pl.BlockSpec((tm, K), lambda i: (i + row0 // tm, 0)),
                  pl.BlockSpec((1, K), lambda i: (0, 0))],
        out_specs=pl.BlockSpec((tm, K), lambda i: (i, 0)),
        out_shape=jax.ShapeDtypeStruct((rows, K), F32),
        compiler_params=_cparams("parallel"),
        name="final_norm",
    )(x, g.reshape(1, K).astype(F32))


CONV_HALO = 16
CONV_ROWS = 64


def _normmm_conv_kernel(*refs, nseg, width, tm, seq, epilogue, nout, resident_cols):
    xm_ref, xp_ref, xn_ref, g_ref = refs[:4]
    if resident_cols is None:
        segs = [refs[4 + 3 * s:7 + 3 * s] for s in range(nseg)]
        nin = 4 + 3 * nseg
    else:
        segs = [(refs[4],) + tuple(refs[5 + 2 * s:7 + 2 * s]) for s in range(nseg)]
        nin = 5 + 2 * nseg
    outs = refs[nin:nin + nout]
    hn_ref = refs[nin + nout]
    exts = refs[nin + nout + 1:]
    h = CONV_HALO
    half = width // 2
    row0 = pl.program_id(0) * tm
    at_start = (row0 % seq) == 0
    at_end = ((row0 + tm) % seq) == 0

    @pl.when(pl.program_id(1) == 0)
    def _():
        def nrm(x):
            ms = jnp.mean(x * x, axis=-1, keepdims=True)
            return (x * lax.rsqrt(ms + EPS) * g_ref[...]).astype(BF16)
        hn_ref[0:h, :] = nrm(xp_ref[...])
        hn_ref[h:h + tm, :] = nrm(xm_ref[...])
        hn_ref[h + tm:h + tm + h, :] = nrm(xn_ref[...])

    tn = exts[0].shape[1]
    for s, ((w_ref, _, _), ext) in enumerate(zip(segs, exts)):
        if resident_cols is None:
            wt = w_ref[...]
        else:
            c0 = pl.multiple_of(resident_cols[s] + pl.program_id(1) * tn, tn)
            wt = w_ref[:, pl.ds(c0, tn)]
        ext[...] = _dot(hn_ref[...], wt)
        ext[0:h, :] = jnp.where(at_start, 0.0, ext[0:h, :])
        ext[h + tm:h + tm + h, :] = jnp.where(at_end, 0.0, ext[h + tm:h + tm + h, :])

    for rb in range(0, tm, CONV_ROWS):
        for lc in range(0, tn, LANES):
            ls = slice(lc, lc + LANES)
            vals = []
            for (_, cw_ref, cb_ref), ext in zip(segs, exts):
                acc = None
                for k in range(width):
                    term = ext[h - half + k + rb:h - half + k + rb + CONV_ROWS, ls] * cw_ref[k:k + 1, ls]
                    acc = term if acc is None else acc + term
                vals.append(acc + cb_ref[:, ls])
            for o, r in zip(outs, epilogue(*vals)):
                o[rb:rb + CONV_ROWS, ls] = r.astype(o.dtype)


def normmm_conv(x, g, w, conv_w, conv_b, seg_cols, width_cols, epilogue, nout, seq, out_dtype=BF16,
                resident=False, tm=1024, tn=512):
    T, K = x.shape
    width = conv_w.shape[0]
    nseg = len(seg_cols)
    tm = _tile(seq, tm)
    tn = _tile(width_cols, tn)
    assert tm % CONV_ROWS == 0 and tn % LANES == 0
    h = CONV_HALO
    nrb = T // h
    cb = conv_b.reshape(1, -1).astype(F32)
    cw = conv_w.astype(F32)
    in_specs = [pl.BlockSpec((tm, K), lambda i, j: (i, 0)),
                pl.BlockSpec((h, K), lambda i, j: (jnp.maximum(i * (tm // h) - 1, 0), 0)),
                pl.BlockSpec((h, K), lambda i, j: (jnp.minimum((i + 1) * (tm // h), nrb - 1), 0)),
                pl.BlockSpec((1, K), lambda i, j: (0, 0))]
    args = [x, x, x, g.reshape(1, K).astype(F32)]
    if resident:
        in_specs.append(pl.BlockSpec(w.shape, lambda i, j: (0, 0), pipeline_mode=pl.Buffered(1)))
        args.append(w)
    for c0 in seg_cols:
        off = c0 // tn
        if not resident:
            in_specs.append(pl.BlockSpec((K, tn), lambda i, j, off=off: (0, j + off)))
            args.append(w)
        in_specs += [pl.BlockSpec((width, tn), lambda i, j, off=off: (0, j + off)),
                     pl.BlockSpec((1, tn), lambda i, j, off=off: (0, j + off))]
        args += [cw, cb]
    kern = functools.partial(_normmm_conv_kernel, nseg=nseg, width=width, tm=tm, seq=seq,
                             epilogue=epilogue, nout=nout,
                             resident_cols=tuple(seg_cols) if resident else None)
    return pl.pallas_call(
        kern,
        grid=(T // tm, width_cols // tn),
        in_specs=in_specs,
        out_specs=[pl.BlockSpec((tm, tn), lambda i, j: (i, j)) for _ in range(nout)],
        out_shape=[jax.ShapeDtypeStruct((T, width_cols), out_dtype) for _ in range(nout)],
        scratch_shapes=[pltpu.VMEM((tm + 2 * h, K), BF16)]
        + [pltpu.VMEM((tm + 2 * h, tn), F32) for _ in range(nseg)],
        compiler_params=_cparams("parallel", "arbitrary"),
        name="normmm_conv",
    )(*args)


def _epi_silu(c):
    return (_silu(c),)


def _epi_glu(g, up):
    return (_silu(g) * up,)


def _epi_hyena(x0, x1, v):
    return (x0, v * x1)


def _softplus(x):
    return jnp.maximum(x, 0.0) + jnp.log(1.0 + jnp.exp(-jnp.abs(x)))


def _ssd_kernel(*refs, rev, nheads):
    if rev:
        (xs_ref, b_ref, c_ref, dt_ref, bias_ref, alog_ref, e_ref,
         yf_ref, z_ref, gain_ref, o_ref, s_ref, y_ref) = refs
    else:
        (xs_ref, b_ref, c_ref, dt_ref, bias_ref, alog_ref, e_ref,
         dskip_ref, o_ref, s_ref) = refs
        y_ref = o_ref
    Q = SSD_CHUNK
    P = SSD_HEAD_DIM
    hpg = nheads // SSD_GROUPS
    gw = hpg * P
    hoff = nheads if rev else 0

    @pl.when(pl.program_id(1) == 0)
    def _():
        s_ref[...] = jnp.zeros_like(s_ref)

    row = lax.broadcasted_iota(jnp.int32, (Q, Q), 0)
    col = lax.broadcasted_iota(jnp.int32, (Q, Q), 1)
    mask = (col >= row) if rev else (col <= row)
    tri = jnp.where(mask, 1.0, 0.0).astype(BF16)

    dtv = _softplus(dt_ref[...] + bias_ref[...])
    a_row = -jnp.exp(alog_ref[...])
    la = dtv * a_row
    cs = _dot_exact_lhs(tri, la)
    tot = cs[0:1, :] if rev else cs[Q - 1:Q, :]
    cs_t = cs.T
    dt_t = dtv.T
    e = e_ref[...]
    ecs_hi, ecs_lo, _ = _split3(jnp.exp(cs))
    carry_in = _dot(ecs_hi, e) + _dot(ecs_lo, e)
    to_end = _dot((jnp.exp(tot - cs) * dtv).astype(BF16), e)
    dec = _dot_exact_rhs(jnp.broadcast_to(jnp.exp(tot), (8, LANES)), e)[0:1, :]

    xs = xs_ref[...]
    x_state = (xs.astype(F32) * to_end).astype(BF16)
    lane = lax.broadcasted_iota(jnp.int32, (Q, LANES), 1)
    low = lane < P

    for g in range(SSD_GROUPS):
        bg = b_ref[:, g * SSD_STATE:(g + 1) * SSD_STATE]
        cg = c_ref[:, g * SSD_STATE:(g + 1) * SSD_STATE]
        cb = lax.dot_general(cg, bg, (((1,), (1,)), ((), ())), preferred_element_type=F32)
        s_old = s_ref[g]
        y_off = _dot(cg, s_old.astype(BF16)) * carry_in[:, g * gw:(g + 1) * gw]
        s_ref[g] = s_old * dec[:, g * gw:(g + 1) * gw] + lax.dot_general(
            bg, x_state[:, g * gw:(g + 1) * gw], (((0,), (0,)), ((), ())), preferred_element_type=F32)
        for j in range(hpg // 2):
            ws = []
            for hh in range(2):
                hc = hoff + g * hpg + 2 * j + hh
                diff = cs[:, hc:hc + 1] - cs_t[hc:hc + 1, :]
                decay = jnp.exp(jnp.where(mask, diff, NEG_BIG))
                ws.append((cb * decay * dt_t[hc:hc + 1, :]).astype(BF16))
            c0 = g * gw + 2 * j * P
            xp = xs[:, c0:c0 + LANES]
            rhs = jnp.concatenate([jnp.where(low, xp, jnp.zeros_like(xp)),
                                   jnp.where(low, jnp.zeros_like(xp), xp)], axis=0)
            y = _dot(jnp.concatenate(ws, axis=1), rhs) + y_off[:, 2 * j * P:2 * j * P + LANES]
            if not rev:
                y = y + xp.astype(F32) * dskip_ref[:, c0:c0 + LANES]
            y_ref[:, c0:c0 + LANES] = y

    if rev:
        y = y_ref[...] + yf_ref[...]
        gated = y * _silu(z_ref[...].astype(F32))
        ms = jnp.mean(gated * gated, axis=-1, keepdims=True)
        o_ref[...] = (gated * lax.rsqrt(ms + EPS) * gain_ref[...]).astype(o_ref.dtype)


def ssd_scan(xbc, dtraw, bias_row, alog_row, d_row, z_src, gain_row, nb, seq):
    T = xbc.shape[0]
    Q = SSD_CHUNK
    nc = seq // Q
    gn = SSD_GROUPS * SSD_STATE
    hp = xbc.shape[1] - 2 * gn
    nheads = hp // SSD_HEAD_DIM
    hpg = nheads // SSD_GROUPS
    gw = hpg * SSD_HEAD_DIM
    assert hp % gn == 0 and 2 * nheads <= LANES

    def e_mat(off):
        r = np.arange(LANES)[:, None]
        c = np.arange(hp)[None, :]
        return jnp.asarray((r == off + c // SSD_HEAD_DIM).astype(np.float32), dtype=BF16)

    def specs(rev):
        def blk(c):
            return (nc - 1 - c) if rev else c
        return [
            pl.BlockSpec((Q, hp), lambda b, c: (b * nc + blk(c), 0)),
            pl.BlockSpec((Q, gn), lambda b, c: (b * nc + blk(c), hp // gn)),
            pl.BlockSpec((Q, gn), lambda b, c: (b * nc + blk(c), hp // gn + 1)),
            pl.BlockSpec((Q, LANES), lambda b, c: (b * nc + blk(c), 0)),
            pl.BlockSpec((1, LANES), lambda b, c: (0, 0)),
            pl.BlockSpec((1, LANES), lambda b, c: (0, 0)),
            pl.BlockSpec((LANES, hp), lambda b, c: (0, 0)),
        ], (lambda b, c: (b * nc + blk(c), 0))

    in_f, omap_f = specs(False)
    yf = pl.pallas_call(
        functools.partial(_ssd_kernel, rev=False, nheads=nheads),
        grid=(nb, nc),
        in_specs=in_f + [pl.BlockSpec((1, hp), lambda b, c: (0, 0))],
        out_specs=pl.BlockSpec((Q, hp), omap_f),
        out_shape=jax.ShapeDtypeStruct((T, hp), F32),
        scratch_shapes=[pltpu.VMEM((SSD_GROUPS, SSD_STATE, gw), F32)],
        compiler_params=_cparams("parallel", "arbitrary"),
        name="ssd_fwd",
    )(xbc, xbc, xbc, dtraw, bias_row, alog_row, e_mat(0), d_row)
    in_b, omap_b = specs(True)
    return pl.pallas_call(
        functools.partial(_ssd_kernel, rev=True, nheads=nheads),
        grid=(nb, nc),
        in_specs=in_b + [pl.BlockSpec((Q, hp), omap_b),
                         pl.BlockSpec((Q, hp), omap_b),
                         pl.BlockSpec((1, hp), lambda b, c: (0, 0))],
        out_specs=pl.BlockSpec((Q, hp), omap_b),
        out_shape=jax.ShapeDtypeStruct((T, hp), BF16),
        scratch_shapes=[pltpu.VMEM((SSD_GROUPS, SSD_STATE, gw), F32),
                        pltpu.VMEM((Q, hp), F32)],
        compiler_params=_cparams("parallel", "arbitrary"),
        name="ssd_bwd",
    )(xbc, xbc, xbc, dtraw, bias_row, alog_row, e_mat(nheads), yf, z_src, gain_row)


def _qkprep_kernel(q_ref, k_ref, g_ref, cos_ref, sin_ref, o_ref, *, nq, nk):
    hd = ATTN_HEAD_DIM
    cos = cos_ref[...]
    sin = sin_ref[...]
    for h in range(nq + nk):
        src, c0 = (q_ref, h * hd) if h < nq else (k_ref, (h - nq) * hd)
        x = src[:, c0:c0 + hd].astype(F32)
        ms = jnp.mean(x * x, axis=-1, keepdims=True)
        xn = x * lax.rsqrt(ms + EPS) * g_ref[h]
        o_ref[:, h * hd:(h + 1) * hd] = (xn * cos + pltpu.roll(xn, hd // 2, 1) * sin).astype(o_ref.dtype)


def qk_prep(proj, qcol, nq, nk, gains, cos, sin, seq, tq=256):
    T = proj.shape[0]
    hd = ATTN_HEAD_DIM
    tq = _tile(seq, tq)
    spt = seq // tq
    assert qcol % (nq * hd) == 0 and (qcol + nq * hd) % (nk * hd) == 0
    return pl.pallas_call(
        functools.partial(_qkprep_kernel, nq=nq, nk=nk),
        grid=(T // tq,),
        in_specs=[pl.BlockSpec((tq, nq * hd), lambda i: (i, qcol // (nq * hd))),
                  pl.BlockSpec((tq, nk * hd), lambda i: (i, (qcol + nq * hd) // (nk * hd))),
                  pl.BlockSpec((nq + nk, 1, hd), lambda i: (0, 0, 0)),
                  pl.BlockSpec((tq, hd), lambda i: (i % spt, 0)),
                  pl.BlockSpec((tq, hd), lambda i: (i % spt, 0))],
        out_specs=pl.BlockSpec((tq, (nq + nk) * hd), lambda i: (i, 0)),
        out_shape=jax.ShapeDtypeStruct((T, (nq + nk) * hd), BF16),
        compiler_params=_cparams("parallel"),
        name="qk_prep",
    )(proj, proj, gains, cos, sin)


FLASH_ROW_BLOCK = 32


def _flash_kernel(q_ref, k_ref, v_ref, o_ref, qs_ref, va_ref, s0_ref, s1_ref, p_ref, acc_ref,
                  m_ref, al_ref, *, tk, group):
    hd = ATTN_HEAD_DIM
    tq = q_ref.shape[0]
    rows = group * tq
    seq = k_ref.shape[0]
    nk = seq // tk

    @pl.when(pl.program_id(2) == 0)
    def _():
        va_ref[:, 0:hd] = v_ref[...]
        va_ref[:, hd:2 * hd] = jnp.ones((seq, hd), BF16)

    for g in range(group):
        qs_ref[g * tq:(g + 1) * tq, :] = q_ref[:, g * hd:(g + 1) * hd]
    m_ref[...] = jnp.full(m_ref.shape, NEG_BIG, F32)
    acc_ref[...] = jnp.zeros(acc_ref.shape, F32)
    nlc = tk // LANES

    def scores(t, s_ref):
        k0 = pl.multiple_of(t * tk, tk)
        s_ref[...] = lax.dot_general(qs_ref[...], k_ref[pl.ds(k0, tk), :], (((1,), (1,)), ((), ())),
                                     preferred_element_type=F32)

    def update(t, s_ref):
        for r0 in range(0, rows, FLASH_ROW_BLOCK):
            rs = slice(r0, r0 + FLASH_ROW_BLOCK)
            ch = [s_ref[rs, c * LANES:(c + 1) * LANES] for c in range(nlc)]
            mx = ch[0]
            for c in range(1, nlc):
                mx = jnp.maximum(mx, ch[c])
            m_old = m_ref[rs, :]
            m_new = jnp.maximum(m_old, jnp.max(mx, axis=-1, keepdims=True))
            m_ref[rs, :] = m_new
            al_ref[rs, :] = jnp.exp2(m_old - m_new)
            for c in range(nlc):
                p_ref[rs, c * LANES:(c + 1) * LANES] = jnp.exp2(ch[c] - m_new).astype(BF16)
        k0 = pl.multiple_of(t * tk, tk)
        pv = _dot(p_ref[...], va_ref[pl.ds(k0, tk), :])
        al = al_ref[...]
        acc_ref[...] = acc_ref[...] * jnp.concatenate([al, al], axis=1) + pv

    scores(0, s0_ref)

    def body(t2, carry):
        scores(2 * t2 + 1, s1_ref)
        update(2 * t2, s0_ref)
        scores(2 * t2 + 2, s0_ref)
        update(2 * t2 + 1, s1_ref)
        return carry

    lax.fori_loop(0, nk // 2 - 1, body, 0)
    scores(nk - 1, s1_ref)
    update(nk - 2, s0_ref)
    update(nk - 1, s1_ref)
    o = acc_ref[:, 0:hd] / acc_ref[:, hd:2 * hd]
    for g in range(group):
        o_ref[:, g * hd:(g + 1) * hd] = o[g * tq:(g + 1) * tq, :].astype(o_ref.dtype)


def flash_attention(qk, v_src, v_col0, nq_heads, nb, seq, tq=256, tk=1024):
    T = qk.shape[0]
    hd = ATTN_HEAD_DIM
    nkv = ATTN_KV_HEADS
    group = nq_heads // nkv
    tq = _tile(seq, tq)
    tk = _tile(seq // 2, tk)
    nqt = seq // tq
    return pl.pallas_call(
        functools.partial(_flash_kernel, tk=tk, group=group),
        grid=(nb, nkv, nqt),
        in_specs=[pl.BlockSpec((tq, group * hd), lambda b, h, i: (b * nqt + i, h)),
                  pl.BlockSpec((seq, hd), lambda b, h, i: (b, nq_heads + h)),
                  pl.BlockSpec((seq, hd), lambda b, h, i: (b, v_col0 // hd + h))],
        out_specs=pl.BlockSpec((tq, group * hd), lambda b, h, i: (b * nqt + i, h)),
        out_shape=jax.ShapeDtypeStruct((T, nq_heads * hd), BF16),
        scratch_shapes=[pltpu.VMEM((group * tq, hd), BF16),
                        pltpu.VMEM((seq, 2 * hd), BF16),
                        pltpu.VMEM((group * tq, tk), F32),
                        pltpu.VMEM((group * tq, tk), F32),
                        pltpu.VMEM((group * tq, tk), BF16),
                        pltpu.VMEM((group * tq, 2 * hd), F32),
                        pltpu.VMEM((group * tq, LANES), F32),
                        pltpu.VMEM((group * tq, LANES), F32)],
        compiler_params=_cparams("arbitrary", "arbitrary", "arbitrary"),
        name="flash_attn",
    )(qk, qk, v_src)


def _xattn_kernel(q_ref, kv_ref, o_ref, *, heads):
    d = q_ref.shape[1]
    hd = d // heads
    scale = hd ** -0.5
    for h in range(heads):
        q = q_ref[:, h * hd:(h + 1) * hd]
        k = kv_ref[:, h * hd:(h + 1) * hd]
        v = kv_ref[:, d + h * hd:d + (h + 1) * hd]
        s = lax.dot_general(q, k, (((1,), (1,)), ((), ())), preferred_element_type=F32) * scale
        p = jnp.exp(s - jnp.max(s, axis=-1, keepdims=True))
        l = jnp.sum(p, axis=-1, keepdims=True)
        o = _dot(p.astype(BF16), v) / l
        o_ref[:, h * hd:(h + 1) * hd] = o.astype(o_ref.dtype)


def xattn(q, kv, nb, seq, tq=512):
    T, d = q.shape
    n_mem = kv.shape[0] // nb
    tq = _tile(seq, tq)
    nqt = seq // tq
    return pl.pallas_call(
        functools.partial(_xattn_kernel, heads=XA_HEADS),
        grid=(nb, nqt),
        in_specs=[pl.BlockSpec((tq, d), lambda b, i: (b * nqt + i, 0)),
                  pl.BlockSpec((n_mem, 2 * d), lambda b, i: (b, 0))],
        out_specs=pl.BlockSpec((tq, d), lambda b, i: (b * nqt + i, 0)),
        out_shape=jax.ShapeDtypeStruct((T, d), BF16),
        compiler_params=_cparams("parallel", "arbitrary"),
        name="xattn",
    )(q, kv)


def _hyfilter_kernel(z_ref, t_ref, dl_ref, w1_ref, b1_ref, w2_ref, b2_ref, w3_ref, b3_ref,
                     fr_ref, wo_ref, h_ref, sum_ref, *, tl, d):
    i = pl.program_id(0)
    fr = fr_ref[...]
    h = jnp.sin(fr * (_dot_f32(z_ref[...], w1_ref[...]) + b1_ref[...]))
    h = jnp.sin(fr * (_dot_f32(h, w2_ref[...]) + b2_ref[...]))
    h = jnp.sin(fr * (_dot_f32(h, w3_ref[...]) + b3_ref[...]))
    window = jnp.exp(-t_ref[...] * dl_ref[...])
    rows = lax.broadcasted_iota(jnp.int32, (tl, 1), 0) + i * tl

    @pl.when(i == 0)
    def _():
        sum_ref[...] = jnp.zeros_like(sum_ref)

    for part in range(2):
        hp = _dot_f32(h, wo_ref[:, part * d:(part + 1) * d]) * window
        if part == 1:
            hp = jnp.where(rows == 0, 0.0, hp)
        h_ref[:, part * d:(part + 1) * d] = hp.astype(h_ref.dtype)
        sum_ref[:, part * d:(part + 1) * d] += jnp.sum(jnp.abs(hp), axis=0, keepdims=True)


def hyena_filter(z, t_col, deltas, w1, b1, w2, b2, w3, b3, freq, w_out, tl=256):
    L = z.shape[0]
    d2 = w_out.shape[1]
    d = d2 // 2
    fw = w2.shape[0]
    tl = _tile(L, tl)
    full = lambda a: pl.BlockSpec(a.shape, lambda i: (0,) * a.ndim)
    ops = [w1, b1.reshape(1, fw), w2, b2.reshape(1, fw), w3, b3.reshape(1, fw), freq.reshape(1, fw), w_out]
    return pl.pallas_call(
        functools.partial(_hyfilter_kernel, tl=tl, d=d),
        grid=(L // tl,),
        in_specs=[pl.BlockSpec((tl, z.shape[1]), lambda i: (i, 0)),
                  pl.BlockSpec((tl, 1), lambda i: (i, 0)),
                  full(deltas)] + [full(a) for a in ops],
        out_specs=[pl.BlockSpec((tl, d2), lambda i: (i, 0)),
                   pl.BlockSpec((1, d2), lambda i: (0, 0))],
        out_shape=[jax.ShapeDtypeStruct((L, d2), F32),
                   jax.ShapeDtypeStruct((1, d2), F32)],
        compiler_params=_cparams("arbitrary"),
        name="hyena_filter",
    )(z, t_col, deltas, *ops)


DFT_ROWS = 8


def _dft1_kernel(g_ref, u_ref, o_ref):
    kh, rt, tc = u_ref.shape
    u = u_ref[...].reshape(kh * rt, tc).astype(BF16)
    a = _dot(g_ref[...], u).astype(BF16)
    o_ref[...] = pltpu.bitcast(a, jnp.uint32).reshape(o_ref.shape)


def dft_stage1(g1, u4, tc=512):
    nb, kh, n2, c = u4.shape
    rt = DFT_ROWS
    n1 = g1.shape[0] // (2 * rt)
    tc = _tile(c, tc)
    return pl.pallas_call(
        _dft1_kernel,
        grid=(nb, n2 // rt, c // tc),
        in_specs=[pl.BlockSpec(g1.shape, lambda b, i, j: (0, 0)),
                  pl.BlockSpec((None, kh, rt, tc), lambda b, i, j: (b, 0, i, j))],
        out_specs=pl.BlockSpec((None, n1, rt, tc), lambda b, i, j: (b, 0, i, j)),
        out_shape=jax.ShapeDtypeStruct((nb, n1, n2, c), jnp.uint32),
        compiler_params=_cparams("parallel", "parallel", "arbitrary"),
        name="dft_stage1",
    )(g1, u4)


def _unpack_complex(ref):
    return pltpu.bitcast(ref[...], BF16)


def _spectrum_kernel(af_ref, ab_ref, h_ref, sum_ref, o_ref, *, d_cols):
    n2 = DFT_N2
    hm = h_ref[...]
    xf = _dot(hm, _unpack_complex(af_ref))
    xb = _dot(hm, _unpack_complex(ab_ref))
    inv = 1.0 / (sum_ref[:, 0:d_cols] + sum_ref[:, d_cols:2 * d_cols])
    o_ref[0] = (xf[:n2] + xb[:n2]) * inv
    o_ref[1] = (xf[n2:] - xb[n2:]) * inv


def filter_spectrum(a4, hmat, sums, d):
    n1 = a4.shape[1]
    n2 = DFT_N2
    return pl.pallas_call(
        functools.partial(_spectrum_kernel, d_cols=d),
        grid=(n1,),
        in_specs=[pl.BlockSpec((None, None, n2, d), lambda k: (0, k, 0, 0)),
                  pl.BlockSpec((None, None, n2, d), lambda k: (0, k, 0, 1)),
                  pl.BlockSpec((None, 2 * n2, 2 * n2), lambda k: (k, 0, 0)),
                  pl.BlockSpec((1, 2 * d), lambda k: (0, 0))],
        out_specs=pl.BlockSpec((2, None, n2, d), lambda k: (0, k, 0, 0)),
        out_shape=jax.ShapeDtypeStruct((2, n1, n2, d), F32),
        compiler_params=_cparams("arbitrary"),
        name="filter_spectrum",
    )(a4, a4, hmat, sums)


def _dftmid_kernel(a_ref, h_ref, g_ref, k_ref, o_ref):
    n2 = DFT_N2
    x = _dot(h_ref[...], _unpack_complex(a_ref))
    xr, xi = x[:n2], x[n2:]
    kr, ki = k_ref[0], k_ref[1]
    y = jnp.concatenate([xr * kr - xi * ki, xr * ki + xi * kr], axis=0).astype(BF16)
    zz = _dot(g_ref[...], y).astype(BF16)
    o_ref[...] = pltpu.bitcast(zz, jnp.uint32)


def dft_mid(a4, hmat, gmat, kspec):
    nb, n1, n2, c = a4.shape
    return pl.pallas_call(
        _dftmid_kernel,
        grid=(n1, nb),
        in_specs=[pl.BlockSpec((None, None, n2, c), lambda k, b: (b, k, 0, 0)),
                  pl.BlockSpec((None, 2 * n2, 2 * n2), lambda k, b: (k, 0, 0)),
                  pl.BlockSpec((None, 2 * n2, 2 * n2), lambda k, b: (k, 0, 0)),
                  pl.BlockSpec((2, None, n2, c), lambda k, b: (0, k, 0, 0))],
        out_specs=pl.BlockSpec((None, None, n2, c), lambda k, b: (b, k, 0, 0)),
        out_shape=jax.ShapeDtypeStruct(a4.shape, jnp.uint32),
        compiler_params=_cparams("parallel", "arbitrary"),
        name="dft_mid",
    )(a4, hmat, gmat, kspec)


def _dftout_kernel(g_ref, z_ref, x0_ref, w_ref, skip_ref, o_ref):
    n1, rt, tc = z_ref.shape
    z = pltpu.bitcast(z_ref[...].reshape(n1 * rt, tc), BF16)
    y = _dot(g_ref[...], z).reshape(o_ref.shape)
    o_ref[...] = x0_ref[...] * (y + w_ref[...] * skip_ref[...])


def dft_out(g2, z4, x0, w, skip, tc=512):
    nb, n1, n2, c = z4.shape
    rt = DFT_ROWS
    kh = g2.shape[0] // rt
    tc = _tile(c, tc)
    tok = pl.BlockSpec((None, kh, rt, tc), lambda b, i, j: (b, 0, i, j))
    return pl.pallas_call(
        _dftout_kernel,
        grid=(nb, n2 // rt, c // tc),
        in_specs=[pl.BlockSpec(g2.shape, lambda b, i, j: (0, 0)),
                  pl.BlockSpec((None, n1, rt, tc), lambda b, i, j: (b, 0, i, j)),
                  tok, tok,
                  pl.BlockSpec((1, 1, tc), lambda b, i, j: (0, 0, j))],
        out_specs=tok,
        out_shape=jax.ShapeDtypeStruct((nb, kh, n2, c), F32),
        compiler_params=_cparams("parallel", "parallel", "arbitrary"),
        name="dft_out",
    )(g2, z4, x0, w, skip.reshape(1, 1, c).astype(F32))


def _rope_tables(seq):
    hd = ATTN_HEAD_DIM
    axis_dim = hd // 2
    t = jnp.arange(seq)
    row = (t // GRID_W).astype(F32)
    col = (t % GRID_W).astype(F32)
    inv_freq = ROPE_THETA ** (-jnp.arange(0, axis_dim, 2, dtype=F32) / axis_dim)
    ang = jnp.concatenate([row[:, None] * inv_freq, col[:, None] * inv_freq], axis=-1)
    c, s = jnp.cos(ang), jnp.sin(ang)
    return jnp.concatenate([c, c], axis=-1), jnp.concatenate([-s, s], axis=-1)


def _hyena_features(seq, d):
    t = jnp.linspace(0.0, 1.0, seq, dtype=F32)
    w = 2.0 * math.pi * jnp.arange(seq, dtype=F32) / seq
    f = jnp.linspace(1e-4, HY_BANDS - 1, HY_BANDS, dtype=F32)
    fw = w[:, None] * f[None, :]
    z = jnp.concatenate([t[:, None], jnp.cos(fw), -jnp.sin(fw)], axis=-1)
    z = jnp.pad(z, ((0, 0), (0, LANES - HY_EMB)))
    deltas = jnp.abs(jnp.linspace(math.log(HY_TARGET) / HY_SLOW_PCT,
                                  math.log(HY_TARGET) / HY_FAST_PCT, d, dtype=F32))
    return z, t[:, None], deltas[None, :]


def _dft_tables(seq):
    n = 2 * seq
    n2 = DFT_N2
    n1 = n // n2

    def cs(phase_int, mod):
        ang = (-2.0 * math.pi / mod) * (phase_int % mod).astype(F32)
        return jnp.cos(ang), jnp.sin(ang)

    k1 = jnp.arange(n1)[:, None]
    m1 = jnp.arange(n1 // 2)[None, :]
    fr, fi = cs(k1 * m1, n1)
    base = jnp.stack([fr, fi], axis=-1)
    eye = jnp.eye(DFT_ROWS, dtype=F32)
    rows = DFT_ROWS
    g1 = jnp.einsum('knp,rs->krpns', base, eye).reshape(n1 * rows * 2, (n1 // 2) * rows).astype(BF16)
    g2 = (jnp.einsum('knp,rs->nrksp', base, eye).reshape((n1 // 2) * rows, n1 * rows * 2) / n).astype(BF16)
    kk = (jnp.arange(n1)[:, None, None] + n1 * jnp.arange(n2)[None, :, None])
    nn = jnp.arange(n2)[None, None, :]
    hr, hi = cs(kk * nn, n)
    hmat = jnp.concatenate([jnp.stack([hr, -hi], axis=-1).reshape(n1, n2, 2 * n2),
                            jnp.stack([hi, hr], axis=-1).reshape(n1, n2, 2 * n2)], axis=1).astype(BF16)
    gr, gi = jnp.swapaxes(hr, 1, 2), -jnp.swapaxes(hi, 1, 2)
    gmat = jnp.stack([jnp.concatenate([gr, -gi], axis=2),
                      jnp.concatenate([gi, gr], axis=2)], axis=2).reshape(n1, 2 * n2, 2 * n2).astype(BF16)
    return g1, g2, hmat, gmat


def _deinterleave(nheads):
    hd = ATTN_HEAD_DIM
    one = np.concatenate([np.arange(0, hd, 2), np.arange(1, hd, 2)])
    return np.concatenate([h * hd + one for h in range(nheads)])


def kernel(x_prompt, x_sample, mem_prompt, mem_sample, norm_mix, norm_xa, norm_mem, norm_ffn, xa_wq, xa_wk, xa_wv, xa_wo, ffn_w_in, ffn_conv_w, ffn_conv_b, ffn_w_out, mix_w_in, mix_w_out, ssd_conv_w, ssd_conv_b, ssd_a_log, ssd_dt_bias, ssd_d, ssd_norm, attn_q_norm, attn_k_norm, hy_w_in, hy_conv_w, hy_conv_b, hy_f_w1, hy_f_b1, hy_f_w2, hy_f_b2, hy_f_w3, hy_f_b3, hy_f_freq, hy_f_w_out, hy_skip, hy_w_out, final_norm):
    nbp, seq, d = x_prompt.shape
    nbs = x_sample.shape[0]
    assert x_sample.shape[1] == seq
    nb = nbp + nbs
    T = nb * seq
    depth = norm_mix.shape[0]
    n_mem = mem_prompt.shape[1]
    d_ff = ffn_w_out.shape[1]

    x = jnp.concatenate([x_prompt, x_sample], axis=0).reshape(T, d)
    mem = jnp.concatenate([mem_prompt, mem_sample], axis=0).reshape(nb * n_mem, d)

    d_ssd = d
    nheads = d_ssd // SSD_HEAD_DIM
    gn = SSD_GROUPS * SSD_STATE
    conv_ch = d_ssd + 2 * gn
    n_att = d // ATTN_HEAD_DIM
    d_kv = ATTN_KV_HEADS * ATTN_HEAD_DIM
    o1 = d_ssd
    o2 = o1 + conv_ch
    o3 = o2 + 2 * nheads
    o4 = o3 + d
    o5 = o4 + d_kv
    qcol = o1
    vcol = o1 + d + d_kv
    cos, sin = _rope_tables(seq)
    perm_q = _deinterleave(n_att)
    perm_k = _deinterleave(ATTN_KV_HEADS)
    perm_h = _deinterleave(1)

    n2 = DFT_N2
    n1 = 2 * seq // n2
    hz, t_col, deltas = _hyena_features(seq, d)
    g1, g2, hmat, gmat = _dft_tables(seq)

    for i in range(depth):
        if i % 2 == 0:
            e = i // 2
            w = mix_w_in[e].astype(BF16)
            w_main = jnp.concatenate([w[:, :o1], w[:, o3:o4][:, perm_q], w[:, o4:o5][:, perm_k], w[:, o5:]],
                                     axis=1)
            w_dt = jnp.pad(w[:, o2:o3], ((0, 0), (0, LANES - 2 * nheads)))
            proj = normmm(x, norm_mix[i], w_main, tm=512, tn=w_main.shape[1])
            dtraw = normmm(x, norm_mix[i], w_dt, out_dtype=F32, tn=LANES)
            (xbc,) = normmm_conv(x, norm_mix[i], w[:, o1:o2], ssd_conv_w[e], ssd_conv_b[e],
                                 [0], conv_ch, _epi_silu, 1, seq, resident=True)
            pad_row = lambda a: jnp.pad(a.reshape(1, -1).astype(F32), ((0, 0), (0, LANES - 2 * nheads)))
            y_ssd = ssd_scan(xbc, dtraw, pad_row(ssd_dt_bias[e]), pad_row(ssd_a_log[e]),
                             jnp.repeat(ssd_d[e].astype(F32), SSD_HEAD_DIM)[None, :],
                             proj, ssd_norm[e].reshape(1, -1).astype(F32), nb, seq)
            scale = ATTN_HEAD_DIM ** -0.5 * math.log2(math.e)
            gains = jnp.concatenate([jnp.tile(attn_q_norm[e][perm_h][None, :] * scale, (n_att, 1)),
                                     jnp.tile(attn_k_norm[e][perm_h][None, :], (ATTN_KV_HEADS, 1))],
                                    axis=0)[:, None, :].astype(F32)
            qk = qk_prep(proj, qcol, n_att, ATTN_KV_HEADS, gains, cos, sin, seq)
            y_att = flash_attention(qk, proj, vcol, n_att, nb, seq)
            x = mm_res(jnp.concatenate([y_ssd, y_att], axis=1), mix_w_out[e].astype(BF16), x, tm=512, tn=d)
        else:
            o = i // 2
            x0, wv = normmm_conv(x, norm_mix[i], hy_w_in[o].astype(BF16), hy_conv_w[o], hy_conv_b[o],
                                 [0, d, 2 * d], d, _epi_hyena, 2, seq, out_dtype=F32,
                                 **({} if o == 0 else dict(resident=True, tm=512)))
            w1 = jnp.pad(hy_f_w1[o], ((0, LANES - HY_EMB), (0, 0)))
            hfb, sums = hyena_filter(hz, t_col, deltas, w1, hy_f_b1[o], hy_f_w2[o], hy_f_b2[o],
                                     hy_f_w3[o], hy_f_b3[o], hy_f_freq[o], hy_f_w_out[o])
            a_f = dft_stage1(g1, hfb.reshape(1, n1 // 2, n2, 2 * d))
            kspec = filter_spectrum(a_f, hmat, sums, d)
            a_u = dft_stage1(g1, wv.reshape(nb, n1 // 2, n2, d))
            zz = dft_mid(a_u, hmat, gmat, kspec)
            yh = dft_out(g2, zz, x0.reshape(nb, n1 // 2, n2, d), wv.reshape(nb, n1 // 2, n2, d), hy_skip[o])
            x = mm_res(yh.reshape(T, d), hy_w_out[o].astype(BF16), x, tm=512, tn=d)
        q = normmm(x, norm_xa[i], xa_wq[i].astype(BF16), tn=d)
        kv = normmm(mem, norm_mem[i], jnp.concatenate([xa_wk[i].astype(BF16), xa_wv[i].astype(BF16)], axis=1))
        x = mm_res(xattn(q, kv, nb, seq), xa_wo[i].astype(BF16), x, tm=512, tn=d)
        (act,) = normmm_conv(x, norm_ffn[i], ffn_w_in[i].astype(BF16), ffn_conv_w[i], ffn_conv_b[i],
                             [0, d_ff], d_ff, _epi_glu, 1, seq)
        x = mm_res(act, ffn_w_out[i].astype(BF16), x, tm=512, tn=d)

    y_prompt = rmsnorm(x, final_norm, 0, nbp * seq).reshape(nbp, seq, d)
    y_sample = rmsnorm(x, final_norm, nbp * seq, nbs * seq).reshape(nbs, seq, d)
    return (y_prompt, y_sample)
```

```python
import functools
import math

import numpy as np
import jax
import jax.numpy as jnp
from jax import lax
from jax.experimental import pallas as pl
from jax.experimental.pallas import tpu as pltpu

F32 = jnp.float32
BF16 = jnp.bfloat16
EPS = 1e-6

GRID_W = 64
XA_HEADS = 4
SSD_HEAD_DIM = 64
SSD_GROUPS = 4
SSD_STATE = 128
SSD_CHUNK = 128
ATTN_HEAD_DIM = 128
ATTN_KV_HEADS = 4
ROPE_THETA = 10000.0
HY_EMB = 33
HY_BANDS = (HY_EMB - 1) // 2
HY_TARGET = 1e-2
HY_FAST_PCT = 0.3
HY_SLOW_PCT = 1.5

LANES = 128
MXU_WIDTH = 256
DFT_N2 = 128
VMEM_LIMIT = 56 * 1024 * 1024
NEG_BIG = -1e30


def _cparams(*sem):
    return pltpu.CompilerParams(dimension_semantics=sem, vmem_limit_bytes=VMEM_LIMIT)


def _tile(dim, pref):
    t = min(dim, pref)
    while dim % t:
        t //= 2
    return t


def _split3(x):
    hi = x.astype(BF16)
    r1 = x - hi.astype(F32)
    mid = r1.astype(BF16)
    lo = (r1 - mid.astype(F32)).astype(BF16)
    return hi, mid, lo


def _dot(a, b):
    return jnp.dot(a, b, preferred_element_type=F32)


def _dot_exact_rhs(x, e):
    hi, mid, lo = _split3(x)
    return _dot(hi, e) + _dot(mid, e) + _dot(lo, e)


def _dot_exact_lhs(e, x):
    hi, mid, lo = _split3(x)
    return _dot(e, hi) + _dot(e, mid) + _dot(e, lo)


def _dot_f32(a, b):
    ah, am, _ = _split3(a)
    bh, bm, _ = _split3(b)
    return _dot(ah, bh) + _dot(ah, bm) + _dot(am, bh)


def _silu(x):
    return x * (1.0 / (1.0 + jnp.exp(-x)))


def _normmm_kernel(x_ref, g_ref, w_ref, o_ref, xn_ref):
    @pl.when(pl.program_id(1) == 0)
    def _():
        x = x_ref[...].astype(F32)
        ms = jnp.mean(x * x, axis=-1, keepdims=True)
        xn_ref[...] = (x * lax.rsqrt(ms + EPS) * g_ref[...]).astype(BF16)

    o_ref[...] = _dot(xn_ref[...], w_ref[...]).astype(o_ref.dtype)


def normmm(x, g, w, out_dtype=BF16, tm=1024, tn=1024):
    M, K = x.shape
    N = w.shape[1]
    tm = _tile(M, tm)
    tn = _tile(N, tn)
    return pl.pallas_call(
        _normmm_kernel,
        grid=(M // tm, N // tn),
        in_specs=[pl.BlockSpec((tm, K), lambda i, j: (i, 0)),
                  pl.BlockSpec((1, K), lambda i, j: (0, 0)),
                  pl.BlockSpec((K, tn), lambda i, j: (0, j),
                               pipeline_mode=pl.Buffered(1) if tn == N else None)],
        out_specs=pl.BlockSpec((tm, tn), lambda i, j: (i, j)),
        out_shape=jax.ShapeDtypeStruct((M, N), out_dtype),
        scratch_shapes=[pltpu.VMEM((tm, K), BF16)],
        compiler_params=_cparams("parallel", "arbitrary"),
        name="normmm",
    )(x, g.reshape(1, K).astype(F32), w)


def _mmres_kernel(a_ref, w_ref, r_ref, o_ref):
    o_ref[...] = r_ref[...] + _dot(a_ref[...].astype(BF16), w_ref[...])


def mm_res(a, w, res, tm=1024, tn=None):
    M, K = a.shape
    N = w.shape[1]
    tm = _tile(M, tm)
    tn = _tile(N, tn or (1024 if K <= 2048 else 512))
    w_mode = pl.Buffered(1) if tn == N else None
    return pl.pallas_call(
        _mmres_kernel,
        grid=(M // tm, N // tn),
        in_specs=[pl.BlockSpec((tm, K), lambda i, j: (i, 0)),
                  pl.BlockSpec((K, tn), lambda i, j: (0, j), pipeline_mode=w_mode),
                  pl.BlockSpec((tm, tn), lambda i, j: (i, j))],
        out_specs=pl.BlockSpec((tm, tn), lambda i, j: (i, j)),
        out_shape=jax.ShapeDtypeStruct((M, N), F32),
        compiler_params=_cparams("parallel", "arbitrary"),
        name="mm_res",
    )(a, w, res)


def _rmsnorm_kernel(x_ref, g_ref, o_ref):
    x = x_ref[...]
    ms = jnp.mean(x * x, axis=-1, keepdims=True)
    o_ref[...] = x * lax.rsqrt(ms + EPS) * g_ref[...]


def rmsnorm(x, g, row0, rows, tm=512):
    K = x.shape[1]
    tm = _tile(math.gcd(row0, rows) if row0 else rows, tm)
    return pl.pallas_call(
        _rmsnorm_kernel,
        grid=(rows // tm,),
        in_specs=[pl.BlockSpec((tm, K), lambda i: (i + row0 // tm, 0)),
                  pl.BlockSpec((1, K), lambda i: (0, 0))],
        out_specs=pl.BlockSpec((tm, K), lambda i: (i, 0)),
        out_shape=jax.ShapeDtypeStruct((rows, K), F32),
        compiler_params=_cparams("parallel"),
        name="final_norm",
    )(x, g.reshape(1, K).astype(F32))


CONV_HALO = 16
CONV_ROWS = 64


def _normmm_conv_kernel(*refs, nseg, width, tm, seq, epilogue, nout, CONV_ROWS):
    xm_ref, xp_ref, xn_ref, g_ref = refs[:4]
    segs = [refs[4 + 3 * s:7 + 3 * s] for s in range(nseg)]
    outs = refs[4 + 3 * nseg:4 + 3 * nseg + nout]
    hn_ref = refs[4 + 3 * nseg + nout]
    exts = refs[5 + 3 * nseg + nout:]
    h = CONV_HALO
    half = width // 2
    row0 = pl.program_id(0) * tm
    at_start = (row0 % seq) == 0
    at_end = ((row0 + tm) % seq) == 0

    @pl.when(pl.program_id(1) == 0)
    def _():
        def nrm(x):
            ms = jnp.mean(x * x, axis=-1, keepdims=True)
            return (x * lax.rsqrt(ms + EPS) * g_ref[...]).astype(BF16)
        hn_ref[0:h, :] = nrm(xp_ref[...])
        hn_ref[h:h + tm, :] = nrm(xm_ref[...])
        hn_ref[h + tm:h + tm + h, :] = nrm(xn_ref[...])

    tn = exts[0].shape[1]
    for (w_ref, _, _), ext in zip(segs, exts):
        ext[...] = _dot(hn_ref[...], w_ref[...])
        ext[0:h, :] = jnp.where(at_start, 0.0, ext[0:h, :])
        ext[h + tm:h + tm + h, :] = jnp.where(at_end, 0.0, ext[h + tm:h + tm + h, :])

    for rb in range(0, tm, CONV_ROWS):
        for lc in range(0, tn, LANES):
            ls = slice(lc, lc + LANES)
            vals = []
            for (_, cw_ref, cb_ref), ext in zip(segs, exts):
                acc = None
                for k in range(width):
                    term = ext[h - half + k + rb:h - half + k + rb + CONV_ROWS, ls] * cw_ref[k:k + 1, ls]
                    acc = term if acc is None else acc + term
                vals.append(acc + cb_ref[:, ls])
            for o, r in zip(outs, epilogue(*vals)):
                o[rb:rb + CONV_ROWS, ls] = r.astype(o.dtype)


def normmm_conv(x, g, w, conv_w, conv_b, seg_cols, width_cols, epilogue, nout, seq, out_dtype=BF16,
                tm=1024, tn=512, conv_rows=CONV_ROWS):
    T, K = x.shape
    width = conv_w.shape[0]
    nseg = len(seg_cols)
    tm = _tile(seq, tm)
    tn = _tile(width_cols, tn)
    assert tm % CONV_ROWS == 0 and tn % LANES == 0
    h = CONV_HALO
    nrb = T // h
    cb = conv_b.reshape(1, -1).astype(F32)
    cw = conv_w.astype(F32)
    in_specs = [pl.BlockSpec((tm, K), lambda i, j: (i, 0)),
                pl.BlockSpec((h, K), lambda i, j: (jnp.maximum(i * (tm // h) - 1, 0), 0)),
                pl.BlockSpec((h, K), lambda i, j: (jnp.minimum((i + 1) * (tm // h), nrb - 1), 0)),
                pl.BlockSpec((1, K), lambda i, j: (0, 0))]
    args = [x, x, x, g.reshape(1, K).astype(F32)]
    for c0 in seg_cols:
        off = c0 // tn
        in_specs += [pl.BlockSpec((K, tn), lambda i, j, off=off: (0, j + off)),
                     pl.BlockSpec((width, tn), lambda i, j, off=off: (0, j + off)),
                     pl.BlockSpec((1, tn), lambda i, j, off=off: (0, j + off))]
        args += [w, cw, cb]
    kern = functools.partial(_normmm_conv_kernel, nseg=nseg, width=width, tm=tm, seq=seq,
                             epilogue=epilogue, nout=nout, CONV_ROWS=conv_rows)
    return pl.pallas_call(
        kern,
        grid=(T // tm, width_cols // tn),
        in_specs=in_specs,
        out_specs=[pl.BlockSpec((tm, tn), lambda i, j: (i, j)) for _ in range(nout)],
        out_shape=[jax.ShapeDtypeStruct((T, width_cols), out_dtype) for _ in range(nout)],
        scratch_shapes=[pltpu.VMEM((tm + 2 * h, K), BF16)]
        + [pltpu.VMEM((tm + 2 * h, tn), F32) for _ in range(nseg)],
        compiler_params=_cparams("parallel", "arbitrary"),
        name="normmm_conv",
    )(*args)


def _epi_silu(c):
    return (_silu(c),)


def _epi_glu(g, up):
    return (_silu(g) * up,)


def _epi_hyena(x0, x1, v):
    return (x0, v * x1)


def _softplus(x):
    return jnp.maximum(x, 0.0) + jnp.log(1.0 + jnp.exp(-jnp.abs(x)))


def _ssd_kernel(*refs, rev, nheads):
    if rev:
        (xs_ref, b_ref, c_ref, dt_ref, bias_ref, alog_ref, e_ref,
         yf_ref, z_ref, gain_ref, o_ref, s_ref, y_ref) = refs
    else:
        (xs_ref, b_ref, c_ref, dt_ref, bias_ref, alog_ref, e_ref,
         dskip_ref, o_ref, s_ref) = refs
        y_ref = o_ref
    Q = SSD_CHUNK
    P = SSD_HEAD_DIM
    hpg = nheads // SSD_GROUPS
    gw = hpg * P
    hoff = nheads if rev else 0

    @pl.when(pl.program_id(1) == 0)
    def _():
        s_ref[...] = jnp.zeros_like(s_ref)

    row = lax.broadcasted_iota(jnp.int32, (Q, Q), 0)
    col = lax.broadcasted_iota(jnp.int32, (Q, Q), 1)
    mask = (col >= row) if rev else (col <= row)
    tri = jnp.where(mask, 1.0, 0.0).astype(BF16)

    dtv = _softplus(dt_ref[...] + bias_ref[...])
    a_row = -jnp.exp(alog_ref[...])
    la = dtv * a_row
    cs = _dot_exact_lhs(tri, la)
    tot = cs[0:1, :] if rev else cs[Q - 1:Q, :]
    cs_t = cs.T
    dt_t = dtv.T
    e = e_ref[...]
    ecs_hi, ecs_lo, _ = _split3(jnp.exp(cs))
    carry_in = _dot(ecs_hi, e) + _dot(ecs_lo, e)
    to_end = _dot((jnp.exp(tot - cs) * dtv).astype(BF16), e)
    dec = _dot_exact_rhs(jnp.broadcast_to(jnp.exp(tot), (8, LANES)), e)[0:1, :]

    xs = xs_ref[...]
    x_state = (xs.astype(F32) * to_end).astype(BF16)
    lane = lax.broadcasted_iota(jnp.int32, (Q, LANES), 1)
    low = lane < P

    for g in range(SSD_GROUPS):
        bg = b_ref[:, g * SSD_STATE:(g + 1) * SSD_STATE]
        cg = c_ref[:, g * SSD_STATE:(g + 1) * SSD_STATE]
        cb = lax.dot_general(cg, bg, (((1,), (1,)), ((), ())), preferred_element_type=F32)
        s_old = s_ref[g]
        y_off = _dot(cg, s_old.astype(BF16)) * carry_in[:, g * gw:(g + 1) * gw]
        s_ref[g] = s_old * dec[:, g * gw:(g + 1) * gw] + lax.dot_general(
            bg, x_state[:, g * gw:(g + 1) * gw], (((0,), (0,)), ((), ())), preferred_element_type=F32)
        for j in range(hpg // 2):
            ws = []
            for hh in range(2):
                hc = hoff + g * hpg + 2 * j + hh
                diff = cs[:, hc:hc + 1] - cs_t[hc:hc + 1, :]
                decay = jnp.exp(jnp.where(mask, diff, NEG_BIG))
                ws.append((cb * decay * dt_t[hc:hc + 1, :]).astype(BF16))
            c0 = g * gw + 2 * j * P
            xp = xs[:, c0:c0 + LANES]
            rhs = jnp.concatenate([jnp.where(low, xp, jnp.zeros_like(xp)),
                                   jnp.where(low, jnp.zeros_like(xp), xp)], axis=0)
            y = _dot(jnp.concatenate(ws, axis=1), rhs) + y_off[:, 2 * j * P:2 * j * P + LANES]
            if not rev:
                y = y + xp.astype(F32) * dskip_ref[:, c0:c0 + LANES]
            y_ref[:, c0:c0 + LANES] = y

    if rev:
        y = y_ref[...] + yf_ref[...]
        gated = y * _silu(z_ref[...].astype(F32))
        ms = jnp.mean(gated * gated, axis=-1, keepdims=True)
        o_ref[...] = (gated * lax.rsqrt(ms + EPS) * gain_ref[...]).astype(o_ref.dtype)


def ssd_scan(xbc, dtraw, bias_row, alog_row, d_row, z_src, gain_row, nb, seq):
    T = xbc.shape[0]
    Q = SSD_CHUNK
    nc = seq // Q
    gn = SSD_GROUPS * SSD_STATE
    hp = xbc.shape[1] - 2 * gn
    nheads = hp // SSD_HEAD_DIM
    hpg = nheads // SSD_GROUPS
    gw = hpg * SSD_HEAD_DIM
    assert hp % gn == 0 and 2 * nheads <= LANES

    def e_mat(off):
        r = np.arange(LANES)[:, None]
        c = np.arange(hp)[None, :]
        return jnp.asarray((r == off + c // SSD_HEAD_DIM).astype(np.float32), dtype=BF16)

    def specs(rev):
        def blk(c):
            return (nc - 1 - c) if rev else c
        return [
            pl.BlockSpec((Q, hp), lambda b, c: (b * nc + blk(c), 0)),
            pl.BlockSpec((Q, gn), lambda b, c: (b * nc + blk(c), hp // gn)),
            pl.BlockSpec((Q, gn), lambda b, c: (b * nc + blk(c), hp // gn + 1)),
            pl.BlockSpec((Q, LANES), lambda b, c: (b * nc + blk(c), 0)),
            pl.BlockSpec((1, LANES), lambda b, c: (0, 0)),
            pl.BlockSpec((1, LANES), lambda b, c: (0, 0)),
            pl.BlockSpec((LANES, hp), lambda b, c: (0, 0)),
        ], (lambda b, c: (b * nc + blk(c), 0))

    in_f, omap_f = specs(False)
    yf = pl.pallas_call(
        functools.partial(_ssd_kernel, rev=False, nheads=nheads),
        grid=(nb, nc),
        in_specs=in_f + [pl.BlockSpec((1, hp), lambda b, c: (0, 0))],
        out_specs=pl.BlockSpec((Q, hp), omap_f),
        out_shape=jax.ShapeDtypeStruct((T, hp), F32),
        scratch_shapes=[pltpu.VMEM((SSD_GROUPS, SSD_STATE, gw), F32)],
        compiler_params=_cparams("parallel", "arbitrary"),
        name="ssd_fwd",
    )(xbc, xbc, xbc, dtraw, bias_row, alog_row, e_mat(0), d_row)
    in_b, omap_b = specs(True)
    return pl.pallas_call(
        functools.partial(_ssd_kernel, rev=True, nheads=nheads),
        grid=(nb, nc),
        in_specs=in_b + [pl.BlockSpec((Q, hp), omap_b),
                         pl.BlockSpec((Q, hp), omap_b),
                         pl.BlockSpec((1, hp), lambda b, c: (0, 0))],
        out_specs=pl.BlockSpec((Q, hp), omap_b),
        out_shape=jax.ShapeDtypeStruct((T, hp), BF16),
        scratch_shapes=[pltpu.VMEM((SSD_GROUPS, SSD_STATE, gw), F32),
                        pltpu.VMEM((Q, hp), F32)],
        compiler_params=_cparams("parallel", "arbitrary"),
        name="ssd_bwd",
    )(xbc, xbc, xbc, dtraw, bias_row, alog_row, e_mat(nheads), yf, z_src, gain_row)


def _qkprep_kernel(q_ref, k_ref, g_ref, cos_ref, sin_ref, o_ref, *, nq, nk):
    hd = ATTN_HEAD_DIM
    cos = cos_ref[...]
    sin = sin_ref[...]
    for h in range(nq + nk):
        src, c0 = (q_ref, h * hd) if h < nq else (k_ref, (h - nq) * hd)
        x = src[:, c0:c0 + hd].astype(F32)
        ms = jnp.mean(x * x, axis=-1, keepdims=True)
        xn = x * lax.rsqrt(ms + EPS) * g_ref[h]
        o_ref[:, h * hd:(h + 1) * hd] = (xn * cos + pltpu.roll(xn, hd // 2, 1) * sin).astype(o_ref.dtype)


def qk_prep(proj, qcol, nq, nk, gains, cos, sin, seq, tq=256):
    T = proj.shape[0]
    hd = ATTN_HEAD_DIM
    tq = _tile(seq, tq)
    spt = seq // tq
    assert qcol % (nq * hd) == 0 and (qcol + nq * hd) % (nk * hd) == 0
    return pl.pallas_call(
        functools.partial(_qkprep_kernel, nq=nq, nk=nk),
        grid=(T // tq,),
        in_specs=[pl.BlockSpec((tq, nq * hd), lambda i: (i, qcol // (nq * hd))),
                  pl.BlockSpec((tq, nk * hd), lambda i: (i, (qcol + nq * hd) // (nk * hd))),
                  pl.BlockSpec((nq + nk, 1, hd), lambda i: (0, 0, 0)),
                  pl.BlockSpec((tq, hd), lambda i: (i % spt, 0)),
                  pl.BlockSpec((tq, hd), lambda i: (i % spt, 0))],
        out_specs=pl.BlockSpec((tq, (nq + nk) * hd), lambda i: (i, 0)),
        out_shape=jax.ShapeDtypeStruct((T, (nq + nk) * hd), BF16),
        compiler_params=_cparams("parallel"),
        name="qk_prep",
    )(proj, proj, gains, cos, sin)


FLASH_ROW_BLOCK = 32


def _flash_kernel(q_ref, k_ref, v_ref, o_ref, qs_ref, va_ref, s0_ref, s1_ref, p_ref, acc_ref,
                  m_ref, al_ref, *, tk, group):
    hd = ATTN_HEAD_DIM
    tq = q_ref.shape[0]
    rows = group * tq
    seq = k_ref.shape[0]
    nk = seq // tk

    @pl.when(pl.program_id(2) == 0)
    def _():
        va_ref[:, 0:hd] = v_ref[...]
        va_ref[:, hd:2 * hd] = jnp.ones((seq, hd), BF16)

    for g in range(group):
        qs_ref[g * tq:(g + 1) * tq, :] = q_ref[:, g * hd:(g + 1) * hd]
    m_ref[...] = jnp.full(m_ref.shape, NEG_BIG, F32)
    acc_ref[...] = jnp.zeros(acc_ref.shape, F32)
    nlc = tk // LANES

    def scores(t, s_ref):
        k0 = pl.multiple_of(t * tk, tk)
        s_ref[...] = lax.dot_general(qs_ref[...], k_ref[pl.ds(k0, tk), :], (((1,), (1,)), ((), ())),
                                     preferred_element_type=F32)

    def update(t, s_ref):
        for r0 in range(0, rows, FLASH_ROW_BLOCK):
            rs = slice(r0, r0 + FLASH_ROW_BLOCK)
            ch = [s_ref[rs, c * LANES:(c + 1) * LANES] for c in range(nlc)]
            mx = ch[0]
            for c in range(1, nlc):
                mx = jnp.maximum(mx, ch[c])
            m_old = m_ref[rs, :]
            m_new = jnp.maximum(m_old, jnp.max(mx, axis=-1, keepdims=True))
            m_ref[rs, :] = m_new
            al_ref[rs, :] = jnp.exp2(m_old - m_new)
            for c in range(nlc):
                p_ref[rs, c * LANES:(c + 1) * LANES] = jnp.exp2(ch[c] - m_new).astype(BF16)
        k0 = pl.multiple_of(t * tk, tk)
        pv = _dot(p_ref[...], va_ref[pl.ds(k0, tk), :])
        al = al_ref[...]
        acc_ref[...] = acc_ref[...] * jnp.concatenate([al, al], axis=1) + pv

    scores(0, s0_ref)

    def body(t2, carry):
        scores(2 * t2 + 1, s1_ref)
        update(2 * t2, s0_ref)
        scores(2 * t2 + 2, s0_ref)
        update(2 * t2 + 1, s1_ref)
        return carry

    lax.fori_loop(0, nk // 2 - 1, body, 0)
    scores(nk - 1, s1_ref)
    update(nk - 2, s0_ref)
    update(nk - 1, s1_ref)
    o = acc_ref[:, 0:hd] / acc_ref[:, hd:2 * hd]
    for g in range(group):
        o_ref[:, g * hd:(g + 1) * hd] = o[g * tq:(g + 1) * tq, :].astype(o_ref.dtype)


def flash_attention(qk, v_src, v_col0, nq_heads, nb, seq, tq=256, tk=1024):
    T = qk.shape[0]
    hd = ATTN_HEAD_DIM
    nkv = ATTN_KV_HEADS
    group = nq_heads // nkv
    tq = _tile(seq, tq)
    tk = _tile(seq // 2, tk)
    nqt = seq // tq
    return pl.pallas_call(
        functools.partial(_flash_kernel, tk=tk, group=group),
        grid=(nb, nkv, nqt),
        in_specs=[pl.BlockSpec((tq, group * hd), lambda b, h, i: (b * nqt + i, h)),
                  pl.BlockSpec((seq, hd), lambda b, h, i: (b, nq_heads + h)),
                  pl.BlockSpec((seq, hd), lambda b, h, i: (b, v_col0 // hd + h))],
        out_specs=pl.BlockSpec((tq, group * hd), lambda b, h, i: (b * nqt + i, h)),
        out_shape=jax.ShapeDtypeStruct((T, nq_heads * hd), BF16),
        scratch_shapes=[pltpu.VMEM((group * tq, hd), BF16),
                        pltpu.VMEM((seq, 2 * hd), BF16),
                        pltpu.VMEM((group * tq, tk), F32),
                        pltpu.VMEM((group * tq, tk), F32),
                        pltpu.VMEM((group * tq, tk), BF16),
                        pltpu.VMEM((group * tq, 2 * hd), F32),
                        pltpu.VMEM((group * tq, LANES), F32),
                        pltpu.VMEM((group * tq, LANES), F32)],
        compiler_params=_cparams("arbitrary", "arbitrary", "arbitrary"),
        name="flash_attn",
    )(qk, qk, v_src)


def _xattn_kernel(q_ref, kv_ref, o_ref, *, heads):
    d = q_ref.shape[1]
    hd = d // heads
    scale = hd ** -0.5
    for h in range(heads):
        q = q_ref[:, h * hd:(h + 1) * hd]
        k = kv_ref[:, h * hd:(h + 1) * hd]
        v = kv_ref[:, d + h * hd:d + (h + 1) * hd]
        s = lax.dot_general(q, k, (((1,), (1,)), ((), ())), preferred_element_type=F32) * scale
        p = jnp.exp(s - jnp.max(s, axis=-1, keepdims=True))
        l = jnp.sum(p, axis=-1, keepdims=True)
        o = _dot(p.astype(BF16), v) / l
        o_ref[:, h * hd:(h + 1) * hd] = o.astype(o_ref.dtype)


def xattn(q, kv, nb, seq, tq=512):
    T, d = q.shape
    n_mem = kv.shape[0] // nb
    tq = _tile(seq, tq)
    nqt = seq // tq
    return pl.pallas_call(
        functools.partial(_xattn_kernel, heads=XA_HEADS),
        grid=(nb, nqt),
        in_specs=[pl.BlockSpec((tq, d), lambda b, i: (b * nqt + i, 0)),
                  pl.BlockSpec((n_mem, 2 * d), lambda b, i: (b, 0))],
        out_specs=pl.BlockSpec((tq, d), lambda b, i: (b * nqt + i, 0)),
        out_shape=jax.ShapeDtypeStruct((T, d), BF16),
        compiler_params=_cparams("parallel", "arbitrary"),
        name="xattn",
    )(q, kv)


def _hyfilter_kernel(z_ref, t_ref, dl_ref, w1_ref, b1_ref, w2_ref, b2_ref, w3_ref, b3_ref,
                     fr_ref, wo_ref, h_ref, sum_ref, *, tl, d):
    i = pl.program_id(0)
    fr = fr_ref[...]
    h = jnp.sin(fr * (_dot_f32(z_ref[...], w1_ref[...]) + b1_ref[...]))
    h = jnp.sin(fr * (_dot_f32(h, w2_ref[...]) + b2_ref[...]))
    h = jnp.sin(fr * (_dot_f32(h, w3_ref[...]) + b3_ref[...]))
    window = jnp.exp(-t_ref[...] * dl_ref[...])
    rows = lax.broadcasted_iota(jnp.int32, (tl, 1), 0) + i * tl

    @pl.when(i == 0)
    def _():
        sum_ref[...] = jnp.zeros_like(sum_ref)

    for part in range(2):
        hp = _dot_f32(h, wo_ref[:, part * d:(part + 1) * d]) * window
        if part == 1:
            hp = jnp.where(rows == 0, 0.0, hp)
        h_ref[:, part * d:(part + 1) * d] = hp.astype(h_ref.dtype)
        sum_ref[:, part * d:(part + 1) * d] += jnp.sum(jnp.abs(hp), axis=0, keepdims=True)


def hyena_filter(z, t_col, deltas, w1, b1, w2, b2, w3, b3, freq, w_out, tl=256):
    L = z.shape[0]
    d2 = w_out.shape[1]
    d = d2 // 2
    fw = w2.shape[0]
    tl = _tile(L, tl)
    full = lambda a: pl.BlockSpec(a.shape, lambda i: (0,) * a.ndim)
    ops = [w1, b1.reshape(1, fw), w2, b2.reshape(1, fw), w3, b3.reshape(1, fw), freq.reshape(1, fw), w_out]
    return pl.pallas_call(
        functools.partial(_hyfilter_kernel, tl=tl, d=d),
        grid=(L // tl,),
        in_specs=[pl.BlockSpec((tl, z.shape[1]), lambda i: (i, 0)),
                  pl.BlockSpec((tl, 1), lambda i: (i, 0)),
                  full(deltas)] + [full(a) for a in ops],
        out_specs=[pl.BlockSpec((tl, d2), lambda i: (i, 0)),
                   pl.BlockSpec((1, d2), lambda i: (0, 0))],
        out_shape=[jax.ShapeDtypeStruct((L, d2), F32),
                   jax.ShapeDtypeStruct((1, d2), F32)],
        compiler_params=_cparams("arbitrary"),
        name="hyena_filter",
    )(z, t_col, deltas, *ops)


DFT_ROWS = 8


def _dft1_kernel(g_ref, u_ref, o_ref):
    kh, rt, tc = u_ref.shape
    u = u_ref[...].reshape(kh * rt, tc).astype(BF16)
    a = _dot(g_ref[...], u).astype(BF16)
    o_ref[...] = pltpu.bitcast(a, jnp.uint32).reshape(o_ref.shape)


def dft_stage1(g1, u4, tc=512):
    nb, kh, n2, c = u4.shape
    rt = DFT_ROWS
    n1 = g1.shape[0] // (2 * rt)
    tc = _tile(c, tc)
    return pl.pallas_call(
        _dft1_kernel,
        grid=(nb, n2 // rt, c // tc),
        in_specs=[pl.BlockSpec(g1.shape, lambda b, i, j: (0, 0)),
                  pl.BlockSpec((None, kh, rt, tc), lambda b, i, j: (b, 0, i, j))],
        out_specs=pl.BlockSpec((None, n1, rt, tc), lambda b, i, j: (b, 0, i, j)),
        out_shape=jax.ShapeDtypeStruct((nb, n1, n2, c), jnp.uint32),
        compiler_params=_cparams("parallel", "parallel", "arbitrary"),
        name="dft_stage1",
    )(g1, u4)


def _unpack_complex(ref):
    return pltpu.bitcast(ref[...], BF16)


def _spectrum_kernel(af_ref, ab_ref, h_ref, sum_ref, o_ref, *, d_cols):
    n2 = DFT_N2
    hm = h_ref[...]
    xf = _dot(hm, _unpack_complex(af_ref))
    xb = _dot(hm, _unpack_complex(ab_ref))
    inv = 1.0 / (sum_ref[:, 0:d_cols] + sum_ref[:, d_cols:2 * d_cols])
    o_ref[0] = (xf[:n2] + xb[:n2]) * inv
    o_ref[1] = (xf[n2:] - xb[n2:]) * inv


def filter_spectrum(a4, hmat, sums, d):
    n1 = a4.shape[1]
    n2 = DFT_N2
    return pl.pallas_call(
        functools.partial(_spectrum_kernel, d_cols=d),
        grid=(n1,),
        in_specs=[pl.BlockSpec((None, None, n2, d), lambda k: (0, k, 0, 0)),
                  pl.BlockSpec((None, None, n2, d), lambda k: (0, k, 0, 1)),
                  pl.BlockSpec((None, 2 * n2, 2 * n2), lambda k: (k, 0, 0)),
                  pl.BlockSpec((1, 2 * d), lambda k: (0, 0))],
        out_specs=pl.BlockSpec((2, None, n2, d), lambda k: (0, k, 0, 0)),
        out_shape=jax.ShapeDtypeStruct((2, n1, n2, d), F32),
        compiler_params=_cparams("arbitrary"),
        name="filter_spectrum",
    )(a4, a4, hmat, sums)


def _dftmid_kernel(a_ref, h_ref, g_ref, k_ref, o_ref):
    n2 = DFT_N2
    x = _dot(h_ref[...], _unpack_complex(a_ref))
    xr, xi = x[:n2], x[n2:]
    kr, ki = k_ref[0], k_ref[1]
    y = jnp.concatenate([xr * kr - xi * ki, xr * ki + xi * kr], axis=0).astype(BF16)
    zz = _dot(g_ref[...], y).astype(BF16)
    o_ref[...] = pltpu.bitcast(zz, jnp.uint32)


def dft_mid(a4, hmat, gmat, kspec):
    nb, n1, n2, c = a4.shape
    return pl.pallas_call(
        _dftmid_kernel,
        grid=(n1, nb),
        in_specs=[pl.BlockSpec((None, None, n2, c), lambda k, b: (b, k, 0, 0)),
                  pl.BlockSpec((None, 2 * n2, 2 * n2), lambda k, b: (k, 0, 0)),
                  pl.BlockSpec((None, 2 * n2, 2 * n2), lambda k, b: (k, 0, 0)),
                  pl.BlockSpec((2, None, n2, c), lambda k, b: (0, k, 0, 0))],
        out_specs=pl.BlockSpec((None, None, n2, c), lambda k, b: (b, k, 0, 0)),
        out_shape=jax.ShapeDtypeStruct(a4.shape, jnp.uint32),
        compiler_params=_cparams("parallel", "arbitrary"),
        name="dft_mid",
    )(a4, hmat, gmat, kspec)


def _dftout_kernel(g_ref, z_ref, x0_ref, w_ref, skip_ref, o_ref):
    n1, rt, tc = z_ref.shape
    z = pltpu.bitcast(z_ref[...].reshape(n1 * rt, tc), BF16)
    y = _dot(g_ref[...], z).reshape(o_ref.shape)
    o_ref[...] = x0_ref[...] * (y + w_ref[...] * skip_ref[...])


def dft_out(g2, z4, x0, w, skip, tc=512):
    nb, n1, n2, c = z4.shape
    rt = DFT_ROWS
    kh = g2.shape[0] // rt
    tc = _tile(c, tc)
    tok = pl.BlockSpec((None, kh, rt, tc), lambda b, i, j: (b, 0, i, j))
    return pl.pallas_call(
        _dftout_kernel,
        grid=(nb, n2 // rt, c // tc),
        in_specs=[pl.BlockSpec(g2.shape, lambda b, i, j: (0, 0)),
                  pl.BlockSpec((None, n1, rt, tc), lambda b, i, j: (b, 0, i, j)),
                  tok, tok,
                  pl.BlockSpec((1, 1, tc), lambda b, i, j: (0, 0, j))],
        out_specs=tok,
        out_shape=jax.ShapeDtypeStruct((nb, kh, n2, c), F32),
        compiler_params=_cparams("parallel", "parallel", "arbitrary"),
        name="dft_out",
    )(g2, z4, x0, w, skip.reshape(1, 1, c).astype(F32))


def _rope_tables(seq):
    hd = ATTN_HEAD_DIM
    axis_dim = hd // 2
    t = jnp.arange(seq)
    row = (t // GRID_W).astype(F32)
    col = (t % GRID_W).astype(F32)
    inv_freq = ROPE_THETA ** (-jnp.arange(0, axis_dim, 2, dtype=F32) / axis_dim)
    ang = jnp.concatenate([row[:, None] * inv_freq, col[:, None] * inv_freq], axis=-1)
    c, s = jnp.cos(ang), jnp.sin(ang)
    return jnp.concatenate([c, c], axis=-1), jnp.concatenate([-s, s], axis=-1)


def _hyena_features(seq, d):
    t = jnp.linspace(0.0, 1.0, seq, dtype=F32)
    w = 2.0 * math.pi * jnp.arange(seq, dtype=F32) / seq
    f = jnp.linspace(1e-4, HY_BANDS - 1, HY_BANDS, dtype=F32)
    fw = w[:, None] * f[None, :]
    z = jnp.concatenate([t[:, None], jnp.cos(fw), -jnp.sin(fw)], axis=-1)
    z = jnp.pad(z, ((0, 0), (0, LANES - HY_EMB)))
    deltas = jnp.abs(jnp.linspace(math.log(HY_TARGET) / HY_SLOW_PCT,
                                  math.log(HY_TARGET) / HY_FAST_PCT, d, dtype=F32))
    return z, t[:, None], deltas[None, :]


def _dft_tables(seq):
    n = 2 * seq
    n2 = DFT_N2
    n1 = n // n2

    def cs(phase_int, mod):
        ang = (-2.0 * math.pi / mod) * (phase_int % mod).astype(F32)
        return jnp.cos(ang), jnp.sin(ang)

    k1 = jnp.arange(n1)[:, None]
    m1 = jnp.arange(n1 // 2)[None, :]
    fr, fi = cs(k1 * m1, n1)
    base = jnp.stack([fr, fi], axis=-1)
    eye = jnp.eye(DFT_ROWS, dtype=F32)
    rows = DFT_ROWS
    g1 = jnp.einsum('knp,rs->krpns', base, eye).reshape(n1 * rows * 2, (n1 // 2) * rows).astype(BF16)
    g2 = (jnp.einsum('knp,rs->nrksp', base, eye).reshape((n1 // 2) * rows, n1 * rows * 2) / n).astype(BF16)
    kk = (jnp.arange(n1)[:, None, None] + n1 * jnp.arange(n2)[None, :, None])
    nn = jnp.arange(n2)[None, None, :]
    hr, hi = cs(kk * nn, n)
    hmat = jnp.concatenate([jnp.stack([hr, -hi], axis=-1).reshape(n1, n2, 2 * n2),
                            jnp.stack([hi, hr], axis=-1).reshape(n1, n2, 2 * n2)], axis=1).astype(BF16)
    gr, gi = jnp.swapaxes(hr, 1, 2), -jnp.swapaxes(hi, 1, 2)
    gmat = jnp.stack([jnp.concatenate([gr, -gi], axis=2),
                      jnp.concatenate([gi, gr], axis=2)], axis=2).reshape(n1, 2 * n2, 2 * n2).astype(BF16)
    return g1, g2, hmat, gmat


def _deinterleave(nheads):
    hd = ATTN_HEAD_DIM
    one = np.concatenate([np.arange(0, hd, 2), np.arange(1, hd, 2)])
    return np.concatenate([h * hd + one for h in range(nheads)])


def kernel(x_prompt, x_sample, mem_prompt, mem_sample, norm_mix, norm_xa, norm_mem, norm_ffn, xa_wq, xa_wk, xa_wv, xa_wo, ffn_w_in, ffn_conv_w, ffn_conv_b, ffn_w_out, mix_w_in, mix_w_out, ssd_conv_w, ssd_conv_b, ssd_a_log, ssd_dt_bias, ssd_d, ssd_norm, attn_q_norm, attn_k_norm, hy_w_in, hy_conv_w, hy_conv_b, hy_f_w1, hy_f_b1, hy_f_w2, hy_f_b2, hy_f_w3, hy_f_b3, hy_f_freq, hy_f_w_out, hy_skip, hy_w_out, final_norm):
    nbp, seq, d = x_prompt.shape
    nbs = x_sample.shape[0]
    assert x_sample.shape[1] == seq
    nb = nbp + nbs
    T = nb * seq
    depth = norm_mix.shape[0]
    n_mem = mem_prompt.shape[1]
    d_ff = ffn_w_out.shape[1]

    x = jnp.concatenate([x_prompt, x_sample], axis=0).reshape(T, d)
    mem = jnp.concatenate([mem_prompt, mem_sample], axis=0).reshape(nb * n_mem, d)

    d_ssd = d
    nheads = d_ssd // SSD_HEAD_DIM
    gn = SSD_GROUPS * SSD_STATE
    conv_ch = d_ssd + 2 * gn
    n_att = d // ATTN_HEAD_DIM
    d_kv = ATTN_KV_HEADS * ATTN_HEAD_DIM
    o1 = d_ssd
    o2 = o1 + conv_ch
    o3 = o2 + 2 * nheads
    o4 = o3 + d
    o5 = o4 + d_kv
    qcol = o1
    vcol = o1 + d + d_kv
    cos, sin = _rope_tables(seq)
    perm_q = _deinterleave(n_att)
    perm_k = _deinterleave(ATTN_KV_HEADS)
    perm_h = _deinterleave(1)

    n2 = DFT_N2
    n1 = 2 * seq // n2
    hz, t_col, deltas = _hyena_features(seq, d)
    g1, g2, hmat, gmat = _dft_tables(seq)

    for i in range(depth):
        if i % 2 == 0:
            e = i // 2
            w = mix_w_in[e].astype(BF16)
            w_main = jnp.concatenate([w[:, :o1], w[:, o3:o4][:, perm_q], w[:, o4:o5][:, perm_k], w[:, o5:]],
                                     axis=1)
            w_dt = jnp.pad(w[:, o2:o3], ((0, 0), (0, LANES - 2 * nheads)))
            proj = normmm(x, norm_mix[i], w_main, tm=512, tn=w_main.shape[1])
            dtraw = normmm(x, norm_mix[i], w_dt, out_dtype=F32, tn=LANES)
            (xbc,) = normmm_conv(x, norm_mix[i], w[:, o1:o2], ssd_conv_w[e], ssd_conv_b[e],
                                 [0], conv_ch, _epi_silu, 1, seq)
            pad_row = lambda a: jnp.pad(a.reshape(1, -1).astype(F32), ((0, 0), (0, LANES - 2 * nheads)))
            y_ssd = ssd_scan(xbc, dtraw, pad_row(ssd_dt_bias[e]), pad_row(ssd_a_log[e]),
                             jnp.repeat(ssd_d[e].astype(F32), SSD_HEAD_DIM)[None, :],
                             proj, ssd_norm[e].reshape(1, -1).astype(F32), nb, seq)
            scale = ATTN_HEAD_DIM ** -0.5 * math.log2(math.e)
            gains = jnp.concatenate([jnp.tile(attn_q_norm[e][perm_h][None, :] * scale, (n_att, 1)),
                                     jnp.tile(attn_k_norm[e][perm_h][None, :], (ATTN_KV_HEADS, 1))],
                                    axis=0)[:, None, :].astype(F32)
            qk = qk_prep(proj, qcol, n_att, ATTN_KV_HEADS, gains, cos, sin, seq)
            y_att = flash_attention(qk, proj, vcol, n_att, nb, seq)
            x = mm_res(jnp.concatenate([y_ssd, y_att], axis=1), mix_w_out[e].astype(BF16), x, tm=512, tn=d)
        else:
            o = i // 2
            x0, wv = normmm_conv(x, norm_mix[i], hy_w_in[o].astype(BF16), hy_conv_w[o], hy_conv_b[o],
                                 [0, d, 2 * d], d, _epi_hyena, 2, seq, out_dtype=F32)
            w1 = jnp.pad(hy_f_w1[o], ((0, LANES - HY_EMB), (0, 0)))
            hfb, sums = hyena_filter(hz, t_col, deltas, w1, hy_f_b1[o], hy_f_w2[o], hy_f_b2[o],
                                     hy_f_w3[o], hy_f_b3[o], hy_f_freq[o], hy_f_w_out[o])
            a_f = dft_stage1(g1, hfb.reshape(1, n1 // 2, n2, 2 * d))
            kspec = filter_spectrum(a_f, hmat, sums, d)
            a_u = dft_stage1(g1, wv.reshape(nb, n1 // 2, n2, d))
            zz = dft_mid(a_u, hmat, gmat, kspec)
            yh = dft_out(g2, zz, x0.reshape(nb, n1 // 2, n2, d), wv.reshape(nb, n1 // 2, n2, d), hy_skip[o])
            x = mm_res(yh.reshape(T, d), hy_w_out[o].astype(BF16), x, tm=512, tn=d)
        q = normmm(x, norm_xa[i], xa_wq[i].astype(BF16), tn=d)
        kv = normmm(mem, norm_mem[i], jnp.concatenate([xa_wk[i].astype(BF16), xa_wv[i].astype(BF16)], axis=1))
        x = mm_res(xattn(q, kv, nb, seq), xa_wo[i].astype(BF16), x, tm=512, tn=d)
        (act,) = normmm_conv(x, norm_ffn[i], ffn_w_in[i].astype(BF16), ffn_conv_w[i], ffn_conv_b[i],
                             [0, d_ff], d_ff, _epi_glu, 1, seq,
                             **[dict(), dict(conv_rows=128), dict(conv_rows=32), dict(tn=256)][i % 4])
        x = mm_res(act, ffn_w_out[i].astype(BF16), x, tm=512, tn=d)

    y_prompt = rmsnorm(x, final_norm, 0, nbp * seq).reshape(nbp, seq, d)
    y_sample = rmsnorm(x, final_norm, nbp * seq, nbs * seq).reshape(nbs, seq, d)
    return (y_prompt, y_sample)
```

```python
import functools
import math

import numpy as np
import jax
import jax.numpy as jnp
from jax import lax
from jax.experimental import pallas as pl
from jax.experimental.pallas import tpu as pltpu

F32 = jnp.float32
BF16 = jnp.bfloat16
EPS = 1e-6

GRID_W = 64
XA_HEADS = 4
SSD_HEAD_DIM = 64
SSD_GROUPS = 4
SSD_STATE = 128
SSD_CHUNK = 128
ATTN_HEAD_DIM = 128
ATTN_KV_HEADS = 4
ROPE_THETA = 10000.0
HY_EMB = 33
HY_BANDS = (HY_EMB - 1) // 2
HY_TARGET = 1e-2
HY_FAST_PCT = 0.3
HY_SLOW_PCT = 1.5

LANES = 128
DFT_N2 = 128
VMEM_LIMIT = 56 * 1024 * 1024
NEG_BIG = -1e30


def _cparams(*sem):
    return pltpu.CompilerParams(dimension_semantics=sem, vmem_limit_bytes=VMEM_LIMIT)


def _tile(dim, pref):
    t = min(dim, pref)
    while dim % t:
        t //= 2
    return t


def _split3(x):
    hi = x.astype(BF16)
    r1 = x - hi.astype(F32)
    mid = r1.astype(BF16)
    lo = (r1 - mid.astype(F32)).astype(BF16)
    return hi, mid, lo


def _dot(a, b):
    return jnp.dot(a, b, preferred_element_type=F32)


def _dot_exact_rhs(x, e):
    hi, mid, lo = _split3(x)
    return _dot(hi, e) + _dot(mid, e) + _dot(lo, e)


def _dot_exact_lhs(e, x):
    hi, mid, lo = _split3(x)
    return _dot(e, hi) + _dot(e, mid) + _dot(e, lo)


def _dot_f32(a, b):
    ah, am, _ = _split3(a)
    bh, bm, _ = _split3(b)
    return _dot(ah, bh) + _dot(ah, bm) + _dot(am, bh)


def _silu(x):
    return x * (1.0 / (1.0 + jnp.exp(-x)))


def _normmm_kernel(x_ref, g_ref, w_ref, o_ref, xn_ref):
    @pl.when(pl.program_id(1) == 0)
    def _():
        x = x_ref[...].astype(F32)
        ms = jnp.mean(x * x, axis=-1, keepdims=True)
        xn_ref[...] = (x * lax.rsqrt(ms + EPS) * g_ref[...]).astype(BF16)

    o_ref[...] = _dot(xn_ref[...], w_ref[...]).astype(o_ref.dtype)


def normmm(x, g, w, out_dtype=BF16, tm=1024, tn=1024):
    M, K = x.shape
    N = w.shape[1]
    tm = _tile(M, tm)
    tn = _tile(N, tn)
    return pl.pallas_call(
        _normmm_kernel,
        grid=(M // tm, N // tn),
        in_specs=[pl.BlockSpec((tm, K), lambda i, j: (i, 0)),
                  pl.BlockSpec((1, K), lambda i, j: (0, 0)),
                  pl.BlockSpec((K, tn), lambda i, j: (0, j),
                               pipeline_mode=pl.Buffered(1) if tn == N else None)],
        out_specs=pl.BlockSpec((tm, tn), lambda i, j: (i, j)),
        out_shape=jax.ShapeDtypeStruct((M, N), out_dtype),
        scratch_shapes=[pltpu.VMEM((tm, K), BF16)],
        compiler_params=_cparams("parallel", "arbitrary"),
        name="normmm",
    )(x, g.reshape(1, K).astype(F32), w)


def _mmres_kernel(a_ref, w_ref, r_ref, o_ref):
    o_ref[...] = r_ref[...] + _dot(a_ref[...].astype(BF16), w_ref[...])


def mm_res(a, w, res, tm=1024, tn=None):
    M, K = a.shape
    N = w.shape[1]
    tm = _tile(M, tm)
    tn = _tile(N, tn or (1024 if K <= 2048 else 512))
    w_mode = pl.Buffered(1) if tn == N else None
    return pl.pallas_call(
        _mmres_kernel,
        grid=(M // tm, N // tn),
        in_specs=[pl.BlockSpec((tm, K), lambda i, j: (i, 0)),
                  pl.BlockSpec((K, tn), lambda i, j: (0, j), pipeline_mode=w_mode),
                  pl.BlockSpec((tm, tn), lambda i, j: (i, j))],
        out_specs=pl.BlockSpec((tm, tn), lambda i, j: (i, j)),
        out_shape=jax.ShapeDtypeStruct((M, N), F32),
        compiler_params=_cparams("parallel", "arbitrary"),
        name="mm_res",
    )(a, w, res)


def _rmsnorm_kernel(x_ref, g_ref, o_ref):
    x = x_ref[...]
    ms = jnp.mean(x * x, axis=-1, keepdims=True)
    o_ref[...] = x * lax.rsqrt(ms + EPS) * g_ref[...]


def rmsnorm(x, g, row0, rows, tm=512):
    K = x.shape[1]
    tm = _tile(math.gcd(row0, rows) if row0 else rows, tm)
    return pl.pallas_call(
        _rmsnorm_kernel,
        grid=(rows // tm,),
        in_specs=[pl.BlockSpec((tm, K), lambda i: (i + row0 // tm, 0)),
                  pl.BlockSpec((1, K), lambda i: (0, 0))],
        out_specs=pl.BlockSpec((tm, K), lambda i: (i, 0)),
        out_shape=jax.ShapeDtypeStruct((rows, K), F32),
        compiler_params=_cparams("parallel"),
        name="final_norm",
    )(x, g.reshape(1, K).astype(F32))


CONV_HALO = 16
CONV_ROWS = 64


def _normmm_conv_kernel(*refs, nseg, width, tm, seq, epilogue, nout):
    xm_ref, xp_ref, xn_ref, g_ref = refs[:4]
    segs = [refs[4 + 3 * s:7 + 3 * s] for s in range(nseg)]
    outs = refs[4 + 3 * nseg:4 + 3 * nseg + nout]
    hn_ref = refs[4 + 3 * nseg + nout]
    exts = refs[5 + 3 * nseg + nout:]
    h = CONV_HALO
    half = width // 2
    row0 = pl.program_id(0) * tm
    at_start = (row0 % seq) == 0
    at_end = ((row0 + tm) % seq) == 0

    @pl.when(pl.program_id(1) == 0)
    def _():
        def nrm(x):
            ms = jnp.mean(x * x, axis=-1, keepdims=True)
            return (x * lax.rsqrt(ms + EPS) * g_ref[...]).astype(BF16)
        hn_ref[0:h, :] = nrm(xp_ref[...])
        hn_ref[h:h + tm, :] = nrm(xm_ref[...])
        hn_ref[h + tm:h + tm + h, :] = nrm(xn_ref[...])

    tn = exts[0].shape[1]
    for (w_ref, _, _), ext in zip(segs, exts):
        ext[...] = _dot(hn_ref[...], w_ref[...])
        ext[0:h, :] = jnp.where(at_start, 0.0, ext[0:h, :])
        ext[h + tm:h + tm + h, :] = jnp.where(at_end, 0.0, ext[h + tm:h + tm + h, :])

    for rb in range(0, tm, CONV_ROWS):
        for lc in range(0, tn, LANES):
            ls = slice(lc, lc + LANES)
            vals = []
            for (_, cw_ref, cb_ref), ext in zip(segs, exts):
                acc = None
                for k in range(width):
                    term = ext[h - half + k + rb:h - half + k + rb + CONV_ROWS, ls] * cw_ref[k:k + 1, ls]
                    acc = term if acc is None else acc + term
                vals.append(acc + cb_ref[:, ls])
            for o, r in zip(outs, epilogue(*vals)):
                o[rb:rb + CONV_ROWS, ls] = r.astype(o.dtype)


def normmm_conv(x, g, w, conv_w, conv_b, seg_cols, width_cols, epilogue, nout, seq, out_dtype=BF16,
                tm=1024, tn=512):
    T, K = x.shape
    width = conv_w.shape[0]
    nseg = len(seg_cols)
    tm = _tile(seq, tm)
    tn = _tile(width_cols, tn)
    assert tm % CONV_ROWS == 0 and tn % LANES == 0
    h = CONV_HALO
    nrb = T // h
    cb = conv_b.reshape(1, -1).astype(F32)
    cw = conv_w.astype(F32)
    in_specs = [pl.BlockSpec((tm, K), lambda i, j: (i, 0)),
                pl.BlockSpec((h, K), lambda i, j: (jnp.maximum(i * (tm // h) - 1, 0), 0)),
                pl.BlockSpec((h, K), lambda i, j: (jnp.minimum((i + 1) * (tm // h), nrb - 1), 0)),
                pl.BlockSpec((1, K), lambda i, j: (0, 0))]
    args = [x, x, x, g.reshape(1, K).astype(F32)]
    for c0 in seg_cols:
        off = c0 // tn
        in_specs += [pl.BlockSpec((K, tn), lambda i, j, off=off: (0, j + off)),
                     pl.BlockSpec((width, tn), lambda i, j, off=off: (0, j + off)),
                     pl.BlockSpec((1, tn), lambda i, j, off=off: (0, j + off))]
        args += [w, cw, cb]
    kern = functools.partial(_normmm_conv_kernel, nseg=nseg, width=width, tm=tm, seq=seq,
                             epilogue=epilogue, nout=nout)
    return pl.pallas_call(
        kern,
        grid=(T // tm, width_cols // tn),
        in_specs=in_specs,
        out_specs=[pl.BlockSpec((tm, tn), lambda i, j: (i, j)) for _ in range(nout)],
        out_shape=[jax.ShapeDtypeStruct((T, width_cols), out_dtype) for _ in range(nout)],
        scratch_shapes=[pltpu.VMEM((tm + 2 * h, K), BF16)]
        + [pltpu.VMEM((tm + 2 * h, tn), F32) for _ in range(nseg)],
        compiler_params=_cparams("parallel", "arbitrary"),
        name="normmm_conv",
    )(*args)


def _epi_silu(c):
    return (_silu(c),)


def _epi_glu(g, up):
    return (_silu(g) * up,)


def _epi_hyena(x0, x1, v):
    return (x0, v * x1)


def _softplus(x):
    return jnp.maximum(x, 0.0) + jnp.log(1.0 + jnp.exp(-jnp.abs(x)))


def _ssd_kernel(*refs, rev, nheads):
    if rev:
        (xs_ref, b_ref, c_ref, dt_ref, bias_ref, alog_ref, e_ref,
         yf_ref, z_ref, gain_ref, o_ref, s_ref, y_ref) = refs
    else:
        (xs_ref, b_ref, c_ref, dt_ref, bias_ref, alog_ref, e_ref,
         dskip_ref, o_ref, s_ref) = refs
        y_ref = o_ref
    Q = SSD_CHUNK
    P = SSD_HEAD_DIM
    hpg = nheads // SSD_GROUPS
    gw = hpg * P
    hoff = nheads if rev else 0

    @pl.when(pl.program_id(1) == 0)
    def _():
        s_ref[...] = jnp.zeros_like(s_ref)

    row = lax.broadcasted_iota(jnp.int32, (Q, Q), 0)
    col = lax.broadcasted_iota(jnp.int32, (Q, Q), 1)
    mask = (col >= row) if rev else (col <= row)
    tri = jnp.where(mask, 1.0, 0.0).astype(BF16)

    dtv = _softplus(dt_ref[...] + bias_ref[...])
    a_row = -jnp.exp(alog_ref[...])
    la = dtv * a_row
    cs = _dot_exact_lhs(tri, la)
    tot = cs[0:1, :] if rev else cs[Q - 1:Q, :]
    cs_t = cs.T
    dt_t = dtv.T
    e = e_ref[...]
    ecs_hi, ecs_lo, _ = _split3(jnp.exp(cs))
    carry_in = _dot(ecs_hi, e) + _dot(ecs_lo, e)
    to_end = _dot((jnp.exp(tot - cs) * dtv).astype(BF16), e)
    dec = _dot_exact_rhs(jnp.broadcast_to(jnp.exp(tot), (8, LANES)), e)[0:1, :]

    xs = xs_ref[...]
    x_state = (xs.astype(F32) * to_end).astype(BF16)
    lane = lax.broadcasted_iota(jnp.int32, (Q, LANES), 1)
    low = lane < P

    for g in range(SSD_GROUPS):
        bg = b_ref[:, g * SSD_STATE:(g + 1) * SSD_STATE]
        cg = c_ref[:, g * SSD_STATE:(g + 1) * SSD_STATE]
        cb = lax.dot_general(cg, bg, (((1,), (1,)), ((), ())), preferred_element_type=F32)
        s_old = s_ref[g]
        y_off = _dot(cg, s_old.astype(BF16)) * carry_in[:, g * gw:(g + 1) * gw]
        s_ref[g] = s_old * dec[:, g * gw:(g + 1) * gw] + lax.dot_general(
            bg, x_state[:, g * gw:(g + 1) * gw], (((0,), (0,)), ((), ())), preferred_element_type=F32)
        for j in range(hpg // 2):
            ws = []
            for hh in range(2):
                hc = hoff + g * hpg + 2 * j + hh
                diff = cs[:, hc:hc + 1] - cs_t[hc:hc + 1, :]
                decay = jnp.exp(jnp.where(mask, diff, NEG_BIG))
                ws.append((cb * decay * dt_t[hc:hc + 1, :]).astype(BF16))
            c0 = g * gw + 2 * j * P
            xp = xs[:, c0:c0 + LANES]
            rhs = jnp.concatenate([jnp.where(low, xp, jnp.zeros_like(xp)),
                                   jnp.where(low, jnp.zeros_like(xp), xp)], axis=0)
            y = _dot(jnp.concatenate(ws, axis=1), rhs) + y_off[:, 2 * j * P:2 * j * P + LANES]
            if not rev:
                y = y + xp.astype(F32) * dskip_ref[:, c0:c0 + LANES]
            y_ref[:, c0:c0 + LANES] = y

    if rev:
        y = y_ref[...] + yf_ref[...]
        gated = y * _silu(z_ref[...].astype(F32))
        ms = jnp.mean(gated * gated, axis=-1, keepdims=True)
        o_ref[...] = (gated * lax.rsqrt(ms + EPS) * gain_ref[...]).astype(o_ref.dtype)


def ssd_scan(xbc, dtraw, bias_row, alog_row, d_row, z_src, gain_row, nb, seq):
    T = xbc.shape[0]
    Q = SSD_CHUNK
    nc = seq // Q
    gn = SSD_GROUPS * SSD_STATE
    hp = xbc.shape[1] - 2 * gn
    nheads = hp // SSD_HEAD_DIM
    hpg = nheads // SSD_GROUPS
    gw = hpg * SSD_HEAD_DIM
    assert hp % gn == 0 and 2 * nheads <= LANES

    def e_mat(off):
        r = np.arange(LANES)[:, None]
        c = np.arange(hp)[None, :]
        return jnp.asarray((r == off + c // SSD_HEAD_DIM).astype(np.float32), dtype=BF16)

    def specs(rev):
        def blk(c):
            return (nc - 1 - c) if rev else c
        return [
            pl.BlockSpec((Q, hp), lambda b, c: (b * nc + blk(c), 0)),
            pl.BlockSpec((Q, gn), lambda b, c: (b * nc + blk(c), hp // gn)),
            pl.BlockSpec((Q, gn), lambda b, c: (b * nc + blk(c), hp // gn + 1)),
            pl.BlockSpec((Q, LANES), lambda b, c: (b * nc + blk(c), 0)),
            pl.BlockSpec((1, LANES), lambda b, c: (0, 0)),
            pl.BlockSpec((1, LANES), lambda b, c: (0, 0)),
            pl.BlockSpec((LANES, hp), lambda b, c: (0, 0)),
        ], (lambda b, c: (b * nc + blk(c), 0))

    in_f, omap_f = specs(False)
    yf = pl.pallas_call(
        functools.partial(_ssd_kernel, rev=False, nheads=nheads),
        grid=(nb, nc),
        in_specs=in_f + [pl.BlockSpec((1, hp), lambda b, c: (0, 0))],
        out_specs=pl.BlockSpec((Q, hp), omap_f),
        out_shape=jax.ShapeDtypeStruct((T, hp), F32),
        scratch_shapes=[pltpu.VMEM((SSD_GROUPS, SSD_STATE, gw), F32)],
        compiler_params=_cparams("parallel", "arbitrary"),
        name="ssd_fwd",
    )(xbc, xbc, xbc, dtraw, bias_row, alog_row, e_mat(0), d_row)
    in_b, omap_b = specs(True)
    return pl.pallas_call(
        functools.partial(_ssd_kernel, rev=True, nheads=nheads),
        grid=(nb, nc),
        in_specs=in_b + [pl.BlockSpec((Q, hp), omap_b),
                         pl.BlockSpec((Q, hp), omap_b),
                         pl.BlockSpec((1, hp), lambda b, c: (0, 0))],
        out_specs=pl.BlockSpec((Q, hp), omap_b),
        out_shape=jax.ShapeDtypeStruct((T, hp), BF16),
        scratch_shapes=[pltpu.VMEM((SSD_GROUPS, SSD_STATE, gw), F32),
                        pltpu.VMEM((Q, hp), F32)],
        compiler_params=_cparams("parallel", "arbitrary"),
        name="ssd_bwd",
    )(xbc, xbc, xbc, dtraw, bias_row, alog_row, e_mat(nheads), yf, z_src, gain_row)


def _qkprep_kernel(q_ref, k_ref, g_ref, cos_ref, sin_ref, o_ref, *, nq, nk):
    hd = ATTN_HEAD_DIM
    cos = cos_ref[...]
    sin = sin_ref[...]
    for h in range(nq + nk):
        src, c0 = (q_ref, h * hd) if h < nq else (k_ref, (h - nq) * hd)
        x = src[:, c0:c0 + hd].astype(F32)
        ms = jnp.mean(x * x, axis=-1, keepdims=True)
        xn = x * lax.rsqrt(ms + EPS) * g_ref[h]
        o_ref[:, h * hd:(h + 1) * hd] = (xn * cos + pltpu.roll(xn, hd // 2, 1) * sin).astype(o_ref.dtype)


def qk_prep(proj, qcol, nq, nk, gains, cos, sin, seq, tq=256):
    T = proj.shape[0]
    hd = ATTN_HEAD_DIM
    tq = _tile(seq, tq)
    spt = seq // tq
    assert qcol % (nq * hd) == 0 and (qcol + nq * hd) % (nk * hd) == 0
    return pl.pallas_call(
        functools.partial(_qkprep_kernel, nq=nq, nk=nk),
        grid=(T // tq,),
        in_specs=[pl.BlockSpec((tq, nq * hd), lambda i: (i, qcol // (nq * hd))),
                  pl.BlockSpec((tq, nk * hd), lambda i: (i, (qcol + nq * hd) // (nk * hd))),
                  pl.BlockSpec((nq + nk, 1, hd), lambda i: (0, 0, 0)),
                  pl.BlockSpec((tq, hd), lambda i: (i % spt, 0)),
                  pl.BlockSpec((tq, hd), lambda i: (i % spt, 0))],
        out_specs=pl.BlockSpec((tq, (nq + nk) * hd), lambda i: (i, 0)),
        out_shape=jax.ShapeDtypeStruct((T, (nq + nk) * hd), BF16),
        compiler_params=_cparams("parallel"),
        name="qk_prep",
    )(proj, proj, gains, cos, sin)


FLASH_ROW_BLOCK = 32


def _flash_kernel(q_ref, k_ref, v_ref, o_ref, qs_ref, va_ref, s0_ref, s1_ref, p_ref, acc_ref,
                  m_ref, al_ref, *, tk, group):
    hd = ATTN_HEAD_DIM
    tq = q_ref.shape[0]
    rows = group * tq
    seq = k_ref.shape[0]
    nk = seq // tk

    @pl.when(pl.program_id(2) == 0)
    def _():
        va_ref[:, 0:hd] = v_ref[...]
        va_ref[:, hd:2 * hd] = jnp.ones((seq, hd), BF16)

    for g in range(group):
        qs_ref[g * tq:(g + 1) * tq, :] = q_ref[:, g * hd:(g + 1) * hd]
    m_ref[...] = jnp.full(m_ref.shape, NEG_BIG, F32)
    acc_ref[...] = jnp.zeros(acc_ref.shape, F32)
    nlc = tk // LANES

    def scores(t, s_ref):
        k0 = pl.multiple_of(t * tk, tk)
        s_ref[...] = lax.dot_general(qs_ref[...], k_ref[pl.ds(k0, tk), :], (((1,), (1,)), ((), ())),
                                     preferred_element_type=F32)

    def update(t, s_ref):
        for r0 in range(0, rows, FLASH_ROW_BLOCK):
            rs = slice(r0, r0 + FLASH_ROW_BLOCK)
            ch = [s_ref[rs, c * LANES:(c + 1) * LANES] for c in range(nlc)]
            mx = ch[0]
            for c in range(1, nlc):
                mx = jnp.maximum(mx, ch[c])
            m_old = m_ref[rs, :]
            m_new = jnp.maximum(m_old, jnp.max(mx, axis=-1, keepdims=True))
            m_ref[rs, :] = m_new
            al_ref[rs, :] = jnp.exp2(m_old - m_new)
            for c in range(nlc):
                p_ref[rs, c * LANES:(c + 1) * LANES] = jnp.exp2(ch[c] - m_new).astype(BF16)
        k0 = pl.multiple_of(t * tk, tk)
        pv = _dot(p_ref[...], va_ref[pl.ds(k0, tk), :])
        al = al_ref[...]
        acc_ref[...] = acc_ref[...] * jnp.concatenate([al, al], axis=1) + pv

    scores(0, s0_ref)

    def body(t2, carry):
        scores(2 * t2 + 1, s1_ref)
        update(2 * t2, s0_ref)
        scores(2 * t2 + 2, s0_ref)
        update(2 * t2 + 1, s1_ref)
        return carry

    lax.fori_loop(0, nk // 2 - 1, body, 0)
    scores(nk - 1, s1_ref)
    update(nk - 2, s0_ref)
    update(nk - 1, s1_ref)
    o = acc_ref[:, 0:hd] / acc_ref[:, hd:2 * hd]
    for g in range(group):
        o_ref[:, g * hd:(g + 1) * hd] = o[g * tq:(g + 1) * tq, :].astype(o_ref.dtype)


def flash_attention(qk, v_src, v_col0, nq_heads, nb, seq, tq=256, tk=1024):
    T = qk.shape[0]
    hd = ATTN_HEAD_DIM
    nkv = ATTN_KV_HEADS
    group = nq_heads // nkv
    tq = _tile(seq, tq)
    tk = _tile(seq // 2, tk)
    nqt = seq // tq
    return pl.pallas_call(
        functools.partial(_flash_kernel, tk=tk, group=group),
        grid=(nb, nkv, nqt),
        in_specs=[pl.BlockSpec((tq, group * hd), lambda b, h, i: (b * nqt + i, h)),
                  pl.BlockSpec((seq, hd), lambda b, h, i: (b, nq_heads + h)),
                  pl.BlockSpec((seq, hd), lambda b, h, i: (b, v_col0 // hd + h))],
        out_specs=pl.BlockSpec((tq, group * hd), lambda b, h, i: (b * nqt + i, h)),
        out_shape=jax.ShapeDtypeStruct((T, nq_heads * hd), BF16),
        scratch_shapes=[pltpu.VMEM((group * tq, hd), BF16),
                        pltpu.VMEM((seq, 2 * hd), BF16),
                        pltpu.VMEM((group * tq, tk), F32),
                        pltpu.VMEM((group * tq, tk), F32),
                        pltpu.VMEM((group * tq, tk), BF16),
                        pltpu.VMEM((group * tq, 2 * hd), F32),
                        pltpu.VMEM((group * tq, LANES), F32),
                        pltpu.VMEM((group * tq, LANES), F32)],
        compiler_params=_cparams("arbitrary", "arbitrary", "arbitrary"),
        name="flash_attn",
    )(qk, qk, v_src)


def _xattn_kernel(q_ref, kv_ref, o_ref, *, heads):
    d = q_ref.shape[1]
    hd = d // heads
    scale = hd ** -0.5
    for h in range(heads):
        q = q_ref[:, h * hd:(h + 1) * hd]
        k = kv_ref[:, h * hd:(h + 1) * hd]
        v = kv_ref[:, d + h * hd:d + (h + 1) * hd]
        s = lax.dot_general(q, k, (((1,), (1,)), ((), ())), preferred_element_type=F32) * scale
        p = jnp.exp(s - jnp.max(s, axis=-1, keepdims=True))
        l = jnp.sum(p, axis=-1, keepdims=True)
        o = _dot(p.astype(BF16), v) / l
        o_ref[:, h * hd:(h + 1) * hd] = o.astype(o_ref.dtype)


def xattn(q, kv, nb, seq, tq=512):
    T, d = q.shape
    n_mem = kv.shape[0] // nb
    tq = _tile(seq, tq)
    nqt = seq // tq
    return pl.pallas_call(
        functools.partial(_xattn_kernel, heads=XA_HEADS),
        grid=(nb, nqt),
        in_specs=[pl.BlockSpec((tq, d), lambda b, i: (b * nqt + i, 0)),
                  pl.BlockSpec((n_mem, 2 * d), lambda b, i: (b, 0))],
        out_specs=pl.BlockSpec((tq, d), lambda b, i: (b * nqt + i, 0)),
        out_shape=jax.ShapeDtypeStruct((T, d), BF16),
        compiler_params=_cparams("parallel", "arbitrary"),
        name="xattn",
    )(q, kv)


def _hyfilter_kernel(z_ref, t_ref, dl_ref, w1_ref, b1_ref, w2_ref, b2_ref, w3_ref, b3_ref,
                     fr_ref, wo_ref, h_ref, sum_ref, *, tl, d):
    i = pl.program_id(0)
    fr = fr_ref[...]
    h = jnp.sin(fr * (_dot_f32(z_ref[...], w1_ref[...]) + b1_ref[...]))
    h = jnp.sin(fr * (_dot_f32(h, w2_ref[...]) + b2_ref[...]))
    h = jnp.sin(fr * (_dot_f32(h, w3_ref[...]) + b3_ref[...]))
    window = jnp.exp(-t_ref[...] * dl_ref[...])
    rows = lax.broadcasted_iota(jnp.int32, (tl, 1), 0) + i * tl

    @pl.when(i == 0)
    def _():
        sum_ref[...] = jnp.zeros_like(sum_ref)

    for part in range(2):
        hp = _dot_f32(h, wo_ref[:, part * d:(part + 1) * d]) * window
        if part == 1:
            hp = jnp.where(rows == 0, 0.0, hp)
        h_ref[:, part * d:(part + 1) * d] = hp.astype(h_ref.dtype)
        sum_ref[:, part * d:(part + 1) * d] += jnp.sum(jnp.abs(hp), axis=0, keepdims=True)


def hyena_filter(z, t_col, deltas, w1, b1, w2, b2, w3, b3, freq, w_out, tl=256):
    L = z.shape[0]
    d2 = w_out.shape[1]
    d = d2 // 2
    fw = w2.shape[0]
    tl = _tile(L, tl)
    full = lambda a: pl.BlockSpec(a.shape, lambda i: (0,) * a.ndim)
    ops = [w1, b1.reshape(1, fw), w2, b2.reshape(1, fw), w3, b3.reshape(1, fw), freq.reshape(1, fw), w_out]
    return pl.pallas_call(
        functools.partial(_hyfilter_kernel, tl=tl, d=d),
        grid=(L // tl,),
        in_specs=[pl.BlockSpec((tl, z.shape[1]), lambda i: (i, 0)),
                  pl.BlockSpec((tl, 1), lambda i: (i, 0)),
                  full(deltas)] + [full(a) for a in ops],
        out_specs=[pl.BlockSpec((tl, d2), lambda i: (i, 0)),
                   pl.BlockSpec((1, d2), lambda i: (0, 0))],
        out_shape=[jax.ShapeDtypeStruct((L, d2), F32),
                   jax.ShapeDtypeStruct((1, d2), F32)],
        compiler_params=_cparams("arbitrary"),
        name="hyena_filter",
    )(z, t_col, deltas, *ops)


DFT_ROWS = 8


def _dft1_kernel(g_ref, u_ref, o_ref):
    kh, rt, tc = u_ref.shape
    u = u_ref[...].reshape(kh * rt, tc).astype(BF16)
    a = _dot(g_ref[...], u).astype(BF16)
    o_ref[...] = pltpu.bitcast(a, jnp.uint32).reshape(o_ref.shape)


def dft_stage1(g1, u4, tc=512):
    nb, kh, n2, c = u4.shape
    rt = DFT_ROWS
    n1 = g1.shape[0] // (2 * rt)
    tc = _tile(c, tc)
    return pl.pallas_call(
        _dft1_kernel,
        grid=(nb, n2 // rt, c // tc),
        in_specs=[pl.BlockSpec(g1.shape, lambda b, i, j: (0, 0)),
                  pl.BlockSpec((None, kh, rt, tc), lambda b, i, j: (b, 0, i, j))],
        out_specs=pl.BlockSpec((None, n1, rt, tc), lambda b, i, j: (b, 0, i, j)),
        out_shape=jax.ShapeDtypeStruct((nb, n1, n2, c), jnp.uint32),
        compiler_params=_cparams("parallel", "parallel", "arbitrary"),
        name="dft_stage1",
    )(g1, u4)


def _unpack_complex(ref):
    return pltpu.bitcast(ref[...], BF16)


def _spectrum_kernel(af_ref, ab_ref, h_ref, sum_ref, o_ref, *, d_cols):
    n2 = DFT_N2
    hm = h_ref[...]
    xf = _dot(hm, _unpack_complex(af_ref))
    xb = _dot(hm, _unpack_complex(ab_ref))
    inv = 1.0 / (sum_ref[:, 0:d_cols] + sum_ref[:, d_cols:2 * d_cols])
    o_ref[0] = (xf[:n2] + xb[:n2]) * inv
    o_ref[1] = (xf[n2:] - xb[n2:]) * inv


def filter_spectrum(a4, hmat, sums, d):
    n1 = a4.shape[1]
    n2 = DFT_N2
    return pl.pallas_call(
        functools.partial(_spectrum_kernel, d_cols=d),
        grid=(n1,),
        in_specs=[pl.BlockSpec((None, None, n2, d), lambda k: (0, k, 0, 0)),
                  pl.BlockSpec((None, None, n2, d), lambda k: (0, k, 0, 1)),
                  pl.BlockSpec((None, 2 * n2, 2 * n2), lambda k: (k, 0, 0)),
                  pl.BlockSpec((1, 2 * d), lambda k: (0, 0))],
        out_specs=pl.BlockSpec((2, None, n2, d), lambda k: (0, k, 0, 0)),
        out_shape=jax.ShapeDtypeStruct((2, n1, n2, d), F32),
        compiler_params=_cparams("arbitrary"),
        name="filter_spectrum",
    )(a4, a4, hmat, sums)


def _dftmid_kernel(a_ref, h_ref, g_ref, k_ref, o_ref):
    n2 = DFT_N2
    kr, ki = k_ref[0], k_ref[1]
    for b in range(a_ref.shape[0]):
        x = _dot(h_ref[...], pltpu.bitcast(a_ref[b], BF16))
        xr, xi = x[:n2], x[n2:]
        y = jnp.concatenate([xr * kr - xi * ki, xr * ki + xi * kr], axis=0).astype(BF16)
        zz = _dot(g_ref[...], y).astype(BF16)
        o_ref[b] = pltpu.bitcast(zz, jnp.uint32)


def dft_mid(a4, hmat, gmat, kspec):
    nb, n1, n2, c = a4.shape
    return pl.pallas_call(
        _dftmid_kernel,
        grid=(n1,),
        in_specs=[pl.BlockSpec((nb, None, n2, c), lambda k: (0, k, 0, 0)),
                  pl.BlockSpec((None, 2 * n2, 2 * n2), lambda k: (k, 0, 0)),
                  pl.BlockSpec((None, 2 * n2, 2 * n2), lambda k: (k, 0, 0)),
                  pl.BlockSpec((2, None, n2, c), lambda k: (0, k, 0, 0))],
        out_specs=pl.BlockSpec((nb, None, n2, c), lambda k: (0, k, 0, 0)),
        out_shape=jax.ShapeDtypeStruct(a4.shape, jnp.uint32),
        compiler_params=_cparams("parallel"),
        name="dft_mid",
    )(a4, hmat, gmat, kspec)


def _dftout_kernel(g_ref, z_ref, x0_ref, w_ref, skip_ref, o_ref):
    n1, rt, tc = z_ref.shape
    z = pltpu.bitcast(z_ref[...].reshape(n1 * rt, tc), BF16)
    y = _dot(g_ref[...], z).reshape(o_ref.shape)
    o_ref[...] = x0_ref[...] * (y + w_ref[...] * skip_ref[...])


def dft_out(g2, z4, x0, w, skip, tc=512):
    nb, n1, n2, c = z4.shape
    rt = DFT_ROWS
    kh = g2.shape[0] // rt
    tc = _tile(c, tc)
    tok = pl.BlockSpec((None, kh, rt, tc), lambda b, i, j: (b, 0, i, j))
    return pl.pallas_call(
        _dftout_kernel,
        grid=(nb, n2 // rt, c // tc),
        in_specs=[pl.BlockSpec(g2.shape, lambda b, i, j: (0, 0)),
                  pl.BlockSpec((None, n1, rt, tc), lambda b, i, j: (b, 0, i, j)),
                  tok, tok,
                  pl.BlockSpec((1, 1, tc), lambda b, i, j: (0, 0, j))],
        out_specs=tok,
        out_shape=jax.ShapeDtypeStruct((nb, kh, n2, c), F32),
        compiler_params=_cparams("parallel", "parallel", "arbitrary"),
        name="dft_out",
    )(g2, z4, x0, w, skip.reshape(1, 1, c).astype(F32))


def _rope_tables(seq):
    hd = ATTN_HEAD_DIM
    axis_dim = hd // 2
    t = jnp.arange(seq)
    row = (t // GRID_W).astype(F32)
    col = (t % GRID_W).astype(F32)
    inv_freq = ROPE_THETA ** (-jnp.arange(0, axis_dim, 2, dtype=F32) / axis_dim)
    ang = jnp.concatenate([row[:, None] * inv_freq, col[:, None] * inv_freq], axis=-1)
    c, s = jnp.cos(ang), jnp.sin(ang)
    return jnp.concatenate([c, c], axis=-1), jnp.concatenate([-s, s], axis=-1)


def _hyena_features(seq, d):
    t = jnp.linspace(0.0, 1.0, seq, dtype=F32)
    w = 2.0 * math.pi * jnp.arange(seq, dtype=F32) / seq
    f = jnp.linspace(1e-4, HY_BANDS - 1, HY_BANDS, dtype=F32)
    fw = w[:, None] * f[None, :]
    z = jnp.concatenate([t[:, None], jnp.cos(fw), -jnp.sin(fw)], axis=-1)
    z = jnp.pad(z, ((0, 0), (0, LANES - HY_EMB)))
    deltas = jnp.abs(jnp.linspace(math.log(HY_TARGET) / HY_SLOW_PCT,
                                  math.log(HY_TARGET) / HY_FAST_PCT, d, dtype=F32))
    return z, t[:, None], deltas[None, :]


def _dft_tables(seq):
    n = 2 * seq
    n2 = DFT_N2
    n1 = n // n2

    def cs(phase_int, mod):
        ang = (-2.0 * math.pi / mod) * (phase_int % mod).astype(F32)
        return jnp.cos(ang), jnp.sin(ang)

    k1 = jnp.arange(n1)[:, None]
    m1 = jnp.arange(n1 // 2)[None, :]
    fr, fi = cs(k1 * m1, n1)
    base = jnp.stack([fr, fi], axis=-1)
    eye = jnp.eye(DFT_ROWS, dtype=F32)
    rows = DFT_ROWS
    g1 = jnp.einsum('knp,rs->krpns', base, eye).reshape(n1 * rows * 2, (n1 // 2) * rows).astype(BF16)
    g2 = (jnp.einsum('knp,rs->nrksp', base, eye).reshape((n1 // 2) * rows, n1 * rows * 2) / n).astype(BF16)
    kk = (jnp.arange(n1)[:, None, None] + n1 * jnp.arange(n2)[None, :, None])
    nn = jnp.arange(n2)[None, None, :]
    hr, hi = cs(kk * nn, n)
    hmat = jnp.concatenate([jnp.stack([hr, -hi], axis=-1).reshape(n1, n2, 2 * n2),
                            jnp.stack([hi, hr], axis=-1).reshape(n1, n2, 2 * n2)], axis=1).astype(BF16)
    gr, gi = jnp.swapaxes(hr, 1, 2), -jnp.swapaxes(hi, 1, 2)
    gmat = jnp.stack([jnp.concatenate([gr, -gi], axis=2),
                      jnp.concatenate([gi, gr], axis=2)], axis=2).reshape(n1, 2 * n2, 2 * n2).astype(BF16)
    return g1, g2, hmat, gmat


def _deinterleave(nheads):
    hd = ATTN_HEAD_DIM
    one = np.concatenate([np.arange(0, hd, 2), np.arange(1, hd, 2)])
    return np.concatenate([h * hd + one for h in range(nheads)])


def kernel(x_prompt, x_sample, mem_prompt, mem_sample, norm_mix, norm_xa, norm_mem, norm_ffn, xa_wq, xa_wk, xa_wv, xa_wo, ffn_w_in, ffn_conv_w, ffn_conv_b, ffn_w_out, mix_w_in, mix_w_out, ssd_conv_w, ssd_conv_b, ssd_a_log, ssd_dt_bias, ssd_d, ssd_norm, attn_q_norm, attn_k_norm, hy_w_in, hy_conv_w, hy_conv_b, hy_f_w1, hy_f_b1, hy_f_w2, hy_f_b2, hy_f_w3, hy_f_b3, hy_f_freq, hy_f_w_out, hy_skip, hy_w_out, final_norm):
    nbp, seq, d = x_prompt.shape
    nbs = x_sample.shape[0]
    assert x_sample.shape[1] == seq
    nb = nbp + nbs
    T = nb * seq
    depth = norm_mix.shape[0]
    n_mem = mem_prompt.shape[1]
    d_ff = ffn_w_out.shape[1]

    x = jnp.concatenate([x_prompt, x_sample], axis=0).reshape(T, d)
    mem = jnp.concatenate([mem_prompt, mem_sample], axis=0).reshape(nb * n_mem, d)

    d_ssd = d
    nheads = d_ssd // SSD_HEAD_DIM
    gn = SSD_GROUPS * SSD_STATE
    conv_ch = d_ssd + 2 * gn
    n_att = d // ATTN_HEAD_DIM
    d_kv = ATTN_KV_HEADS * ATTN_HEAD_DIM
    o1 = d_ssd
    o2 = o1 + conv_ch
    o3 = o2 + 2 * nheads
    o4 = o3 + d
    o5 = o4 + d_kv
    qcol = o1
    vcol = o1 + d + d_kv
    cos, sin = _rope_tables(seq)
    perm_q = _deinterleave(n_att)
    perm_k = _deinterleave(ATTN_KV_HEADS)
    perm_h = _deinterleave(1)

    n2 = DFT_N2
    n1 = 2 * seq // n2
    hz, t_col, deltas = _hyena_features(seq, d)
    g1, g2, hmat, gmat = _dft_tables(seq)

    for i in range(depth):
        if i % 2 == 0:
            e = i // 2
            w = mix_w_in[e].astype(BF16)
            w_main = jnp.concatenate([w[:, :o1], w[:, o3:o4][:, perm_q], w[:, o4:o5][:, perm_k], w[:, o5:]],
                                     axis=1)
            w_dt = jnp.pad(w[:, o2:o3], ((0, 0), (0, LANES - 2 * nheads)))
            proj = normmm(x, norm_mix[i], w_main, tm=512, tn=w_main.shape[1])
            dtraw = normmm(x, norm_mix[i], w_dt, out_dtype=F32, tn=LANES)
            (xbc,) = normmm_conv(x, norm_mix[i], w[:, o1:o2], ssd_conv_w[e], ssd_conv_b[e],
                                 [0], conv_ch, _epi_silu, 1, seq)
            pad_row = lambda a: jnp.pad(a.reshape(1, -1).astype(F32), ((0, 0), (0, LANES - 2 * nheads)))
            y_ssd = ssd_scan(xbc, dtraw, pad_row(ssd_dt_bias[e]), pad_row(ssd_a_log[e]),
                             jnp.repeat(ssd_d[e].astype(F32), SSD_HEAD_DIM)[None, :],
                             proj, ssd_norm[e].reshape(1, -1).astype(F32), nb, seq)
            scale = ATTN_HEAD_DIM ** -0.5 * math.log2(math.e)
            gains = jnp.concatenate([jnp.tile(attn_q_norm[e][perm_h][None, :] * scale, (n_att, 1)),
                                     jnp.tile(attn_k_norm[e][perm_h][None, :], (ATTN_KV_HEADS, 1))],
                                    axis=0)[:, None, :].astype(F32)
            qk = qk_prep(proj, qcol, n_att, ATTN_KV_HEADS, gains, cos, sin, seq)
            y_att = flash_attention(qk, proj, vcol, n_att, nb, seq)
            x = mm_res(jnp.concatenate([y_ssd, y_att], axis=1), mix_w_out[e].astype(BF16), x, tm=512, tn=d)
        else:
            o = i // 2
            x0, wv = normmm_conv(x, norm_mix[i], hy_w_in[o].astype(BF16), hy_conv_w[o], hy_conv_b[o],
                                 [0, d, 2 * d], d, _epi_hyena, 2, seq, out_dtype=F32)
            w1 = jnp.pad(hy_f_w1[o], ((0, LANES - HY_EMB), (0, 0)))
            hfb, sums = hyena_filter(hz, t_col, deltas, w1, hy_f_b1[o], hy_f_w2[o], hy_f_b2[o],
                                     hy_f_w3[o], hy_f_b3[o], hy_f_freq[o], hy_f_w_out[o])
            a_f = dft_stage1(g1, hfb.reshape(1, n1 // 2, n2, 2 * d))
            kspec = filter_spectrum(a_f, hmat, sums, d)
            a_u = dft_stage1(g1, wv.reshape(nb, n1 // 2, n2, d))
            zz = dft_mid(a_u, hmat, gmat, kspec)
            yh = dft_out(g2, zz, x0.reshape(nb, n1 // 2, n2, d), wv.reshape(nb, n1 // 2, n2, d), hy_skip[o])
            x = mm_res(yh.reshape(T, d), hy_w_out[o].astype(BF16), x, tm=512, tn=d)
        q = normmm(x, norm_xa[i], xa_wq[i].astype(BF16), tn=d)
        kv = normmm(mem, norm_mem[i], jnp.concatenate([xa_wk[i].astype(BF16), xa_wv[i].astype(BF16)], axis=1))
        x = mm_res(xattn(q, kv, nb, seq), xa_wo[i].astype(BF16), x, tm=512, tn=d)
        (act,) = normmm_conv(x, norm_ffn[i], ffn_w_in[i].astype(BF16), ffn_conv_w[i], ffn_conv_b[i],
                             [0, d_ff], d_ff, _epi_glu, 1, seq)
        x = mm_res(act, ffn_w_out[i].astype(BF16), x, tm=512, tn=d)

    y_prompt = rmsnorm(x, final_norm, 0, nbp * seq).reshape(nbp, seq, d)
    y_sample = rmsnorm(x, final_norm, nbp * seq, nbs * seq).reshape(nbs, seq, d)
    return (y_prompt, y_sample)
```

```python
import functools
import math

import numpy as np
import jax
import jax.numpy as jnp
from jax import lax
from jax.experimental import pallas as pl
from jax.experimental.pallas import tpu as pltpu

F32 = jnp.float32
BF16 = jnp.bfloat16
EPS = 1e-6

GRID_W = 64
XA_HEADS = 4
SSD_HEAD_DIM = 64
SSD_GROUPS = 4
SSD_STATE = 128
SSD_CHUNK = 128
ATTN_HEAD_DIM = 128
ATTN_KV_HEADS = 4
ROPE_THETA = 10000.0
HY_EMB = 33
HY_BANDS = (HY_EMB - 1) // 2
HY_TARGET = 1e-2
HY_FAST_PCT = 0.3
HY_SLOW_PCT = 1.5

LANES = 128
DFT_N2 = 128
VMEM_LIMIT = 56 * 1024 * 1024
NEG_BIG = -1e30


def _cparams(*sem):
    return pltpu.CompilerParams(dimension_semantics=sem, vmem_limit_bytes=VMEM_LIMIT)


def _tile(dim, pref):
    t = min(dim, pref)
    while dim % t:
        t //= 2
    return t


def _split3(x):
    hi = x.astype(BF16)
    r1 = x - hi.astype(F32)
    mid = r1.astype(BF16)
    lo = (r1 - mid.astype(F32)).astype(BF16)
    return hi, mid, lo


def _dot(a, b):
    return jnp.dot(a, b, preferred_element_type=F32)


def _dot_exact_rhs(x, e):
    hi, mid, lo = _split3(x)
    return _dot(hi, e) + _dot(mid, e) + _dot(lo, e)


def _dot_exact_lhs(e, x):
    hi, mid, lo = _split3(x)
    return _dot(e, hi) + _dot(e, mid) + _dot(e, lo)


def _dot_f32(a, b):
    ah, am, _ = _split3(a)
    bh, bm, _ = _split3(b)
    return _dot(ah, bh) + _dot(ah, bm) + _dot(am, bh)


def _silu(x):
    return x * (1.0 / (1.0 + jnp.exp(-x)))


def _normmm_kernel(x_ref, g_ref, w_ref, o_ref, xn_ref):
    @pl.when(pl.program_id(1) == 0)
    def _():
        x = x_ref[...].astype(F32)
        ms = jnp.mean(x * x, axis=-1, keepdims=True)
        xn_ref[...] = (x * lax.rsqrt(ms + EPS) * g_ref[...]).astype(BF16)

    o_ref[...] = _dot(xn_ref[...], w_ref[...]).astype(o_ref.dtype)


def normmm(x, g, w, out_dtype=BF16, tm=1024, tn=1024):
    M, K = x.shape
    N = w.shape[1]
    tm = _tile(M, tm)
    tn = _tile(N, tn)
    return pl.pallas_call(
        _normmm_kernel,
        grid=(M // tm, N // tn),
        in_specs=[pl.BlockSpec((tm, K), lambda i, j: (i, 0)),
                  pl.BlockSpec((1, K), lambda i, j: (0, 0)),
                  pl.BlockSpec((K, tn), lambda i, j: (0, j),
                               pipeline_mode=pl.Buffered(1) if tn == N else None)],
        out_specs=pl.BlockSpec((tm, tn), lambda i, j: (i, j)),
        out_shape=jax.ShapeDtypeStruct((M, N), out_dtype),
        scratch_shapes=[pltpu.VMEM((tm, K), BF16)],
        compiler_params=_cparams("parallel", "arbitrary"),
        name="normmm",
    )(x, g.reshape(1, K).astype(F32), w)


def _mmres_kernel(a_ref, w_ref, r_ref, o_ref):
    o_ref[...] = r_ref[...] + _dot(a_ref[...].astype(BF16), w_ref[...])


def mm_res(a, w, res, tm=1024, tn=None):
    M, K = a.shape
    N = w.shape[1]
    tm = _tile(M, tm)
    tn = _tile(N, tn or (1024 if K <= 2048 else 512))
    w_mode = pl.Buffered(1) if tn == N else None
    return pl.pallas_call(
        _mmres_kernel,
        grid=(M // tm, N // tn),
        in_specs=[pl.BlockSpec((tm, K), lambda i, j: (i, 0)),
                  pl.BlockSpec((K, tn), lambda i, j: (0, j), pipeline_mode=w_mode),
                  pl.BlockSpec((tm, tn), lambda i, j: (i, j))],
        out_specs=pl.BlockSpec((tm, tn), lambda i, j: (i, j)),
        out_shape=jax.ShapeDtypeStruct((M, N), F32),
        compiler_params=_cparams("parallel", "arbitrary"),
        name="mm_res",
    )(a, w, res)


def _rmsnorm_kernel(x_ref, g_ref, o_ref):
    x = x_ref[...]
    ms = jnp.mean(x * x, axis=-1, keepdims=True)
    o_ref[...] = x * lax.rsqrt(ms + EPS) * g_ref[...]


def rmsnorm(x, g, row0, rows, tm=512):
    K = x.shape[1]
    tm = _tile(math.gcd(row0, rows) if row0 else rows, tm)
    return pl.pallas_call(
        _rmsnorm_kernel,
        grid=(rows // tm,),
        in_specs=[pl.BlockSpec((tm, K), lambda i: (i + row0 // tm, 0)),
                  pl.BlockSpec((1, K), lambda i: (0, 0))],
        out_specs=pl.BlockSpec((tm, K), lambda i: (i, 0)),
        out_shape=jax.ShapeDtypeStruct((rows, K), F32),
        compiler_params=_cparams("parallel"),
        name="final_norm",
    )(x, g.reshape(1, K).astype(F32))


CONV_HALO = 16
CONV_ROWS = 64


def _normmm_conv_kernel(*refs, nseg, width, tm, seq, epilogue, nout):
    xm_ref, xp_ref, xn_ref, g_ref = refs[:4]
    segs = [refs[4 + 3 * s:7 + 3 * s] for s in range(nseg)]
    outs = refs[4 + 3 * nseg:4 + 3 * nseg + nout]
    hn_ref = refs[4 + 3 * nseg + nout]
    exts = refs[5 + 3 * nseg + nout:]
    h = CONV_HALO
    half = width // 2
    row0 = pl.program_id(0) * tm
    at_start = (row0 % seq) == 0
    at_end = ((row0 + tm) % seq) == 0

    @pl.when(pl.program_id(1) == 0)
    def _():
        def nrm(x):
            ms = jnp.mean(x * x, axis=-1, keepdims=True)
            return (x * lax.rsqrt(ms + EPS) * g_ref[...]).astype(BF16)
        hn_ref[0:h, :] = nrm(xp_ref[...])
        hn_ref[h:h + tm, :] = nrm(xm_ref[...])
        hn_ref[h + tm:h + tm + h, :] = nrm(xn_ref[...])

    tn = exts[0].shape[1]
    for (w_ref, _, _), ext in zip(segs, exts):
        ext[...] = _dot(hn_ref[...], w_ref[...])
        ext[0:h, :] = jnp.where(at_start, 0.0, ext[0:h, :])
        ext[h + tm:h + tm + h, :] = jnp.where(at_end, 0.0, ext[h + tm:h + tm + h, :])

    for rb in range(0, tm, CONV_ROWS):
        for lc in range(0, tn, LANES):
            ls = slice(lc, lc + LANES)
            vals = []
            for (_, cw_ref, cb_ref), ext in zip(segs, exts):
                acc = None
                for k in range(width):
                    term = ext[h - half + k + rb:h - half + k + rb + CONV_ROWS, ls] * cw_ref[k:k + 1, ls]
                    acc = term if acc is None else acc + term
                vals.append(acc + cb_ref[:, ls])
            for o, r in zip(outs, epilogue(*vals)):
                o[rb:rb + CONV_ROWS, ls] = r.astype(o.dtype)


def normmm_conv(x, g, w, conv_w, conv_b, seg_cols, width_cols, epilogue, nout, seq, out_dtype=BF16,
                tm=1024, tn=512):
    T, K = x.shape
    width = conv_w.shape[0]
    nseg = len(seg_cols)
    tm = _tile(seq, tm)
    tn = _tile(width_cols, tn)
    assert tm % CONV_ROWS == 0 and tn % LANES == 0
    h = CONV_HALO
    nrb = T // h
    cb = conv_b.reshape(1, -1).astype(F32)
    cw = conv_w.astype(F32)
    in_specs = [pl.BlockSpec((tm, K), lambda i, j: (i, 0)),
                pl.BlockSpec((h, K), lambda i, j: (jnp.maximum(i * (tm // h) - 1, 0), 0)),
                pl.BlockSpec((h, K), lambda i, j: (jnp.minimum((i + 1) * (tm // h), nrb - 1), 0)),
                pl.BlockSpec((1, K), lambda i, j: (0, 0))]
    args = [x, x, x, g.reshape(1, K).astype(F32)]
    for c0 in seg_cols:
        off = c0 // tn
        in_specs += [pl.BlockSpec((K, tn), lambda i, j, off=off: (0, j + off)),
                     pl.BlockSpec((width, tn), lambda i, j, off=off: (0, j + off)),
                     pl.BlockSpec((1, tn), lambda i, j, off=off: (0, j + off))]
        args += [w, cw, cb]
    kern = functools.partial(_normmm_conv_kernel, nseg=nseg, width=width, tm=tm, seq=seq,
                             epilogue=epilogue, nout=nout)
    return pl.pallas_call(
        kern,
        grid=(T // tm, width_cols // tn),
        in_specs=in_specs,
        out_specs=[pl.BlockSpec((tm, tn), lambda i, j: (i, j)) for _ in range(nout)],
        out_shape=[jax.ShapeDtypeStruct((T, width_cols), out_dtype) for _ in range(nout)],
        scratch_shapes=[pltpu.VMEM((tm + 2 * h, K), BF16)]
        + [pltpu.VMEM((tm + 2 * h, tn), F32) for _ in range(nseg)],
        compiler_params=_cparams("parallel", "arbitrary"),
        name="normmm_conv",
    )(*args)


def _epi_silu(c):
    return (_silu(c),)


def _epi_glu(g, up):
    return (_silu(g) * up,)


def _epi_hyena(x0, x1, v):
    return (x0, v * x1)


def _softplus(x):
    return jnp.maximum(x, 0.0) + jnp.log(1.0 + jnp.exp(-jnp.abs(x)))


def _ssd_kernel(*refs, rev, nheads):
    if rev:
        (xs_ref, b_ref, c_ref, dt_ref, bias_ref, alog_ref, e_ref,
         yf_ref, z_ref, gain_ref, o_ref, s_ref, y_ref) = refs
    else:
        (xs_ref, b_ref, c_ref, dt_ref, bias_ref, alog_ref, e_ref,
         dskip_ref, o_ref, s_ref) = refs
        y_ref = o_ref
    Q = SSD_CHUNK
    P = SSD_HEAD_DIM
    hpg = nheads // SSD_GROUPS
    gw = hpg * P
    hoff = nheads if rev else 0

    @pl.when(pl.program_id(1) == 0)
    def _():
        s_ref[...] = jnp.zeros_like(s_ref)

    row = lax.broadcasted_iota(jnp.int32, (Q, Q), 0)
    col = lax.broadcasted_iota(jnp.int32, (Q, Q), 1)
    mask = (col >= row) if rev else (col <= row)
    tri = jnp.where(mask, 1.0, 0.0).astype(BF16)

    dtv = _softplus(dt_ref[...] + bias_ref[...])
    a_row = -jnp.exp(alog_ref[...])
    la = dtv * a_row
    cs = _dot_exact_lhs(tri, la)
    tot = cs[0:1, :] if rev else cs[Q - 1:Q, :]
    cs_t = cs.T
    dt_t = dtv.T
    e = e_ref[...]
    ecs_hi, ecs_lo, _ = _split3(jnp.exp(cs))
    carry_in = _dot(ecs_hi, e) + _dot(ecs_lo, e)
    to_end = _dot((jnp.exp(tot - cs) * dtv).astype(BF16), e)
    dec = _dot_exact_rhs(jnp.broadcast_to(jnp.exp(tot), (8, LANES)), e)[0:1, :]

    xs = xs_ref[...]
    x_state = (xs.astype(F32) * to_end).astype(BF16)
    lane = lax.broadcasted_iota(jnp.int32, (Q, LANES), 1)
    low = lane < P

    for g in range(SSD_GROUPS):
        bg = b_ref[:, g * SSD_STATE:(g + 1) * SSD_STATE]
        cg = c_ref[:, g * SSD_STATE:(g + 1) * SSD_STATE]
        cb = lax.dot_general(cg, bg, (((1,), (1,)), ((), ())), preferred_element_type=F32)
        s_old = s_ref[g]
        y_off = _dot(cg, s_old.astype(BF16)) * carry_in[:, g * gw:(g + 1) * gw]
        s_ref[g] = s_old * dec[:, g * gw:(g + 1) * gw] + lax.dot_general(
            bg, x_state[:, g * gw:(g + 1) * gw], (((0,), (0,)), ((), ())), preferred_element_type=F32)
        for j in range(hpg // 2):
            ws = []
            for hh in range(2):
                hc = hoff + g * hpg + 2 * j + hh
                diff = cs[:, hc:hc + 1] - cs_t[hc:hc + 1, :]
                decay = jnp.exp(jnp.where(mask, diff, NEG_BIG))
                ws.append((cb * decay * dt_t[hc:hc + 1, :]).astype(BF16))
            c0 = g * gw + 2 * j * P
            xp = xs[:, c0:c0 + LANES]
            rhs = jnp.concatenate([jnp.where(low, xp, jnp.zeros_like(xp)),
                                   jnp.where(low, jnp.zeros_like(xp), xp)], axis=0)
            y = _dot(jnp.concatenate(ws, axis=1), rhs) + y_off[:, 2 * j * P:2 * j * P + LANES]
            if not rev:
                y = y + xp.astype(F32) * dskip_ref[:, c0:c0 + LANES]
            y_ref[:, c0:c0 + LANES] = y

    if rev:
        y = y_ref[...] + yf_ref[...]
        gated = y * _silu(z_ref[...].astype(F32))
        ms = jnp.mean(gated * gated, axis=-1, keepdims=True)
        o_ref[...] = (gated * lax.rsqrt(ms + EPS) * gain_ref[...]).astype(o_ref.dtype)


def ssd_scan(xbc, dtraw, bias_row, alog_row, d_row, z_src, gain_row, nb, seq):
    T = xbc.shape[0]
    Q = SSD_CHUNK
    nc = seq // Q
    gn = SSD_GROUPS * SSD_STATE
    hp = xbc.shape[1] - 2 * gn
    nheads = hp // SSD_HEAD_DIM
    hpg = nheads // SSD_GROUPS
    gw = hpg * SSD_HEAD_DIM
    assert hp % gn == 0 and 2 * nheads <= LANES

    def e_mat(off):
        r = np.arange(LANES)[:, None]
        c = np.arange(hp)[None, :]
        return jnp.asarray((r == off + c // SSD_HEAD_DIM).astype(np.float32), dtype=BF16)

    def specs(rev):
        def blk(c):
            return (nc - 1 - c) if rev else c
        return [
            pl.BlockSpec((Q, hp), lambda b, c: (b * nc + blk(c), 0)),
            pl.BlockSpec((Q, gn), lambda b, c: (b * nc + blk(c), hp // gn)),
            pl.BlockSpec((Q, gn), lambda b, c: (b * nc + blk(c), hp // gn + 1)),
            pl.BlockSpec((Q, LANES), lambda b, c: (b * nc + blk(c), 0)),
            pl.BlockSpec((1, LANES), lambda b, c: (0, 0)),
            pl.BlockSpec((1, LANES), lambda b, c: (0, 0)),
            pl.BlockSpec((LANES, hp), lambda b, c: (0, 0)),
        ], (lambda b, c: (b * nc + blk(c), 0))

    in_f, omap_f = specs(False)
    yf = pl.pallas_call(
        functools.partial(_ssd_kernel, rev=False, nheads=nheads),
        grid=(nb, nc),
        in_specs=in_f + [pl.BlockSpec((1, hp), lambda b, c: (0, 0))],
        out_specs=pl.BlockSpec((Q, hp), omap_f),
        out_shape=jax.ShapeDtypeStruct((T, hp), F32),
        scratch_shapes=[pltpu.VMEM((SSD_GROUPS, SSD_STATE, gw), F32)],
        compiler_params=_cparams("parallel", "arbitrary"),
        name="ssd_fwd",
    )(xbc, xbc, xbc, dtraw, bias_row, alog_row, e_mat(0), d_row)
    in_b, omap_b = specs(True)
    return pl.pallas_call(
        functools.partial(_ssd_kernel, rev=True, nheads=nheads),
        grid=(nb, nc),
        in_specs=in_b + [pl.BlockSpec((Q, hp), omap_b),
                         pl.BlockSpec((Q, hp), omap_b),
                         pl.BlockSpec((1, hp), lambda b, c: (0, 0))],
        out_specs=pl.BlockSpec((Q, hp), omap_b),
        out_shape=jax.ShapeDtypeStruct((T, hp), BF16),
        scratch_shapes=[pltpu.VMEM((SSD_GROUPS, SSD_STATE, gw), F32),
                        pltpu.VMEM((Q, hp), F32)],
        compiler_params=_cparams("parallel", "arbitrary"),
        name="ssd_bwd",
    )(xbc, xbc, xbc, dtraw, bias_row, alog_row, e_mat(nheads), yf, z_src, gain_row)


def _qkprep_kernel(q_ref, k_ref, g_ref, cos_ref, sin_ref, o_ref, *, nq, nk):
    hd = ATTN_HEAD_DIM
    cos = cos_ref[...]
    sin = sin_ref[...]
    for h in range(nq + nk):
        src, c0 = (q_ref, h * hd) if h < nq else (k_ref, (h - nq) * hd)
        x = src[:, c0:c0 + hd].astype(F32)
        ms = jnp.mean(x * x, axis=-1, keepdims=True)
        xn = x * lax.rsqrt(ms + EPS) * g_ref[h]
        o_ref[:, h * hd:(h + 1) * hd] = (xn * cos + pltpu.roll(xn, hd // 2, 1) * sin).astype(o_ref.dtype)


def qk_prep(proj, qcol, nq, nk, gains, cos, sin, seq, tq=256):
    T = proj.shape[0]
    hd = ATTN_HEAD_DIM
    tq = _tile(seq, tq)
    spt = seq // tq
    assert qcol % (nq * hd) == 0 and (qcol + nq * hd) % (nk * hd) == 0
    return pl.pallas_call(
        functools.partial(_qkprep_kernel, nq=nq, nk=nk),
        grid=(T // tq,),
        in_specs=[pl.BlockSpec((tq, nq * hd), lambda i: (i, qcol // (nq * hd))),
                  pl.BlockSpec((tq, nk * hd), lambda i: (i, (qcol + nq * hd) // (nk * hd))),
                  pl.BlockSpec((nq + nk, 1, hd), lambda i: (0, 0, 0)),
                  pl.BlockSpec((tq, hd), lambda i: (i % spt, 0)),
                  pl.BlockSpec((tq, hd), lambda i: (i % spt, 0))],
        out_specs=pl.BlockSpec((tq, (nq + nk) * hd), lambda i: (i, 0)),
        out_shape=jax.ShapeDtypeStruct((T, (nq + nk) * hd), BF16),
        compiler_params=_cparams("parallel"),
        name="qk_prep",
    )(proj, proj, gains, cos, sin)


FLASH_ROW_BLOCK = 32


def _flash_kernel(q_ref, k_ref, v_ref, o_ref, qs_ref, va_ref, s0_ref, s1_ref, p_ref, acc_ref,
                  m_ref, al_ref, *, tk, group):
    hd = ATTN_HEAD_DIM
    tq = q_ref.shape[0]
    rows = group * tq
    seq = k_ref.shape[0]
    nk = seq // tk

    @pl.when(pl.program_id(2) == 0)
    def _():
        va_ref[:, 0:hd] = v_ref[...]
        va_ref[:, hd:2 * hd] = jnp.ones((seq, hd), BF16)

    for g in range(group):
        qs_ref[g * tq:(g + 1) * tq, :] = q_ref[:, g * hd:(g + 1) * hd]
    m_ref[...] = jnp.full(m_ref.shape, NEG_BIG, F32)
    acc_ref[...] = jnp.zeros(acc_ref.shape, F32)
    nlc = tk // LANES

    def scores(t, s_ref):
        k0 = pl.multiple_of(t * tk, tk)
        s_ref[...] = lax.dot_general(qs_ref[...], k_ref[pl.ds(k0, tk), :], (((1,), (1,)), ((), ())),
                                     preferred_element_type=F32)

    def update(t, s_ref):
        for r0 in range(0, rows, FLASH_ROW_BLOCK):
            rs = slice(r0, r0 + FLASH_ROW_BLOCK)
            ch = [s_ref[rs, c * LANES:(c + 1) * LANES] for c in range(nlc)]
            mx = ch[0]
            for c in range(1, nlc):
                mx = jnp.maximum(mx, ch[c])
            m_old = m_ref[rs, :]
            m_new = jnp.maximum(m_old, jnp.max(mx, axis=-1, keepdims=True))
            m_ref[rs, :] = m_new
            al_ref[rs, :] = jnp.exp2(m_old - m_new)
            for c in range(nlc):
                p_ref[rs, c * LANES:(c + 1) * LANES] = jnp.exp2(ch[c] - m_new).astype(BF16)
        k0 = pl.multiple_of(t * tk, tk)
        pv = _dot(p_ref[...], va_ref[pl.ds(k0, tk), :])
        al = al_ref[...]
        acc_ref[...] = acc_ref[...] * jnp.concatenate([al, al], axis=1) + pv

    scores(0, s0_ref)

    def body(t2, carry):
        scores(2 * t2 + 1, s1_ref)
        update(2 * t2, s0_ref)
        scores(2 * t2 + 2, s0_ref)
        update(2 * t2 + 1, s1_ref)
        return carry

    lax.fori_loop(0, nk // 2 - 1, body, 0)
    scores(nk - 1, s1_ref)
    update(nk - 2, s0_ref)
    update(nk - 1, s1_ref)
    o = acc_ref[:, 0:hd] / acc_ref[:, hd:2 * hd]
    for g in range(group):
        o_ref[:, g * hd:(g + 1) * hd] = o[g * tq:(g + 1) * tq, :].astype(o_ref.dtype)


def flash_attention(qk, v_src, v_col0, nq_heads, nb, seq, tq=256, tk=1024):
    T = qk.shape[0]
    hd = ATTN_HEAD_DIM
    nkv = ATTN_KV_HEADS
    group = nq_heads // nkv
    tq = _tile(seq, tq)
    tk = _tile(seq // 2, tk)
    nqt = seq // tq
    return pl.pallas_call(
        functools.partial(_flash_kernel, tk=tk, group=group),
        grid=(nb, nkv, nqt),
        in_specs=[pl.BlockSpec((tq, group * hd), lambda b, h, i: (b * nqt + i, h)),
                  pl.BlockSpec((seq, hd), lambda b, h, i: (b, nq_heads + h)),
                  pl.BlockSpec((seq, hd), lambda b, h, i: (b, v_col0 // hd + h))],
        out_specs=pl.BlockSpec((tq, group * hd), lambda b, h, i: (b * nqt + i, h)),
        out_shape=jax.ShapeDtypeStruct((T, nq_heads * hd), BF16),
        scratch_shapes=[pltpu.VMEM((group * tq, hd), BF16),
                        pltpu.VMEM((seq, 2 * hd), BF16),
                        pltpu.VMEM((group * tq, tk), F32),
                        pltpu.VMEM((group * tq, tk), F32),
                        pltpu.VMEM((group * tq, tk), BF16),
                        pltpu.VMEM((group * tq, 2 * hd), F32),
                        pltpu.VMEM((group * tq, LANES), F32),
                        pltpu.VMEM((group * tq, LANES), F32)],
        compiler_params=_cparams("arbitrary", "arbitrary", "arbitrary"),
        name="flash_attn",
    )(qk, qk, v_src)


def _xattn_kernel(q_ref, kv_ref, o_ref, *, heads):
    d = q_ref.shape[1]
    hd = d // heads
    scale = hd ** -0.5
    for h in range(heads):
        q = q_ref[:, h * hd:(h + 1) * hd]
        k = kv_ref[:, h * hd:(h + 1) * hd]
        v = kv_ref[:, d + h * hd:d + (h + 1) * hd]
        s = lax.dot_general(q, k, (((1,), (1,)), ((), ())), preferred_element_type=F32) * scale
        p = jnp.exp(s - jnp.max(s, axis=-1, keepdims=True))
        l = jnp.sum(p, axis=-1, keepdims=True)
        o = _dot(p.astype(BF16), v) / l
        o_ref[:, h * hd:(h + 1) * hd] = o.astype(o_ref.dtype)


def xattn(q, kv, nb, seq, tq=512):
    T, d = q.shape
    n_mem = kv.shape[0] // nb
    tq = _tile(seq, tq)
    nqt = seq // tq
    return pl.pallas_call(
        functools.partial(_xattn_kernel, heads=XA_HEADS),
        grid=(nb, nqt),
        in_specs=[pl.BlockSpec((tq, d), lambda b, i: (b * nqt + i, 0)),
                  pl.BlockSpec((n_mem, 2 * d), lambda b, i: (b, 0))],
        out_specs=pl.BlockSpec((tq, d), lambda b, i: (b * nqt + i, 0)),
        out_shape=jax.ShapeDtypeStruct((T, d), BF16),
        compiler_params=_cparams("parallel", "arbitrary"),
        name="xattn",
    )(q, kv)


def _hyfilter_kernel(z_ref, t_ref, dl_ref, w1_ref, b1_ref, w2_ref, b2_ref, w3_ref, b3_ref,
                     fr_ref, wo_ref, h_ref, sum_ref, *, tl, d):
    i = pl.program_id(0)
    fr = fr_ref[...]
    h = jnp.sin(fr * (_dot_f32(z_ref[...], w1_ref[...]) + b1_ref[...]))
    h = jnp.sin(fr * (_dot_f32(h, w2_ref[...]) + b2_ref[...]))
    h = jnp.sin(fr * (_dot_f32(h, w3_ref[...]) + b3_ref[...]))
    window = jnp.exp(-t_ref[...] * dl_ref[...])
    rows = lax.broadcasted_iota(jnp.int32, (tl, 1), 0) + i * tl

    @pl.when(i == 0)
    def _():
        sum_ref[...] = jnp.zeros_like(sum_ref)

    for part in range(2):
        hp = _dot_f32(h, wo_ref[:, part * d:(part + 1) * d]) * window
        if part == 1:
            hp = jnp.where(rows == 0, 0.0, hp)
        h_ref[:, part * d:(part + 1) * d] = hp.astype(h_ref.dtype)
        sum_ref[:, part * d:(part + 1) * d] += jnp.sum(jnp.abs(hp), axis=0, keepdims=True)


def hyena_filter(z, t_col, deltas, w1, b1, w2, b2, w3, b3, freq, w_out, tl=256):
    L = z.shape[0]
    d2 = w_out.shape[1]
    d = d2 // 2
    fw = w2.shape[0]
    tl = _tile(L, tl)
    full = lambda a: pl.BlockSpec(a.shape, lambda i: (0,) * a.ndim)
    ops = [w1, b1.reshape(1, fw), w2, b2.reshape(1, fw), w3, b3.reshape(1, fw), freq.reshape(1, fw), w_out]
    return pl.pallas_call(
        functools.partial(_hyfilter_kernel, tl=tl, d=d),
        grid=(L // tl,),
        in_specs=[pl.BlockSpec((tl, z.shape[1]), lambda i: (i, 0)),
                  pl.BlockSpec((tl, 1), lambda i: (i, 0)),
                  full(deltas)] + [full(a) for a in ops],
        out_specs=[pl.BlockSpec((tl, d2), lambda i: (i, 0)),
                   pl.BlockSpec((1, d2), lambda i: (0, 0))],
        out_shape=[jax.ShapeDtypeStruct((L, d2), F32),
                   jax.ShapeDtypeStruct((1, d2), F32)],
        compiler_params=_cparams("arbitrary"),
        name="hyena_filter",
    )(z, t_col, deltas, *ops)


DFT_ROWS = 8


def _dft1_kernel(g_ref, u_ref, o_ref):
    kh, rt, tc = u_ref.shape
    u = u_ref[...].reshape(kh * rt, tc).astype(BF16)
    a = _dot(g_ref[...], u).astype(BF16)
    o_ref[...] = pltpu.bitcast(a, jnp.uint32).reshape(o_ref.shape)


def dft_stage1(g1, u4, tc=1024):
    nb, kh, n2, c = u4.shape
    rt = DFT_ROWS
    n1 = g1.shape[0] // (2 * rt)
    tc = _tile(c, tc)
    return pl.pallas_call(
        _dft1_kernel,
        grid=(nb, n2 // rt, c // tc),
        in_specs=[pl.BlockSpec(g1.shape, lambda b, i, j: (0, 0)),
                  pl.BlockSpec((None, kh, rt, tc), lambda b, i, j: (b, 0, i, j))],
        out_specs=pl.BlockSpec((None, n1, rt, tc), lambda b, i, j: (b, 0, i, j)),
        out_shape=jax.ShapeDtypeStruct((nb, n1, n2, c), jnp.uint32),
        compiler_params=_cparams("parallel", "parallel", "arbitrary"),
        name="dft_stage1",
    )(g1, u4)


def _unpack_complex(ref):
    return pltpu.bitcast(ref[...], BF16)


def _spectrum_kernel(af_ref, ab_ref, h_ref, sum_ref, o_ref, *, d_cols):
    n2 = DFT_N2
    hm = h_ref[...]
    xf = _dot(hm, _unpack_complex(af_ref))
    xb = _dot(hm, _unpack_complex(ab_ref))
    inv = 1.0 / (sum_ref[:, 0:d_cols] + sum_ref[:, d_cols:2 * d_cols])
    o_ref[0] = (xf[:n2] + xb[:n2]) * inv
    o_ref[1] = (xf[n2:] - xb[n2:]) * inv


def filter_spectrum(a4, hmat, sums, d):
    n1 = a4.shape[1]
    n2 = DFT_N2
    return pl.pallas_call(
        functools.partial(_spectrum_kernel, d_cols=d),
        grid=(n1,),
        in_specs=[pl.BlockSpec((None, None, n2, d), lambda k: (0, k, 0, 0)),
                  pl.BlockSpec((None, None, n2, d), lambda k: (0, k, 0, 1)),
                  pl.BlockSpec((None, 2 * n2, 2 * n2), lambda k: (k, 0, 0)),
                  pl.BlockSpec((1, 2 * d), lambda k: (0, 0))],
        out_specs=pl.BlockSpec((2, None, n2, d), lambda k: (0, k, 0, 0)),
        out_shape=jax.ShapeDtypeStruct((2, n1, n2, d), F32),
        compiler_params=_cparams("arbitrary"),
        name="filter_spectrum",
    )(a4, a4, hmat, sums)


def _dftmid_kernel(a_ref, h_ref, g_ref, k_ref, o_ref):
    n2 = DFT_N2
    kr, ki = k_ref[0], k_ref[1]
    for b in range(a_ref.shape[0]):
        x = _dot(h_ref[...], pltpu.bitcast(a_ref[b], BF16))
        xr, xi = x[:n2], x[n2:]
        y = jnp.concatenate([xr * kr - xi * ki, xr * ki + xi * kr], axis=0).astype(BF16)
        zz = _dot(g_ref[...], y).astype(BF16)
        o_ref[b] = pltpu.bitcast(zz, jnp.uint32)


def dft_mid(a4, hmat, gmat, kspec):
    nb, n1, n2, c = a4.shape
    return pl.pallas_call(
        _dftmid_kernel,
        grid=(n1,),
        in_specs=[pl.BlockSpec((nb, None, n2, c), lambda k: (0, k, 0, 0)),
                  pl.BlockSpec((None, 2 * n2, 2 * n2), lambda k: (k, 0, 0)),
                  pl.BlockSpec((None, 2 * n2, 2 * n2), lambda k: (k, 0, 0)),
                  pl.BlockSpec((2, None, n2, c), lambda k: (0, k, 0, 0))],
        out_specs=pl.BlockSpec((nb, None, n2, c), lambda k: (0, k, 0, 0)),
        out_shape=jax.ShapeDtypeStruct(a4.shape, jnp.uint32),
        compiler_params=_cparams("parallel"),
        name="dft_mid",
    )(a4, hmat, gmat, kspec)


def _dftout_kernel(g_ref, z_ref, x0_ref, w_ref, skip_ref, o_ref):
    n1, rt, tc = z_ref.shape
    z = pltpu.bitcast(z_ref[...].reshape(n1 * rt, tc), BF16)
    y = _dot(g_ref[...], z).reshape(o_ref.shape)
    o_ref[...] = x0_ref[...] * (y + w_ref[...] * skip_ref[...])


def dft_out(g2, z4, x0, w, skip, tc=1024):
    nb, n1, n2, c = z4.shape
    rt = DFT_ROWS
    kh = g2.shape[0] // rt
    tc = _tile(c, tc)
    tok = pl.BlockSpec((None, kh, rt, tc), lambda b, i, j: (b, 0, i, j))
    return pl.pallas_call(
        _dftout_kernel,
        grid=(nb, n2 // rt, c // tc),
        in_specs=[pl.BlockSpec(g2.shape, lambda b, i, j: (0, 0)),
                  pl.BlockSpec((None, n1, rt, tc), lambda b, i, j: (b, 0, i, j)),
                  tok, tok,
                  pl.BlockSpec((1, 1, tc), lambda b, i, j: (0, 0, j))],
        out_specs=tok,
        out_shape=jax.ShapeDtypeStruct((nb, kh, n2, c), F32),
        compiler_params=_cparams("parallel", "parallel", "arbitrary"),
        name="dft_out",
    )(g2, z4, x0, w, skip.reshape(1, 1, c).astype(F32))


def _rope_tables(seq):
    hd = ATTN_HEAD_DIM
    axis_dim = hd // 2
    t = jnp.arange(seq)
    row = (t // GRID_W).astype(F32)
    col = (t % GRID_W).astype(F32)
    inv_freq = ROPE_THETA ** (-jnp.arange(0, axis_dim, 2, dtype=F32) / axis_dim)
    ang = jnp.concatenate([row[:, None] * inv_freq, col[:, None] * inv_freq], axis=-1)
    c, s = jnp.cos(ang), jnp.sin(ang)
    return jnp.concatenate([c, c], axis=-1), jnp.concatenate([-s, s], axis=-1)


def _hyena_features(seq, d):
    t = jnp.linspace(0.0, 1.0, seq, dtype=F32)
    w = 2.0 * math.pi * jnp.arange(seq, dtype=F32) / seq
    f = jnp.linspace(1e-4, HY_BANDS - 1, HY_BANDS, dtype=F32)
    fw = w[:, None] * f[None, :]
    z = jnp.concatenate([t[:, None], jnp.cos(fw), -jnp.sin(fw)], axis=-1)
    z = jnp.pad(z, ((0, 0), (0, LANES - HY_EMB)))
    deltas = jnp.abs(jnp.linspace(math.log(HY_TARGET) / HY_SLOW_PCT,
                                  math.log(HY_TARGET) / HY_FAST_PCT, d, dtype=F32))
    return z, t[:, None], deltas[None, :]


def _dft_tables(seq):
    n = 2 * seq
    n2 = DFT_N2
    n1 = n // n2

    def cs(phase_int, mod):
        ang = (-2.0 * math.pi / mod) * (phase_int % mod).astype(F32)
        return jnp.cos(ang), jnp.sin(ang)

    k1 = jnp.arange(n1)[:, None]
    m1 = jnp.arange(n1 // 2)[None, :]
    fr, fi = cs(k1 * m1, n1)
    base = jnp.stack([fr, fi], axis=-1)
    eye = jnp.eye(DFT_ROWS, dtype=F32)
    rows = DFT_ROWS
    g1 = jnp.einsum('knp,rs->krpns', base, eye).reshape(n1 * rows * 2, (n1 // 2) * rows).astype(BF16)
    g2 = (jnp.einsum('knp,rs->nrksp', base, eye).reshape((n1 // 2) * rows, n1 * rows * 2) / n).astype(BF16)
    kk = (jnp.arange(n1)[:, None, None] + n1 * jnp.arange(n2)[None, :, None])
    nn = jnp.arange(n2)[None, None, :]
    hr, hi = cs(kk * nn, n)
    hmat = jnp.concatenate([jnp.stack([hr, -hi], axis=-1).reshape(n1, n2, 2 * n2),
                            jnp.stack([hi, hr], axis=-1).reshape(n1, n2, 2 * n2)], axis=1).astype(BF16)
    gr, gi = jnp.swapaxes(hr, 1, 2), -jnp.swapaxes(hi, 1, 2)
    gmat = jnp.stack([jnp.concatenate([gr, -gi], axis=2),
                      jnp.concatenate([gi, gr], axis=2)], axis=2).reshape(n1, 2 * n2, 2 * n2).astype(BF16)
    return g1, g2, hmat, gmat


def _deinterleave(nheads):
    hd = ATTN_HEAD_DIM
    one = np.concatenate([np.arange(0, hd, 2), np.arange(1, hd, 2)])
    return np.concatenate([h * hd + one for h in range(nheads)])


def kernel(x_prompt, x_sample, mem_prompt, mem_sample, norm_mix, norm_xa, norm_mem, norm_ffn, xa_wq, xa_wk, xa_wv, xa_wo, ffn_w_in, ffn_conv_w, ffn_conv_b, ffn_w_out, mix_w_in, mix_w_out, ssd_conv_w, ssd_conv_b, ssd_a_log, ssd_dt_bias, ssd_d, ssd_norm, attn_q_norm, attn_k_norm, hy_w_in, hy_conv_w, hy_conv_b, hy_f_w1, hy_f_b1, hy_f_w2, hy_f_b2, hy_f_w3, hy_f_b3, hy_f_freq, hy_f_w_out, hy_skip, hy_w_out, final_norm):
    nbp, seq, d = x_prompt.shape
    nbs = x_sample.shape[0]
    assert x_sample.shape[1] == seq
    nb = nbp + nbs
    T = nb * seq
    depth = norm_mix.shape[0]
    n_mem = mem_prompt.shape[1]
    d_ff = ffn_w_out.shape[1]

    x = jnp.concatenate([x_prompt, x_sample], axis=0).reshape(T, d)
    mem = jnp.concatenate([mem_prompt, mem_sample], axis=0).reshape(nb * n_mem, d)

    d_ssd = d
    nheads = d_ssd // SSD_HEAD_DIM
    gn = SSD_GROUPS * SSD_STATE
    conv_ch = d_ssd + 2 * gn
    n_att = d // ATTN_HEAD_DIM
    d_kv = ATTN_KV_HEADS * ATTN_HEAD_DIM
    o1 = d_ssd
    o2 = o1 + conv_ch
    o3 = o2 + 2 * nheads
    o4 = o3 + d
    o5 = o4 + d_kv
    qcol = o1
    vcol = o1 + d + d_kv
    cos, sin = _rope_tables(seq)
    perm_q = _deinterleave(n_att)
    perm_k = _deinterleave(ATTN_KV_HEADS)
    perm_h = _deinterleave(1)

    n2 = DFT_N2
    n1 = 2 * seq // n2
    hz, t_col, deltas = _hyena_features(seq, d)
    g1, g2, hmat, gmat = _dft_tables(seq)

    for i in range(depth):
        if i % 2 == 0:
            e = i // 2
            w = mix_w_in[e].astype(BF16)
            w_main = jnp.concatenate([w[:, :o1], w[:, o3:o4][:, perm_q], w[:, o4:o5][:, perm_k], w[:, o5:]],
                                     axis=1)
            w_dt = jnp.pad(w[:, o2:o3], ((0, 0), (0, LANES - 2 * nheads)))
            proj = normmm(x, norm_mix[i], w_main, tm=512, tn=w_main.shape[1])
            dtraw = normmm(x, norm_mix[i], w_dt, out_dtype=F32, tn=LANES)
            (xbc,) = normmm_conv(x, norm_mix[i], w[:, o1:o2], ssd_conv_w[e], ssd_conv_b[e],
                                 [0], conv_ch, _epi_silu, 1, seq)
            pad_row = lambda a: jnp.pad(a.reshape(1, -1).astype(F32), ((0, 0), (0, LANES - 2 * nheads)))
            y_ssd = ssd_scan(xbc, dtraw, pad_row(ssd_dt_bias[e]), pad_row(ssd_a_log[e]),
                             jnp.repeat(ssd_d[e].astype(F32), SSD_HEAD_DIM)[None, :],
                             proj, ssd_norm[e].reshape(1, -1).astype(F32), nb, seq)
            scale = ATTN_HEAD_DIM ** -0.5 * math.log2(math.e)
            gains = jnp.concatenate([jnp.tile(attn_q_norm[e][perm_h][None, :] * scale, (n_att, 1)),
                                     jnp.tile(attn_k_norm[e][perm_h][None, :], (ATTN_KV_HEADS, 1))],
                                    axis=0)[:, None, :].astype(F32)
            qk = qk_prep(proj, qcol, n_att, ATTN_KV_HEADS, gains, cos, sin, seq)
            y_att = flash_attention(qk, proj, vcol, n_att, nb, seq)
            x = mm_res(jnp.concatenate([y_ssd, y_att], axis=1), mix_w_out[e].astype(BF16), x, tm=512, tn=d)
        else:
            o = i // 2
            x0, wv = normmm_conv(x, norm_mix[i], hy_w_in[o].astype(BF16), hy_conv_w[o], hy_conv_b[o],
                                 [0, d, 2 * d], d, _epi_hyena, 2, seq, out_dtype=F32)
            w1 = jnp.pad(hy_f_w1[o], ((0, LANES - HY_EMB), (0, 0)))
            hfb, sums = hyena_filter(hz, t_col, deltas, w1, hy_f_b1[o], hy_f_w2[o], hy_f_b2[o],
                                     hy_f_w3[o], hy_f_b3[o], hy_f_freq[o], hy_f_w_out[o])
            a_f = dft_stage1(g1, hfb.reshape(1, n1 // 2, n2, 2 * d))
            kspec = filter_spectrum(a_f, hmat, sums, d)
            a_u = dft_stage1(g1, wv.reshape(nb, n1 // 2, n2, d))
            zz = dft_mid(a_u, hmat, gmat, kspec)
            yh = dft_out(g2, zz, x0.reshape(nb, n1 // 2, n2, d), wv.reshape(nb, n1 // 2, n2, d), hy_skip[o])
            x = mm_res(yh.reshape(T, d), hy_w_out[o].astype(BF16), x, tm=512, tn=d)
        q = normmm(x, norm_xa[i], xa_wq[i].astype(BF16), tn=d)
        kv = normmm(mem, norm_mem[i], jnp.concatenate([xa_wk[i].astype(BF16), xa_wv[i].astype(BF16)], axis=1))
        x = mm_res(xattn(q, kv, nb, seq), xa_wo[i].astype(BF16), x, tm=512, tn=d)
        (act,) = normmm_conv(x, norm_ffn[i], ffn_w_in[i].astype(BF16), ffn_conv_w[i], ffn_conv_b[i],
                             [0, d_ff], d_ff, _epi_glu, 1, seq)
        x = mm_res(act, ffn_w_out[i].astype(BF16), x, tm=512, tn=d)

    y_prompt = rmsnorm(x, final_norm, 0, nbp * seq).reshape(nbp, seq, d)
    y_sample = rmsnorm(x, final_norm, nbp * seq, nbs * seq).reshape(nbs, seq, d)
    return (y_prompt, y_sample)
```

```python
import functools
import math

import numpy as np
import jax
import jax.numpy as jnp
from jax import lax
from jax.experimental import pallas as pl
from jax.experimental.pallas import tpu as pltpu

F32 = jnp.float32
BF16 = jnp.bfloat16
EPS = 1e-6

GRID_W = 64
XA_HEADS = 4
SSD_HEAD_DIM = 64
SSD_GROUPS = 4
SSD_STATE = 128
SSD_CHUNK = 128
ATTN_HEAD_DIM = 128
ATTN_KV_HEADS = 4
ROPE_THETA = 10000.0
HY_EMB = 33
HY_BANDS = (HY_EMB - 1) // 2
HY_TARGET = 1e-2
HY_FAST_PCT = 0.3
HY_SLOW_PCT = 1.5

LANES = 128
DFT_N2 = 128
VMEM_LIMIT = 56 * 1024 * 1024
NEG_BIG = -1e30


def _cparams(*sem):
    return pltpu.CompilerParams(dimension_semantics=sem, vmem_limit_bytes=VMEM_LIMIT)


def _tile(dim, pref):
    t = min(dim, pref)
    while dim % t:
        t //= 2
    return t


def _split3(x):
    hi = x.astype(BF16)
    r1 = x - hi.astype(F32)
    mid = r1.astype(BF16)
    lo = (r1 - mid.astype(F32)).astype(BF16)
    return hi, mid, lo


def _dot(a, b):
    return jnp.dot(a, b, preferred_element_type=F32)


def _dot_exact_rhs(x, e):
    hi, mid, lo = _split3(x)
    return _dot(hi, e) + _dot(mid, e) + _dot(lo, e)


def _dot_exact_lhs(e, x):
    hi, mid, lo = _split3(x)
    return _dot(e, hi) + _dot(e, mid) + _dot(e, lo)


def _dot_f32(a, b):
    ah, am, _ = _split3(a)
    bh, bm, _ = _split3(b)
    return _dot(ah, bh) + _dot(ah, bm) + _dot(am, bh)


def _silu(x):
    return x * (1.0 / (1.0 + jnp.exp(-x)))


def _normmm_kernel(x_ref, g_ref, w_ref, o_ref, xn_ref):
    @pl.when(pl.program_id(1) == 0)
    def _():
        x = x_ref[...].astype(F32)
        ms = jnp.mean(x * x, axis=-1, keepdims=True)
        xn_ref[...] = (x * lax.rsqrt(ms + EPS) * g_ref[...]).astype(BF16)

    o_ref[...] = _dot(xn_ref[...], w_ref[...]).astype(o_ref.dtype)


def normmm(x, g, w, out_dtype=BF16, tm=1024, tn=1024):
    M, K = x.shape
    N = w.shape[1]
    tm = _tile(M, tm)
    tn = _tile(N, tn)
    return pl.pallas_call(
        _normmm_kernel,
        grid=(M // tm, N // tn),
        in_specs=[pl.BlockSpec((tm, K), lambda i, j: (i, 0)),
                  pl.BlockSpec((1, K), lambda i, j: (0, 0)),
                  pl.BlockSpec((K, tn), lambda i, j: (0, j),
                               pipeline_mode=pl.Buffered(1) if tn == N else None)],
        out_specs=pl.BlockSpec((tm, tn), lambda i, j: (i, j)),
        out_shape=jax.ShapeDtypeStruct((M, N), out_dtype),
        scratch_shapes=[pltpu.VMEM((tm, K), BF16)],
        compiler_params=_cparams("parallel", "arbitrary"),
        name="normmm",
    )(x, g.reshape(1, K).astype(F32), w)


def _mmres_kernel(a_ref, w_ref, r_ref, o_ref):
    o_ref[...] = r_ref[...] + _dot(a_ref[...].astype(BF16), w_ref[...])


def mm_res(a, w, res, tm=1024, tn=None):
    M, K = a.shape
    N = w.shape[1]
    tm = _tile(M, tm)
    tn = _tile(N, tn or (1024 if K <= 2048 else 512))
    w_mode = pl.Buffered(1) if tn == N else None
    return pl.pallas_call(
        _mmres_kernel,
        grid=(M // tm, N // tn),
        in_specs=[pl.BlockSpec((tm, K), lambda i, j: (i, 0)),
                  pl.BlockSpec((K, tn), lambda i, j: (0, j), pipeline_mode=w_mode),
                  pl.BlockSpec((tm, tn), lambda i, j: (i, j))],
        out_specs=pl.BlockSpec((tm, tn), lambda i, j: (i, j)),
        out_shape=jax.ShapeDtypeStruct((M, N), F32),
        compiler_params=_cparams("parallel", "arbitrary"),
        name="mm_res",
    )(a, w, res)


def _rmsnorm_kernel(x_ref, g_ref, o_ref):
    x = x_ref[...]
    ms = jnp.mean(x * x, axis=-1, keepdims=True)
    o_ref[...] = x * lax.rsqrt(ms + EPS) * g_ref[...]


def rmsnorm(x, g, row0, rows, tm=512):
    K = x.shape[1]
    tm = _tile(math.gcd(row0, rows) if row0 else rows, tm)
    return pl.pallas_call(
        _rmsnorm_kernel,
        grid=(rows // tm,),
        in_specs=[pl.BlockSpec((tm, K), lambda i: (i + row0 // tm, 0)),
                  pl.BlockSpec((1, K), lambda i: (0, 0))],
        out_specs=pl.BlockSpec((tm, K), lambda i: (i, 0)),
        out_shape=jax.ShapeDtypeStruct((rows, K), F32),
        compiler_params=_cparams("parallel"),
        name="final_norm",
    )(x, g.reshape(1, K).astype(F32))


CONV_HALO = 16
CONV_ROWS = 64


def _normmm_conv_kernel(*refs, nseg, width, tm, seq, epilogue, nout):
    xm_ref, xp_ref, xn_ref, g_ref = refs[:4]
    segs = [refs[4 + 3 * s:7 + 3 * s] for s in range(nseg)]
    outs = refs[4 + 3 * nseg:4 + 3 * nseg + nout]
    hn_ref = refs[4 + 3 * nseg + nout]
    exts = refs[5 + 3 * nseg + nout:]
    h = CONV_HALO
    half = width // 2
    row0 = pl.program_id(0) * tm
    at_start = (row0 % seq) == 0
    at_end = ((row0 + tm) % seq) == 0

    @pl.when(pl.program_id(1) == 0)
    def _():
        def nrm(x):
            ms = jnp.mean(x * x, axis=-1, keepdims=True)
            return (x * lax.rsqrt(ms + EPS) * g_ref[...]).astype(BF16)
        hn_ref[0:h, :] = nrm(xp_ref[...])
        hn_ref[h:h + tm, :] = nrm(xm_ref[...])
        hn_ref[h + tm:h + tm + h, :] = nrm(xn_ref[...])

    tn = exts[0].shape[1]
    for (w_ref, _, _), ext in zip(segs, exts):
        ext[...] = _dot(hn_ref[...], w_ref[...])
        ext[0:h, :] = jnp.where(at_start, 0.0, ext[0:h, :])
        ext[h + tm:h + tm + h, :] = jnp.where(at_end, 0.0, ext[h + tm:h + tm + h, :])

    for rb in range(0, tm, CONV_ROWS):
        for lc in range(0, tn, LANES):
            ls = slice(lc, lc + LANES)
            vals = []
            for (_, cw_ref, cb_ref), ext in zip(segs, exts):
                acc = None
                for k in range(width):
                    term = ext[h - half + k + rb:h - half + k + rb + CONV_ROWS, ls] * cw_ref[k:k + 1, ls]
                    acc = term if acc is None else acc + term
                vals.append(acc + cb_ref[:, ls])
            for o, r in zip(outs, epilogue(*vals)):
                o[rb:rb + CONV_ROWS, ls] = r.astype(o.dtype)


def normmm_conv(x, g, w, conv_w, conv_b, seg_cols, width_cols, epilogue, nout, seq, out_dtype=BF16,
                tm=1024, tn=512):
    T, K = x.shape
    width = conv_w.shape[0]
    nseg = len(seg_cols)
    tm = _tile(seq, tm)
    tn = _tile(width_cols, tn)
    assert tm % CONV_ROWS == 0 and tn % LANES == 0
    h = CONV_HALO
    nrb = T // h
    cb = conv_b.reshape(1, -1).astype(F32)
    cw = conv_w.astype(F32)
    in_specs = [pl.BlockSpec((tm, K), lambda i, j: (i, 0)),
                pl.BlockSpec((h, K), lambda i, j: (jnp.maximum(i * (tm // h) - 1, 0), 0)),
                pl.BlockSpec((h, K), lambda i, j: (jnp.minimum((i + 1) * (tm // h), nrb - 1), 0)),
                pl.BlockSpec((1, K), lambda i, j: (0, 0))]
    args = [x, x, x, g.reshape(1, K).astype(F32)]
    for c0 in seg_cols:
        off = c0 // tn
        in_specs += [pl.BlockSpec((K, tn), lambda i, j, off=off: (0, j + off)),
                     pl.BlockSpec((width, tn), lambda i, j, off=off: (0, j + off)),
                     pl.BlockSpec((1, tn), lambda i, j, off=off: (0, j + off))]
        args += [w, cw, cb]
    kern = functools.partial(_normmm_conv_kernel, nseg=nseg, width=width, tm=tm, seq=seq,
                             epilogue=epilogue, nout=nout)
    return pl.pallas_call(
        kern,
        grid=(T // tm, width_cols // tn),
        in_specs=in_specs,
        out_specs=[pl.BlockSpec((tm, tn), lambda i, j: (i, j)) for _ in range(nout)],
        out_shape=[jax.ShapeDtypeStruct((T, width_cols), out_dtype) for _ in range(nout)],
        scratch_shapes=[pltpu.VMEM((tm + 2 * h, K), BF16)]
        + [pltpu.VMEM((tm + 2 * h, tn), F32) for _ in range(nseg)],
        compiler_params=_cparams("parallel", "arbitrary"),
        name="normmm_conv",
    )(*args)


def _epi_silu(c):
    return (_silu(c),)


def _epi_glu(g, up):
    return (_silu(g) * up,)


def _epi_hyena(x0, x1, v):
    return (x0, v * x1)


def _softplus(x):
    return jnp.maximum(x, 0.0) + jnp.log(1.0 + jnp.exp(-jnp.abs(x)))


def _ssd_kernel(*refs, rev, nheads):
    if rev:
        (xs_ref, b_ref, c_ref, dt_ref, bias_ref, alog_ref, e_ref,
         yf_ref, z_ref, gain_ref, o_ref, s_ref, y_ref) = refs
    else:
        (xs_ref, b_ref, c_ref, dt_ref, bias_ref, alog_ref, e_ref,
         dskip_ref, o_ref, s_ref) = refs
        y_ref = o_ref
    Q = SSD_CHUNK
    P = SSD_HEAD_DIM
    hpg = nheads // SSD_GROUPS
    gw = hpg * P
    hoff = nheads if rev else 0

    @pl.when(pl.program_id(1) == 0)
    def _():
        s_ref[...] = jnp.zeros_like(s_ref)

    row = lax.broadcasted_iota(jnp.int32, (Q, Q), 0)
    col = lax.broadcasted_iota(jnp.int32, (Q, Q), 1)
    mask = (col >= row) if rev else (col <= row)
    tri = jnp.where(mask, 1.0, 0.0).astype(BF16)

    dtv = _softplus(dt_ref[...] + bias_ref[...])
    a_row = -jnp.exp(alog_ref[...])
    la = dtv * a_row
    cs = _dot_exact_lhs(tri, la)
    tot = cs[0:1, :] if rev else cs[Q - 1:Q, :]
    cs_t = cs.T
    dt_t = dtv.T
    e = e_ref[...]
    ecs_hi, ecs_lo, _ = _split3(jnp.exp(cs))
    carry_in = _dot(ecs_hi, e) + _dot(ecs_lo, e)
    to_end = _dot((jnp.exp(tot - cs) * dtv).astype(BF16), e)
    dec = _dot_exact_rhs(jnp.broadcast_to(jnp.exp(tot), (8, LANES)), e)[0:1, :]

    xs = xs_ref[...]
    x_state = (xs.astype(F32) * to_end).astype(BF16)
    lane = lax.broadcasted_iota(jnp.int32, (Q, LANES), 1)
    low = lane < P

    for g in range(SSD_GROUPS):
        bg = b_ref[:, g * SSD_STATE:(g + 1) * SSD_STATE]
        cg = c_ref[:, g * SSD_STATE:(g + 1) * SSD_STATE]
        cb = lax.dot_general(cg, bg, (((1,), (1,)), ((), ())), preferred_element_type=F32)
        s_old = s_ref[g]
        y_off = _dot(cg, s_old.astype(BF16)) * carry_in[:, g * gw:(g + 1) * gw]
        s_ref[g] = s_old * dec[:, g * gw:(g + 1) * gw] + lax.dot_general(
            bg, x_state[:, g * gw:(g + 1) * gw], (((0,), (0,)), ((), ())), preferred_element_type=F32)
        for j in range(hpg // 2):
            ws = []
            for hh in range(2):
                hc = hoff + g * hpg + 2 * j + hh
                diff = cs[:, hc:hc + 1] - cs_t[hc:hc + 1, :]
                decay = jnp.exp(jnp.where(mask, diff, NEG_BIG))
                ws.append((cb * decay * dt_t[hc:hc + 1, :]).astype(BF16))
            c0 = g * gw + 2 * j * P
            xp = xs[:, c0:c0 + LANES]
            rhs = jnp.concatenate([jnp.where(low, xp, jnp.zeros_like(xp)),
                                   jnp.where(low, jnp.zeros_like(xp), xp)], axis=0)
            y = _dot(jnp.concatenate(ws, axis=1), rhs) + y_off[:, 2 * j * P:2 * j * P + LANES]
            if not rev:
                y = y + xp.astype(F32) * dskip_ref[:, c0:c0 + LANES]
            y_ref[:, c0:c0 + LANES] = y

    if rev:
        y = y_ref[...] + yf_ref[...]
        gated = y * _silu(z_ref[...].astype(F32))
        ms = jnp.mean(gated * gated, axis=-1, keepdims=True)
        o_ref[...] = (gated * lax.rsqrt(ms + EPS) * gain_ref[...]).astype(o_ref.dtype)


def ssd_scan(xbc, dtraw, bias_row, alog_row, d_row, z_src, gain_row, nb, seq):
    T = xbc.shape[0]
    Q = SSD_CHUNK
    nc = seq // Q
    gn = SSD_GROUPS * SSD_STATE
    hp = xbc.shape[1] - 2 * gn
    nheads = hp // SSD_HEAD_DIM
    hpg = nheads // SSD_GROUPS
    gw = hpg * SSD_HEAD_DIM
    assert hp % gn == 0 and 2 * nheads <= LANES

    def e_mat(off):
        r = np.arange(LANES)[:, None]
        c = np.arange(hp)[None, :]
        return jnp.asarray((r == off + c // SSD_HEAD_DIM).astype(np.float32), dtype=BF16)

    def specs(rev):
        def blk(c):
            return (nc - 1 - c) if rev else c
        return [
            pl.BlockSpec((Q, hp), lambda b, c: (b * nc + blk(c), 0)),
            pl.BlockSpec((Q, gn), lambda b, c: (b * nc + blk(c), hp // gn)),
            pl.BlockSpec((Q, gn), lambda b, c: (b * nc + blk(c), hp // gn + 1)),
            pl.BlockSpec((Q, LANES), lambda b, c: (b * nc + blk(c), 0)),
            pl.BlockSpec((1, LANES), lambda b, c: (0, 0)),
            pl.BlockSpec((1, LANES), lambda b, c: (0, 0)),
            pl.BlockSpec((LANES, hp), lambda b, c: (0, 0)),
        ], (lambda b, c: (b * nc + blk(c), 0))

    in_f, omap_f = specs(False)
    yf = pl.pallas_call(
        functools.partial(_ssd_kernel, rev=False, nheads=nheads),
        grid=(nb, nc),
        in_specs=in_f + [pl.BlockSpec((1, hp), lambda b, c: (0, 0))],
        out_specs=pl.BlockSpec((Q, hp), omap_f),
        out_shape=jax.ShapeDtypeStruct((T, hp), F32),
        scratch_shapes=[pltpu.VMEM((SSD_GROUPS, SSD_STATE, gw), F32)],
        compiler_params=_cparams("parallel", "arbitrary"),
        name="ssd_fwd",
    )(xbc, xbc, xbc, dtraw, bias_row, alog_row, e_mat(0), d_row)
    in_b, omap_b = specs(True)
    return pl.pallas_call(
        functools.partial(_ssd_kernel, rev=True, nheads=nheads),
        grid=(nb, nc),
        in_specs=in_b + [pl.BlockSpec((Q, hp), omap_b),
                         pl.BlockSpec((Q, hp), omap_b),
                         pl.BlockSpec((1, hp), lambda b, c: (0, 0))],
        out_specs=pl.BlockSpec((Q, hp), omap_b),
        out_shape=jax.ShapeDtypeStruct((T, hp), BF16),
        scratch_shapes=[pltpu.VMEM((SSD_GROUPS, SSD_STATE, gw), F32),
                        pltpu.VMEM((Q, hp), F32)],
        compiler_params=_cparams("parallel", "arbitrary"),
        name="ssd_bwd",
    )(xbc, xbc, xbc, dtraw, bias_row, alog_row, e_mat(nheads), yf, z_src, gain_row)


def _qkprep_kernel(q_ref, k_ref, g_ref, cos_ref, sin_ref, o_ref, *, nq, nk):
    hd = ATTN_HEAD_DIM
    cos = cos_ref[...]
    sin = sin_ref[...]
    for h in range(nq + nk):
        src, c0 = (q_ref, h * hd) if h < nq else (k_ref, (h - nq) * hd)
        x = src[:, c0:c0 + hd].astype(F32)
        ms = jnp.mean(x * x, axis=-1, keepdims=True)
        xn = x * lax.rsqrt(ms + EPS) * g_ref[h]
        o_ref[:, h * hd:(h + 1) * hd] = (xn * cos + pltpu.roll(xn, hd // 2, 1) * sin).astype(o_ref.dtype)


def qk_prep(proj, qcol, nq, nk, gains, cos, sin, seq, tq=512):
    T = proj.shape[0]
    hd = ATTN_HEAD_DIM
    tq = _tile(seq, tq)
    spt = seq // tq
    assert qcol % (nq * hd) == 0 and (qcol + nq * hd) % (nk * hd) == 0
    return pl.pallas_call(
        functools.partial(_qkprep_kernel, nq=nq, nk=nk),
        grid=(T // tq,),
        in_specs=[pl.BlockSpec((tq, nq * hd), lambda i: (i, qcol // (nq * hd))),
                  pl.BlockSpec((tq, nk * hd), lambda i: (i, (qcol + nq * hd) // (nk * hd))),
                  pl.BlockSpec((nq + nk, 1, hd), lambda i: (0, 0, 0)),
                  pl.BlockSpec((tq, hd), lambda i: (i % spt, 0)),
                  pl.BlockSpec((tq, hd), lambda i: (i % spt, 0))],
        out_specs=pl.BlockSpec((tq, (nq + nk) * hd), lambda i: (i, 0)),
        out_shape=jax.ShapeDtypeStruct((T, (nq + nk) * hd), BF16),
        compiler_params=_cparams("parallel"),
        name="qk_prep",
    )(proj, proj, gains, cos, sin)


FLASH_ROW_BLOCK = 32


def _flash_kernel(q_ref, k_ref, v_ref, o_ref, qs_ref, va_ref, s0_ref, s1_ref, p_ref, acc_ref,
                  m_ref, al_ref, *, tk, group):
    hd = ATTN_HEAD_DIM
    tq = q_ref.shape[0]
    rows = group * tq
    seq = k_ref.shape[0]
    nk = seq // tk

    @pl.when(pl.program_id(2) == 0)
    def _():
        va_ref[:, 0:hd] = v_ref[...]
        va_ref[:, hd:2 * hd] = jnp.ones((seq, hd), BF16)

    for g in range(group):
        qs_ref[g * tq:(g + 1) * tq, :] = q_ref[:, g * hd:(g + 1) * hd]
    m_ref[...] = jnp.full(m_ref.shape, NEG_BIG, F32)
    acc_ref[...] = jnp.zeros(acc_ref.shape, F32)
    nlc = tk // LANES

    def scores(t, s_ref):
        k0 = pl.multiple_of(t * tk, tk)
        s_ref[...] = lax.dot_general(qs_ref[...], k_ref[pl.ds(k0, tk), :], (((1,), (1,)), ((), ())),
                                     preferred_element_type=F32)

    def update(t, s_ref):
        for r0 in range(0, rows, FLASH_ROW_BLOCK):
            rs = slice(r0, r0 + FLASH_ROW_BLOCK)
            ch = [s_ref[rs, c * LANES:(c + 1) * LANES] for c in range(nlc)]
            mx = ch[0]
            for c in range(1, nlc):
                mx = jnp.maximum(mx, ch[c])
            m_old = m_ref[rs, :]
            m_new = jnp.maximum(m_old, jnp.max(mx, axis=-1, keepdims=True))
            m_ref[rs, :] = m_new
            al_ref[rs, :] = jnp.exp2(m_old - m_new)
            for c in range(nlc):
                p_ref[rs, c * LANES:(c + 1) * LANES] = jnp.exp2(ch[c] - m_new).astype(BF16)
        k0 = pl.multiple_of(t * tk, tk)
        pv = _dot(p_ref[...], va_ref[pl.ds(k0, tk), :])
        al = al_ref[...]
        acc_ref[...] = acc_ref[...] * jnp.concatenate([al, al], axis=1) + pv

    scores(0, s0_ref)

    def body(t2, carry):
        scores(2 * t2 + 1, s1_ref)
        update(2 * t2, s0_ref)
        scores(2 * t2 + 2, s0_ref)
        update(2 * t2 + 1, s1_ref)
        return carry

    lax.fori_loop(0, nk // 2 - 1, body, 0)
    scores(nk - 1, s1_ref)
    update(nk - 2, s0_ref)
    update(nk - 1, s1_ref)
    o = acc_ref[:, 0:hd] / acc_ref[:, hd:2 * hd]
    for g in range(group):
        o_ref[:, g * hd:(g + 1) * hd] = o[g * tq:(g + 1) * tq, :].astype(o_ref.dtype)


def flash_attention(qk, v_src, v_col0, nq_heads, nb, seq, tq=256, tk=1024):
    T = qk.shape[0]
    hd = ATTN_HEAD_DIM
    nkv = ATTN_KV_HEADS
    group = nq_heads // nkv
    tq = _tile(seq, tq)
    tk = _tile(seq // 2, tk)
    nqt = seq // tq
    return pl.pallas_call(
        functools.partial(_flash_kernel, tk=tk, group=group),
        grid=(nb, nkv, nqt),
        in_specs=[pl.BlockSpec((tq, group * hd), lambda b, h, i: (b * nqt + i, h)),
                  pl.BlockSpec((seq, hd), lambda b, h, i: (b, nq_heads + h)),
                  pl.BlockSpec((seq, hd), lambda b, h, i: (b, v_col0 // hd + h))],
        out_specs=pl.BlockSpec((tq, group * hd), lambda b, h, i: (b * nqt + i, h)),
        out_shape=jax.ShapeDtypeStruct((T, nq_heads * hd), BF16),
        scratch_shapes=[pltpu.VMEM((group * tq, hd), BF16),
                        pltpu.VMEM((seq, 2 * hd), BF16),
                        pltpu.VMEM((group * tq, tk), F32),
                        pltpu.VMEM((group * tq, tk), F32),
                        pltpu.VMEM((group * tq, tk), BF16),
                        pltpu.VMEM((group * tq, 2 * hd), F32),
                        pltpu.VMEM((group * tq, LANES), F32),
                        pltpu.VMEM((group * tq, LANES), F32)],
        compiler_params=_cparams("arbitrary", "arbitrary", "arbitrary"),
        name="flash_attn",
    )(qk, qk, v_src)


def _xattn_kernel(q_ref, kv_ref, o_ref, *, heads):
    d = q_ref.shape[1]
    hd = d // heads
    scale = hd ** -0.5
    for h in range(heads):
        q = q_ref[:, h * hd:(h + 1) * hd]
        k = kv_ref[:, h * hd:(h + 1) * hd]
        v = kv_ref[:, d + h * hd:d + (h + 1) * hd]
        s = lax.dot_general(q, k, (((1,), (1,)), ((), ())), preferred_element_type=F32) * scale
        p = jnp.exp(s - jnp.max(s, axis=-1, keepdims=True))
        l = jnp.sum(p, axis=-1, keepdims=True)
        o = _dot(p.astype(BF16), v) / l
        o_ref[:, h * hd:(h + 1) * hd] = o.astype(o_ref.dtype)


def xattn(q, kv, nb, seq, tq=512):
    T, d = q.shape
    n_mem = kv.shape[0] // nb
    tq = _tile(seq, tq)
    nqt = seq // tq
    return pl.pallas_call(
        functools.partial(_xattn_kernel, heads=XA_HEADS),
        grid=(nb, nqt),
        in_specs=[pl.BlockSpec((tq, d), lambda b, i: (b * nqt + i, 0)),
                  pl.BlockSpec((n_mem, 2 * d), lambda b, i: (b, 0))],
        out_specs=pl.BlockSpec((tq, d), lambda b, i: (b * nqt + i, 0)),
        out_shape=jax.ShapeDtypeStruct((T, d), BF16),
        compiler_params=_cparams("parallel", "arbitrary"),
        name="xattn",
    )(q, kv)


def _hyfilter_kernel(z_ref, t_ref, dl_ref, w1_ref, b1_ref, w2_ref, b2_ref, w3_ref, b3_ref,
                     fr_ref, wo_ref, h_ref, sum_ref, *, tl, d):
    i = pl.program_id(0)
    fr = fr_ref[...]
    h = jnp.sin(fr * (_dot_f32(z_ref[...], w1_ref[...]) + b1_ref[...]))
    h = jnp.sin(fr * (_dot_f32(h, w2_ref[...]) + b2_ref[...]))
    h = jnp.sin(fr * (_dot_f32(h, w3_ref[...]) + b3_ref[...]))
    window = jnp.exp(-t_ref[...] * dl_ref[...])
    rows = lax.broadcasted_iota(jnp.int32, (tl, 1), 0) + i * tl

    @pl.when(i == 0)
    def _():
        sum_ref[...] = jnp.zeros_like(sum_ref)

    for part in range(2):
        hp = _dot_f32(h, wo_ref[:, part * d:(part + 1) * d]) * window
        if part == 1:
            hp = jnp.where(rows == 0, 0.0, hp)
        h_ref[:, part * d:(part + 1) * d] = hp.astype(h_ref.dtype)
        sum_ref[:, part * d:(part + 1) * d] += jnp.sum(jnp.abs(hp), axis=0, keepdims=True)


def hyena_filter(z, t_col, deltas, w1, b1, w2, b2, w3, b3, freq, w_out, tl=256):
    L = z.shape[0]
    d2 = w_out.shape[1]
    d = d2 // 2
    fw = w2.shape[0]
    tl = _tile(L, tl)
    full = lambda a: pl.BlockSpec(a.shape, lambda i: (0,) * a.ndim)
    ops = [w1, b1.reshape(1, fw), w2, b2.reshape(1, fw), w3, b3.reshape(1, fw), freq.reshape(1, fw), w_out]
    return pl.pallas_call(
        functools.partial(_hyfilter_kernel, tl=tl, d=d),
        grid=(L // tl,),
        in_specs=[pl.BlockSpec((tl, z.shape[1]), lambda i: (i, 0)),
                  pl.BlockSpec((tl, 1), lambda i: (i, 0)),
                  full(deltas)] + [full(a) for a in ops],
        out_specs=[pl.BlockSpec((tl, d2), lambda i: (i, 0)),
                   pl.BlockSpec((1, d2), lambda i: (0, 0))],
        out_shape=[jax.ShapeDtypeStruct((L, d2), F32),
                   jax.ShapeDtypeStruct((1, d2), F32)],
        compiler_params=_cparams("arbitrary"),
        name="hyena_filter",
    )(z, t_col, deltas, *ops)


DFT_ROWS = 8


def _dft1_kernel(g_ref, u_ref, o_ref):
    kh, rt, tc = u_ref.shape
    u = u_ref[...].reshape(kh * rt, tc).astype(BF16)
    a = _dot(g_ref[...], u).astype(BF16)
    o_ref[...] = pltpu.bitcast(a, jnp.uint32).reshape(o_ref.shape)


def dft_stage1(g1, u4, tc=2048):
    nb, kh, n2, c = u4.shape
    rt = DFT_ROWS
    n1 = g1.shape[0] // (2 * rt)
    tc = _tile(c, tc)
    return pl.pallas_call(
        _dft1_kernel,
        grid=(nb, n2 // rt, c // tc),
        in_specs=[pl.BlockSpec(g1.shape, lambda b, i, j: (0, 0)),
                  pl.BlockSpec((None, kh, rt, tc), lambda b, i, j: (b, 0, i, j))],
        out_specs=pl.BlockSpec((None, n1, rt, tc), lambda b, i, j: (b, 0, i, j)),
        out_shape=jax.ShapeDtypeStruct((nb, n1, n2, c), jnp.uint32),
        compiler_params=_cparams("parallel", "parallel", "arbitrary"),
        name="dft_stage1",
    )(g1, u4)


def _unpack_complex(ref):
    return pltpu.bitcast(ref[...], BF16)


def _spectrum_kernel(af_ref, ab_ref, h_ref, sum_ref, o_ref, *, d_cols):
    n2 = DFT_N2
    hm = h_ref[...]
    xf = _dot(hm, _unpack_complex(af_ref))
    xb = _dot(hm, _unpack_complex(ab_ref))
    inv = 1.0 / (sum_ref[:, 0:d_cols] + sum_ref[:, d_cols:2 * d_cols])
    o_ref[0] = (xf[:n2] + xb[:n2]) * inv
    o_ref[1] = (xf[n2:] - xb[n2:]) * inv


def filter_spectrum(a4, hmat, sums, d):
    n1 = a4.shape[1]
    n2 = DFT_N2
    return pl.pallas_call(
        functools.partial(_spectrum_kernel, d_cols=d),
        grid=(n1,),
        in_specs=[pl.BlockSpec((None, None, n2, d), lambda k: (0, k, 0, 0)),
                  pl.BlockSpec((None, None, n2, d), lambda k: (0, k, 0, 1)),
                  pl.BlockSpec((None, 2 * n2, 2 * n2), lambda k: (k, 0, 0)),
                  pl.BlockSpec((1, 2 * d), lambda k: (0, 0))],
        out_specs=pl.BlockSpec((2, None, n2, d), lambda k: (0, k, 0, 0)),
        out_shape=jax.ShapeDtypeStruct((2, n1, n2, d), F32),
        compiler_params=_cparams("arbitrary"),
        name="filter_spectrum",
    )(a4, a4, hmat, sums)


def _dftmid_kernel(a_ref, h_ref, g_ref, k_ref, o_ref):
    n2 = DFT_N2
    kr, ki = k_ref[0], k_ref[1]
    for b in range(a_ref.shape[0]):
        x = _dot(h_ref[...], pltpu.bitcast(a_ref[b], BF16))
        xr, xi = x[:n2], x[n2:]
        y = jnp.concatenate([xr * kr - xi * ki, xr * ki + xi * kr], axis=0).astype(BF16)
        zz = _dot(g_ref[...], y).astype(BF16)
        o_ref[b] = pltpu.bitcast(zz, jnp.uint32)


def dft_mid(a4, hmat, gmat, kspec):
    nb, n1, n2, c = a4.shape
    return pl.pallas_call(
        _dftmid_kernel,
        grid=(n1,),
        in_specs=[pl.BlockSpec((nb, None, n2, c), lambda k: (0, k, 0, 0)),
                  pl.BlockSpec((None, 2 * n2, 2 * n2), lambda k: (k, 0, 0)),
                  pl.BlockSpec((None, 2 * n2, 2 * n2), lambda k: (k, 0, 0)),
                  pl.BlockSpec((2, None, n2, c), lambda k: (0, k, 0, 0))],
        out_specs=pl.BlockSpec((nb, None, n2, c), lambda k: (0, k, 0, 0)),
        out_shape=jax.ShapeDtypeStruct(a4.shape, jnp.uint32),
        compiler_params=_cparams("parallel"),
        name="dft_mid",
    )(a4, hmat, gmat, kspec)


def _dftout_kernel(g_ref, z_ref, x0_ref, w_ref, skip_ref, o_ref):
    n1, rt, tc = z_ref.shape
    z = pltpu.bitcast(z_ref[...].reshape(n1 * rt, tc), BF16)
    y = _dot(g_ref[...], z).reshape(o_ref.shape)
    o_ref[...] = x0_ref[...] * (y + w_ref[...] * skip_ref[...])


def dft_out(g2, z4, x0, w, skip, tc=2048):
    nb, n1, n2, c = z4.shape
    rt = DFT_ROWS
    kh = g2.shape[0] // rt
    tc = _tile(c, tc)
    tok = pl.BlockSpec((None, kh, rt, tc), lambda b, i, j: (b, 0, i, j))
    return pl.pallas_call(
        _dftout_kernel,
        grid=(nb, n2 // rt, c // tc),
        in_specs=[pl.BlockSpec(g2.shape, lambda b, i, j: (0, 0)),
                  pl.BlockSpec((None, n1, rt, tc), lambda b, i, j: (b, 0, i, j)),
                  tok, tok,
                  pl.BlockSpec((1, 1, tc), lambda b, i, j: (0, 0, j))],
        out_specs=tok,
        out_shape=jax.ShapeDtypeStruct((nb, kh, n2, c), F32),
        compiler_params=_cparams("parallel", "parallel", "arbitrary"),
        name="dft_out",
    )(g2, z4, x0, w, skip.reshape(1, 1, c).astype(F32))


def _rope_tables(seq):
    hd = ATTN_HEAD_DIM
    axis_dim = hd // 2
    t = jnp.arange(seq)
    row = (t // GRID_W).astype(F32)
    col = (t % GRID_W).astype(F32)
    inv_freq = ROPE_THETA ** (-jnp.arange(0, axis_dim, 2, dtype=F32) / axis_dim)
    ang = jnp.concatenate([row[:, None] * inv_freq, col[:, None] * inv_freq], axis=-1)
    c, s = jnp.cos(ang), jnp.sin(ang)
    return jnp.concatenate([c, c], axis=-1), jnp.concatenate([-s, s], axis=-1)


def _hyena_features(seq, d):
    t = jnp.linspace(0.0, 1.0, seq, dtype=F32)
    w = 2.0 * math.pi * jnp.arange(seq, dtype=F32) / seq
    f = jnp.linspace(1e-4, HY_BANDS - 1, HY_BANDS, dtype=F32)
    fw = w[:, None] * f[None, :]
    z = jnp.concatenate([t[:, None], jnp.cos(fw), -jnp.sin(fw)], axis=-1)
    z = jnp.pad(z, ((0, 0), (0, LANES - HY_EMB)))
    deltas = jnp.abs(jnp.linspace(math.log(HY_TARGET) / HY_SLOW_PCT,
                                  math.log(HY_TARGET) / HY_FAST_PCT, d, dtype=F32))
    return z, t[:, None], deltas[None, :]


def _dft_tables(seq):
    n = 2 * seq
    n2 = DFT_N2
    n1 = n // n2

    def cs(phase_int, mod):
        ang = (-2.0 * math.pi / mod) * (phase_int % mod).astype(F32)
        return jnp.cos(ang), jnp.sin(ang)

    k1 = jnp.arange(n1)[:, None]
    m1 = jnp.arange(n1 // 2)[None, :]
    fr, fi = cs(k1 * m1, n1)
    base = jnp.stack([fr, fi], axis=-1)
    eye = jnp.eye(DFT_ROWS, dtype=F32)
    rows = DFT_ROWS
    g1 = jnp.einsum('knp,rs->krpns', base, eye).reshape(n1 * rows * 2, (n1 // 2) * rows).astype(BF16)
    g2 = (jnp.einsum('knp,rs->nrksp', base, eye).reshape((n1 // 2) * rows, n1 * rows * 2) / n).astype(BF16)
    kk = (jnp.arange(n1)[:, None, None] + n1 * jnp.arange(n2)[None, :, None])
    nn = jnp.arange(n2)[None, None, :]
    hr, hi = cs(kk * nn, n)
    hmat = jnp.concatenate([jnp.stack([hr, -hi], axis=-1).reshape(n1, n2, 2 * n2),
                            jnp.stack([hi, hr], axis=-1).reshape(n1, n2, 2 * n2)], axis=1).astype(BF16)
    gr, gi = jnp.swapaxes(hr, 1, 2), -jnp.swapaxes(hi, 1, 2)
    gmat = jnp.stack([jnp.concatenate([gr, -gi], axis=2),
                      jnp.concatenate([gi, gr], axis=2)], axis=2).reshape(n1, 2 * n2, 2 * n2).astype(BF16)
    return g1, g2, hmat, gmat


def _deinterleave(nheads):
    hd = ATTN_HEAD_DIM
    one = np.concatenate([np.arange(0, hd, 2), np.arange(1, hd, 2)])
    return np.concatenate([h * hd + one for h in range(nheads)])


def kernel(x_prompt, x_sample, mem_prompt, mem_sample, norm_mix, norm_xa, norm_mem, norm_ffn, xa_wq, xa_wk, xa_wv, xa_wo, ffn_w_in, ffn_conv_w, ffn_conv_b, ffn_w_out, mix_w_in, mix_w_out, ssd_conv_w, ssd_conv_b, ssd_a_log, ssd_dt_bias, ssd_d, ssd_norm, attn_q_norm, attn_k_norm, hy_w_in, hy_conv_w, hy_conv_b, hy_f_w1, hy_f_b1, hy_f_w2, hy_f_b2, hy_f_w3, hy_f_b3, hy_f_freq, hy_f_w_out, hy_skip, hy_w_out, final_norm):
    nbp, seq, d = x_prompt.shape
    nbs = x_sample.shape[0]
    assert x_sample.shape[1] == seq
    nb = nbp + nbs
    T = nb * seq
    depth = norm_mix.shape[0]
    n_mem = mem_prompt.shape[1]
    d_ff = ffn_w_out.shape[1]

    x = jnp.concatenate([x_prompt, x_sample], axis=0).reshape(T, d)
    mem = jnp.concatenate([mem_prompt, mem_sample], axis=0).reshape(nb * n_mem, d)

    d_ssd = d
    nheads = d_ssd // SSD_HEAD_DIM
    gn = SSD_GROUPS * SSD_STATE
    conv_ch = d_ssd + 2 * gn
    n_att = d // ATTN_HEAD_DIM
    d_kv = ATTN_KV_HEADS * ATTN_HEAD_DIM
    o1 = d_ssd
    o2 = o1 + conv_ch
    o3 = o2 + 2 * nheads
    o4 = o3 + d
    o5 = o4 + d_kv
    qcol = o1
    vcol = o1 + d + d_kv
    cos, sin = _rope_tables(seq)
    perm_q = _deinterleave(n_att)
    perm_k = _deinterleave(ATTN_KV_HEADS)
    perm_h = _deinterleave(1)

    n2 = DFT_N2
    n1 = 2 * seq // n2
    hz, t_col, deltas = _hyena_features(seq, d)
    g1, g2, hmat, gmat = _dft_tables(seq)

    for i in range(depth):
        if i % 2 == 0:
            e = i // 2
            w = mix_w_in[e].astype(BF16)
            w_main = jnp.concatenate([w[:, :o1], w[:, o3:o4][:, perm_q], w[:, o4:o5][:, perm_k], w[:, o5:]],
                                     axis=1)
            w_dt = jnp.pad(w[:, o2:o3], ((0, 0), (0, LANES - 2 * nheads)))
            proj = normmm(x, norm_mix[i], w_main, tm=512, tn=w_main.shape[1])
            dtraw = normmm(x, norm_mix[i], w_dt, out_dtype=F32, tn=LANES)
            (xbc,) = normmm_conv(x, norm_mix[i], w[:, o1:o2], ssd_conv_w[e], ssd_conv_b[e],
                                 [0], conv_ch, _epi_silu, 1, seq)
            pad_row = lambda a: jnp.pad(a.reshape(1, -1).astype(F32), ((0, 0), (0, LANES - 2 * nheads)))
            y_ssd = ssd_scan(xbc, dtraw, pad_row(ssd_dt_bias[e]), pad_row(ssd_a_log[e]),
                             jnp.repeat(ssd_d[e].astype(F32), SSD_HEAD_DIM)[None, :],
                             proj, ssd_norm[e].reshape(1, -1).astype(F32), nb, seq)
            scale = ATTN_HEAD_DIM ** -0.5 * math.log2(math.e)
            gains = jnp.concatenate([jnp.tile(attn_q_norm[e][perm_h][None, :] * scale, (n_att, 1)),
                                     jnp.tile(attn_k_norm[e][perm_h][None, :], (ATTN_KV_HEADS, 1))],
                                    axis=0)[:, None, :].astype(F32)
            qk = qk_prep(proj, qcol, n_att, ATTN_KV_HEADS, gains, cos, sin, seq)
            y_att = flash_attention(qk, proj, vcol, n_att, nb, seq)
            x = mm_res(jnp.concatenate([y_ssd, y_att], axis=1), mix_w_out[e].astype(BF16), x, tm=512, tn=d)
        else:
            o = i // 2
            x0, wv = normmm_conv(x, norm_mix[i], hy_w_in[o].astype(BF16), hy_conv_w[o], hy_conv_b[o],
                                 [0, d, 2 * d], d, _epi_hyena, 2, seq, out_dtype=F32)
            w1 = jnp.pad(hy_f_w1[o], ((0, LANES - HY_EMB), (0, 0)))
            hfb, sums = hyena_filter(hz, t_col, deltas, w1, hy_f_b1[o], hy_f_w2[o], hy_f_b2[o],
                                     hy_f_w3[o], hy_f_b3[o], hy_f_freq[o], hy_f_w_out[o])
            a_f = dft_stage1(g1, hfb.reshape(1, n1 // 2, n2, 2 * d))
            kspec = filter_spectrum(a_f, hmat, sums, d)
            a_u = dft_stage1(g1, wv.reshape(nb, n1 // 2, n2, d))
            zz = dft_mid(a_u, hmat, gmat, kspec)
            yh = dft_out(g2, zz, x0.reshape(nb, n1 // 2, n2, d), wv.reshape(nb, n1 // 2, n2, d), hy_skip[o])
            x = mm_res(yh.reshape(T, d), hy_w_out[o].astype(BF16), x, tm=512, tn=d)
        q = normmm(x, norm_xa[i], xa_wq[i].astype(BF16), tn=d)
        kv = normmm(mem, norm_mem[i], jnp.concatenate([xa_wk[i].astype(BF16), xa_wv[i].astype(BF16)], axis=1))
        x = mm_res(xattn(q, kv, nb, seq), xa_wo[i].astype(BF16), x, tm=512, tn=d)
        (act,) = normmm_conv(x, norm_ffn[i], ffn_w_in[i].astype(BF16), ffn_conv_w[i], ffn_conv_b[i],
                             [0, d_ff], d_ff, _epi_glu, 1, seq)
        x = mm_res(act, ffn_w_out[i].astype(BF16), x, tm=512, tn=d)

    y_prompt = rmsnorm(x, final_norm, 0, nbp * seq).reshape(nbp, seq, d)
    y_sample = rmsnorm(x, final_norm, nbp * seq, nbs * seq).reshape(nbs, seq, d)
    return (y_prompt, y_sample)
```

```python
import functools
import math

import numpy as np
import jax
import jax.numpy as jnp
from jax import lax
from jax.experimental import pallas as pl
from jax.experimental.pallas import tpu as pltpu

F32 = jnp.float32
BF16 = jnp.bfloat16
EPS = 1e-6

GRID_W = 64
XA_HEADS = 4
SSD_HEAD_DIM = 64
SSD_GROUPS = 4
SSD_STATE = 128
SSD_CHUNK = 128
ATTN_HEAD_DIM = 128
ATTN_KV_HEADS = 4
ROPE_THETA = 10000.0
HY_EMB = 33
HY_BANDS = (HY_EMB - 1) // 2
HY_TARGET = 1e-2
HY_FAST_PCT = 0.3
HY_SLOW_PCT = 1.5

LANES = 128
DFT_N2 = 128
VMEM_LIMIT = 56 * 1024 * 1024
NEG_BIG = -1e30


def _cparams(*sem):
    return pltpu.CompilerParams(dimension_semantics=sem, vmem_limit_bytes=VMEM_LIMIT)


def _tile(dim, pref):
    t = min(dim, pref)
    while dim % t:
        t //= 2
    return t


def _split3(x):
    hi = x.astype(BF16)
    r1 = x - hi.astype(F32)
    mid = r1.astype(BF16)
    lo = (r1 - mid.astype(F32)).astype(BF16)
    return hi, mid, lo


def _dot(a, b):
    return jnp.dot(a, b, preferred_element_type=F32)


def _dot_exact_rhs(x, e):
    hi, mid, lo = _split3(x)
    return _dot(hi, e) + _dot(mid, e) + _dot(lo, e)


def _dot_exact_lhs(e, x):
    hi, mid, lo = _split3(x)
    return _dot(e, hi) + _dot(e, mid) + _dot(e, lo)


def _dot_f32(a, b):
    ah, am, _ = _split3(a)
    bh, bm, _ = _split3(b)
    return _dot(ah, bh) + _dot(ah, bm) + _dot(am, bh)


def _silu(x):
    return x * (1.0 / (1.0 + jnp.exp(-x)))


def _normmm_kernel(x_ref, g_ref, w_ref, o_ref, xn_ref):
    @pl.when(pl.program_id(1) == 0)
    def _():
        x = x_ref[...].astype(F32)
        ms = jnp.mean(x * x, axis=-1, keepdims=True)
        xn_ref[...] = (x * lax.rsqrt(ms + EPS) * g_ref[...]).astype(BF16)

    o_ref[...] = _dot(xn_ref[...], w_ref[...]).astype(o_ref.dtype)


def normmm(x, g, w, out_dtype=BF16, tm=1024, tn=1024):
    M, K = x.shape
    N = w.shape[1]
    tm = _tile(M, tm)
    tn = _tile(N, tn)
    return pl.pallas_call(
        _normmm_kernel,
        grid=(M // tm, N // tn),
        in_specs=[pl.BlockSpec((tm, K), lambda i, j: (i, 0)),
                  pl.BlockSpec((1, K), lambda i, j: (0, 0)),
                  pl.BlockSpec((K, tn), lambda i, j: (0, j),
                               pipeline_mode=pl.Buffered(1) if tn == N else None)],
        out_specs=pl.BlockSpec((tm, tn), lambda i, j: (i, j)),
        out_shape=jax.ShapeDtypeStruct((M, N), out_dtype),
        scratch_shapes=[pltpu.VMEM((tm, K), BF16)],
        compiler_params=_cparams("parallel", "arbitrary"),
        name="normmm",
    )(x, g.reshape(1, K).astype(F32), w)


def _mmres_kernel(a_ref, w_ref, r_ref, o_ref):
    o_ref[...] = r_ref[...] + _dot(a_ref[...].astype(BF16), w_ref[...])


def mm_res(a, w, res, tm=1024, tn=None):
    M, K = a.shape
    N = w.shape[1]
    tm = _tile(M, tm)
    tn = _tile(N, tn or (1024 if K <= 2048 else 512))
    w_mode = pl.Buffered(1) if tn == N else None
    return pl.pallas_call(
        _mmres_kernel,
        grid=(M // tm, N // tn),
        in_specs=[pl.BlockSpec((tm, K), lambda i, j: (i, 0)),
                  pl.BlockSpec((K, tn), lambda i, j: (0, j), pipeline_mode=w_mode),
                  pl.BlockSpec((tm, tn), lambda i, j: (i, j))],
        out_specs=pl.BlockSpec((tm, tn), lambda i, j: (i, j)),
        out_shape=jax.ShapeDtypeStruct((M, N), F32),
        compiler_params=_cparams("parallel", "arbitrary"),
        name="mm_res",
    )(a, w, res)


def _rmsnorm_kernel(x_ref, g_ref, o_ref):
    x = x_ref[...]
    ms = jnp.mean(x * x, axis=-1, keepdims=True)
    o_ref[...] = x * lax.rsqrt(ms + EPS) * g_ref[...]


def rmsnorm(x, g, row0, rows, tm=1024):
    K = x.shape[1]
    tm = _tile(math.gcd(row0, rows) if row0 else rows, tm)
    return pl.pallas_call(
        _rmsnorm_kernel,
        grid=(rows // tm,),
        in_specs=[pl.BlockSpec((tm, K), lambda i: (i + row0 // tm, 0)),
                  pl.BlockSpec((1, K), lambda i: (0, 0))],
        out_specs=pl.BlockSpec((tm, K), lambda i: (i, 0)),
        out_shape=jax.ShapeDtypeStruct((rows, K), F32),
        compiler_params=_cparams("parallel"),
        name="final_norm",
    )(x, g.reshape(1, K).astype(F32))


CONV_HALO = 16
CONV_ROWS = 64


def _normmm_conv_kernel(*refs, nseg, width, tm, seq, epilogue, nout):
    xm_ref, xp_ref, xn_ref, g_ref = refs[:4]
    segs = [refs[4 + 3 * s:7 + 3 * s] for s in range(nseg)]
    outs = refs[4 + 3 * nseg:4 + 3 * nseg + nout]
    hn_ref = refs[4 + 3 * nseg + nout]
    exts = refs[5 + 3 * nseg + nout:]
    h = CONV_HALO
    half = width // 2
    row0 = pl.program_id(0) * tm
    at_start = (row0 % seq) == 0
    at_end = ((row0 + tm) % seq) == 0

    @pl.when(pl.program_id(1) == 0)
    def _():
        def nrm(x):
            ms = jnp.mean(x * x, axis=-1, keepdims=True)
            return (x * lax.rsqrt(ms + EPS) * g_ref[...]).astype(BF16)
        hn_ref[0:h, :] = nrm(xp_ref[...])
        hn_ref[h:h + tm, :] = nrm(xm_ref[...])
        hn_ref[h + tm:h + tm + h, :] = nrm(xn_ref[...])

    tn = exts[0].shape[1]
    for (w_ref, _, _), ext in zip(segs, exts):
        ext[...] = _dot(hn_ref[...], w_ref[...])
        ext[0:h, :] = jnp.where(at_start, 0.0, ext[0:h, :])
        ext[h + tm:h + tm + h, :] = jnp.where(at_end, 0.0, ext[h + tm:h + tm + h, :])

    for rb in range(0, tm, CONV_ROWS):
        for lc in range(0, tn, LANES):
            ls = slice(lc, lc + LANES)
            vals = []
            for (_, cw_ref, cb_ref), ext in zip(segs, exts):
                acc = None
                for k in range(width):
                    term = ext[h - half + k + rb:h - half + k + rb + CONV_ROWS, ls] * cw_ref[k:k + 1, ls]
                    acc = term if acc is None else acc + term
                vals.append(acc + cb_ref[:, ls])
            for o, r in zip(outs, epilogue(*vals)):
                o[rb:rb + CONV_ROWS, ls] = r.astype(o.dtype)


def normmm_conv(x, g, w, conv_w, conv_b, seg_cols, width_cols, epilogue, nout, seq, out_dtype=BF16,
                tm=1024, tn=512):
    T, K = x.shape
    width = conv_w.shape[0]
    nseg = len(seg_cols)
    tm = _tile(seq, tm)
    tn = _tile(width_cols, tn)
    assert tm % CONV_ROWS == 0 and tn % LANES == 0
    h = CONV_HALO
    nrb = T // h
    cb = conv_b.reshape(1, -1).astype(F32)
    cw = conv_w.astype(F32)
    in_specs = [pl.BlockSpec((tm, K), lambda i, j: (i, 0)),
                pl.BlockSpec((h, K), lambda i, j: (jnp.maximum(i * (tm // h) - 1, 0), 0)),
                pl.BlockSpec((h, K), lambda i, j: (jnp.minimum((i + 1) * (tm // h), nrb - 1), 0)),
                pl.BlockSpec((1, K), lambda i, j: (0, 0))]
    args = [x, x, x, g.reshape(1, K).astype(F32)]
    for c0 in seg_cols:
        off = c0 // tn
        in_specs += [pl.BlockSpec((K, tn), lambda i, j, off=off: (0, j + off)),
                     pl.BlockSpec((width, tn), lambda i, j, off=off: (0, j + off)),
                     pl.BlockSpec((1, tn), lambda i, j, off=off: (0, j + off))]
        args += [w, cw, cb]
    kern = functools.partial(_normmm_conv_kernel, nseg=nseg, width=width, tm=tm, seq=seq,
                             epilogue=epilogue, nout=nout)
    return pl.pallas_call(
        kern,
        grid=(T // tm, width_cols // tn),
        in_specs=in_specs,
        out_specs=[pl.BlockSpec((tm, tn), lambda i, j: (i, j)) for _ in range(nout)],
        out_shape=[jax.ShapeDtypeStruct((T, width_cols), out_dtype) for _ in range(nout)],
        scratch_shapes=[pltpu.VMEM((tm + 2 * h, K), BF16)]
        + [pltpu.VMEM((tm + 2 * h, tn), F32) for _ in range(nseg)],
        compiler_params=_cparams("parallel", "arbitrary"),
        name="normmm_conv",
    )(*args)


def _epi_silu(c):
    return (_silu(c),)


def _epi_glu(g, up):
    return (_silu(g) * up,)


def _epi_hyena(x0, x1, v):
    return (x0, v * x1)


def _softplus(x):
    return jnp.maximum(x, 0.0) + jnp.log(1.0 + jnp.exp(-jnp.abs(x)))


def _ssd_kernel(*refs, rev, nheads):
    if rev:
        (xs_ref, b_ref, c_ref, dt_ref, bias_ref, alog_ref, e_ref,
         yf_ref, z_ref, gain_ref, o_ref, s_ref, y_ref) = refs
    else:
        (xs_ref, b_ref, c_ref, dt_ref, bias_ref, alog_ref, e_ref,
         dskip_ref, o_ref, s_ref) = refs
        y_ref = o_ref
    Q = SSD_CHUNK
    P = SSD_HEAD_DIM
    hpg = nheads // SSD_GROUPS
    gw = hpg * P
    hoff = nheads if rev else 0

    @pl.when(pl.program_id(1) == 0)
    def _():
        s_ref[...] = jnp.zeros_like(s_ref)

    row = lax.broadcasted_iota(jnp.int32, (Q, Q), 0)
    col = lax.broadcasted_iota(jnp.int32, (Q, Q), 1)
    mask = (col >= row) if rev else (col <= row)
    tri = jnp.where(mask, 1.0, 0.0).astype(BF16)

    dtv = _softplus(dt_ref[...] + bias_ref[...])
    a_row = -jnp.exp(alog_ref[...])
    la = dtv * a_row
    cs = _dot_exact_lhs(tri, la)
    tot = cs[0:1, :] if rev else cs[Q - 1:Q, :]
    cs_t = cs.T
    dt_t = dtv.T
    e = e_ref[...]
    ecs_hi, ecs_lo, _ = _split3(jnp.exp(cs))
    carry_in = _dot(ecs_hi, e) + _dot(ecs_lo, e)
    to_end = _dot((jnp.exp(tot - cs) * dtv).astype(BF16), e)
    dec = _dot_exact_rhs(jnp.broadcast_to(jnp.exp(tot), (8, LANES)), e)[0:1, :]

    xs = xs_ref[...]
    x_state = (xs.astype(F32) * to_end).astype(BF16)
    lane = lax.broadcasted_iota(jnp.int32, (Q, LANES), 1)
    low = lane < P

    for g in range(SSD_GROUPS):
        bg = b_ref[:, g * SSD_STATE:(g + 1) * SSD_STATE]
        cg = c_ref[:, g * SSD_STATE:(g + 1) * SSD_STATE]
        cb = lax.dot_general(cg, bg, (((1,), (1,)), ((), ())), preferred_element_type=F32)
        s_old = s_ref[g]
        y_off = _dot(cg, s_old.astype(BF16)) * carry_in[:, g * gw:(g + 1) * gw]
        s_ref[g] = s_old * dec[:, g * gw:(g + 1) * gw] + lax.dot_general(
            bg, x_state[:, g * gw:(g + 1) * gw], (((0,), (0,)), ((), ())), preferred_element_type=F32)
        for j in range(hpg // 2):
            ws = []
            for hh in range(2):
                hc = hoff + g * hpg + 2 * j + hh
                diff = cs[:, hc:hc + 1] - cs_t[hc:hc + 1, :]
                decay = jnp.exp(jnp.where(mask, diff, NEG_BIG))
                ws.append((cb * decay * dt_t[hc:hc + 1, :]).astype(BF16))
            c0 = g * gw + 2 * j * P
            xp = xs[:, c0:c0 + LANES]
            rhs = jnp.concatenate([jnp.where(low, xp, jnp.zeros_like(xp)),
                                   jnp.where(low, jnp.zeros_like(xp), xp)], axis=0)
            y = _dot(jnp.concatenate(ws, axis=1), rhs) + y_off[:, 2 * j * P:2 * j * P + LANES]
            if not rev:
                y = y + xp.astype(F32) * dskip_ref[:, c0:c0 + LANES]
            y_ref[:, c0:c0 + LANES] = y

    if rev:
        y = y_ref[...] + yf_ref[...]
        gated = y * _silu(z_ref[...].astype(F32))
        ms = jnp.mean(gated * gated, axis=-1, keepdims=True)
        o_ref[...] = (gated * lax.rsqrt(ms + EPS) * gain_ref[...]).astype(o_ref.dtype)


def ssd_scan(xbc, dtraw, bias_row, alog_row, d_row, z_src, gain_row, nb, seq):
    T = xbc.shape[0]
    Q = SSD_CHUNK
    nc = seq // Q
    gn = SSD_GROUPS * SSD_STATE
    hp = xbc.shape[1] - 2 * gn
    nheads = hp // SSD_HEAD_DIM
    hpg = nheads // SSD_GROUPS
    gw = hpg * SSD_HEAD_DIM
    assert hp % gn == 0 and 2 * nheads <= LANES

    def e_mat(off):
        r = np.arange(LANES)[:, None]
        c = np.arange(hp)[None, :]
        return jnp.asarray((r == off + c // SSD_HEAD_DIM).astype(np.float32), dtype=BF16)

    def specs(rev):
        def blk(c):
            return (nc - 1 - c) if rev else c
        return [
            pl.BlockSpec((Q, hp), lambda b, c: (b * nc + blk(c), 0)),
            pl.BlockSpec((Q, gn), lambda b, c: (b * nc + blk(c), hp // gn)),
            pl.BlockSpec((Q, gn), lambda b, c: (b * nc + blk(c), hp // gn + 1)),
            pl.BlockSpec((Q, LANES), lambda b, c: (b * nc + blk(c), 0)),
            pl.BlockSpec((1, LANES), lambda b, c: (0, 0)),
            pl.BlockSpec((1, LANES), lambda b, c: (0, 0)),
            pl.BlockSpec((LANES, hp), lambda b, c: (0, 0)),
        ], (lambda b, c: (b * nc + blk(c), 0))

    in_f, omap_f = specs(False)
    yf = pl.pallas_call(
        functools.partial(_ssd_kernel, rev=False, nheads=nheads),
        grid=(nb, nc),
        in_specs=in_f + [pl.BlockSpec((1, hp), lambda b, c: (0, 0))],
        out_specs=pl.BlockSpec((Q, hp), omap_f),
        out_shape=jax.ShapeDtypeStruct((T, hp), F32),
        scratch_shapes=[pltpu.VMEM((SSD_GROUPS, SSD_STATE, gw), F32)],
        compiler_params=_cparams("parallel", "arbitrary"),
        name="ssd_fwd",
    )(xbc, xbc, xbc, dtraw, bias_row, alog_row, e_mat(0), d_row)
    in_b, omap_b = specs(True)
    return pl.pallas_call(
        functools.partial(_ssd_kernel, rev=True, nheads=nheads),
        grid=(nb, nc),
        in_specs=in_b + [pl.BlockSpec((Q, hp), omap_b),
                         pl.BlockSpec((Q, hp), omap_b),
                         pl.BlockSpec((1, hp), lambda b, c: (0, 0))],
        out_specs=pl.BlockSpec((Q, hp), omap_b),
        out_shape=jax.ShapeDtypeStruct((T, hp), BF16),
        scratch_shapes=[pltpu.VMEM((SSD_GROUPS, SSD_STATE, gw), F32),
                        pltpu.VMEM((Q, hp), F32)],
        compiler_params=_cparams("parallel", "arbitrary"),
        name="ssd_bwd",
    )(xbc, xbc, xbc, dtraw, bias_row, alog_row, e_mat(nheads), yf, z_src, gain_row)


def _qkprep_kernel(q_ref, k_ref, g_ref, cos_ref, sin_ref, o_ref, *, nq, nk):
    hd = ATTN_HEAD_DIM
    cos = cos_ref[...]
    sin = sin_ref[...]
    for h in range(nq + nk):
        src, c0 = (q_ref, h * hd) if h < nq else (k_ref, (h - nq) * hd)
        x = src[:, c0:c0 + hd].astype(F32)
        ms = jnp.mean(x * x, axis=-1, keepdims=True)
        xn = x * lax.rsqrt(ms + EPS) * g_ref[h]
        o_ref[:, h * hd:(h + 1) * hd] = (xn * cos + pltpu.roll(xn, hd // 2, 1) * sin).astype(o_ref.dtype)


def qk_prep(proj, qcol, nq, nk, gains, cos, sin, seq, tq=1024):
    T = proj.shape[0]
    hd = ATTN_HEAD_DIM
    tq = _tile(seq, tq)
    spt = seq // tq
    assert qcol % (nq * hd) == 0 and (qcol + nq * hd) % (nk * hd) == 0
    return pl.pallas_call(
        functools.partial(_qkprep_kernel, nq=nq, nk=nk),
        grid=(T // tq,),
        in_specs=[pl.BlockSpec((tq, nq * hd), lambda i: (i, qcol // (nq * hd))),
                  pl.BlockSpec((tq, nk * hd), lambda i: (i, (qcol + nq * hd) // (nk * hd))),
                  pl.BlockSpec((nq + nk, 1, hd), lambda i: (0, 0, 0)),
                  pl.BlockSpec((tq, hd), lambda i: (i % spt, 0)),
                  pl.BlockSpec((tq, hd), lambda i: (i % spt, 0))],
        out_specs=pl.BlockSpec((tq, (nq + nk) * hd), lambda i: (i, 0)),
        out_shape=jax.ShapeDtypeStruct((T, (nq + nk) * hd), BF16),
        compiler_params=_cparams("parallel"),
        name="qk_prep",
    )(proj, proj, gains, cos, sin)


FLASH_ROW_BLOCK = 32


def _flash_kernel(q_ref, k_ref, v_ref, o_ref, qs_ref, va_ref, s0_ref, s1_ref, p_ref, acc_ref,
                  m_ref, al_ref, *, tk, group):
    hd = ATTN_HEAD_DIM
    tq = q_ref.shape[0]
    rows = group * tq
    seq = k_ref.shape[0]
    nk = seq // tk

    @pl.when(pl.program_id(2) == 0)
    def _():
        va_ref[:, 0:hd] = v_ref[...]
        va_ref[:, hd:2 * hd] = jnp.ones((seq, hd), BF16)

    for g in range(group):
        qs_ref[g * tq:(g + 1) * tq, :] = q_ref[:, g * hd:(g + 1) * hd]
    m_ref[...] = jnp.full(m_ref.shape, NEG_BIG, F32)
    acc_ref[...] = jnp.zeros(acc_ref.shape, F32)
    nlc = tk // LANES

    def scores(t, s_ref):
        k0 = pl.multiple_of(t * tk, tk)
        s_ref[...] = lax.dot_general(qs_ref[...], k_ref[pl.ds(k0, tk), :], (((1,), (1,)), ((), ())),
                                     preferred_element_type=F32)

    def update(t, s_ref):
        for r0 in range(0, rows, FLASH_ROW_BLOCK):
            rs = slice(r0, r0 + FLASH_ROW_BLOCK)
            ch = [s_ref[rs, c * LANES:(c + 1) * LANES] for c in range(nlc)]
            mx = ch[0]
            for c in range(1, nlc):
                mx = jnp.maximum(mx, ch[c])
            m_old = m_ref[rs, :]
            m_new = jnp.maximum(m_old, jnp.max(mx, axis=-1, keepdims=True))
            m_ref[rs, :] = m_new
            al_ref[rs, :] = jnp.exp2(m_old - m_new)
            for c in range(nlc):
                p_ref[rs, c * LANES:(c + 1) * LANES] = jnp.exp2(ch[c] - m_new).astype(BF16)
        k0 = pl.multiple_of(t * tk, tk)
        pv = _dot(p_ref[...], va_ref[pl.ds(k0, tk), :])
        al = al_ref[...]
        acc_ref[...] = acc_ref[...] * jnp.concatenate([al, al], axis=1) + pv

    scores(0, s0_ref)

    def body(t2, carry):
        scores(2 * t2 + 1, s1_ref)
        update(2 * t2, s0_ref)
        scores(2 * t2 + 2, s0_ref)
        update(2 * t2 + 1, s1_ref)
        return carry

    lax.fori_loop(0, nk // 2 - 1, body, 0)
    scores(nk - 1, s1_ref)
    update(nk - 2, s0_ref)
    update(nk - 1, s1_ref)
    o = acc_ref[:, 0:hd] / acc_ref[:, hd:2 * hd]
    for g in range(group):
        o_ref[:, g * hd:(g + 1) * hd] = o[g * tq:(g + 1) * tq, :].astype(o_ref.dtype)


def flash_attention(qk, v_src, v_col0, nq_heads, nb, seq, tq=256, tk=1024):
    T = qk.shape[0]
    hd = ATTN_HEAD_DIM
    nkv = ATTN_KV_HEADS
    group = nq_heads // nkv
    tq = _tile(seq, tq)
    tk = _tile(seq // 2, tk)
    nqt = seq // tq
    return pl.pallas_call(
        functools.partial(_flash_kernel, tk=tk, group=group),
        grid=(nb, nkv, nqt),
        in_specs=[pl.BlockSpec((tq, group * hd), lambda b, h, i: (b * nqt + i, h)),
                  pl.BlockSpec((seq, hd), lambda b, h, i: (b, nq_heads + h)),
                  pl.BlockSpec((seq, hd), lambda b, h, i: (b, v_col0 // hd + h))],
        out_specs=pl.BlockSpec((tq, group * hd), lambda b, h, i: (b * nqt + i, h)),
        out_shape=jax.ShapeDtypeStruct((T, nq_heads * hd), BF16),
        scratch_shapes=[pltpu.VMEM((group * tq, hd), BF16),
                        pltpu.VMEM((seq, 2 * hd), BF16),
                        pltpu.VMEM((group * tq, tk), F32),
                        pltpu.VMEM((group * tq, tk), F32),
                        pltpu.VMEM((group * tq, tk), BF16),
                        pltpu.VMEM((group * tq, 2 * hd), F32),
                        pltpu.VMEM((group * tq, LANES), F32),
                        pltpu.VMEM((group * tq, LANES), F32)],
        compiler_params=_cparams("arbitrary", "arbitrary", "arbitrary"),
        name="flash_attn",
    )(qk, qk, v_src)


def _xattn_kernel(q_ref, kv_ref, o_ref, *, heads):
    d = q_ref.shape[1]
    hd = d // heads
    scale = hd ** -0.5
    for h in range(heads):
        q = q_ref[:, h * hd:(h + 1) * hd]
        k = kv_ref[:, h * hd:(h + 1) * hd]
        v = kv_ref[:, d + h * hd:d + (h + 1) * hd]
        s = lax.dot_general(q, k, (((1,), (1,)), ((), ())), preferred_element_type=F32) * scale
        p = jnp.exp(s - jnp.max(s, axis=-1, keepdims=True))
        l = jnp.sum(p, axis=-1, keepdims=True)
        o = _dot(p.astype(BF16), v) / l
        o_ref[:, h * hd:(h + 1) * hd] = o.astype(o_ref.dtype)


def xattn(q, kv, nb, seq, tq=1024):
    T, d = q.shape
    n_mem = kv.shape[0] // nb
    tq = _tile(seq, tq)
    nqt = seq // tq
    return pl.pallas_call(
        functools.partial(_xattn_kernel, heads=XA_HEADS),
        grid=(nb, nqt),
        in_specs=[pl.BlockSpec((tq, d), lambda b, i: (b * nqt + i, 0)),
                  pl.BlockSpec((n_mem, 2 * d), lambda b, i: (b, 0))],
        out_specs=pl.BlockSpec((tq, d), lambda b, i: (b * nqt + i, 0)),
        out_shape=jax.ShapeDtypeStruct((T, d), BF16),
        compiler_params=_cparams("parallel", "arbitrary"),
        name="xattn",
    )(q, kv)


def _hyfilter_kernel(z_ref, t_ref, dl_ref, w1_ref, b1_ref, w2_ref, b2_ref, w3_ref, b3_ref,
                     fr_ref, wo_ref, h_ref, sum_ref, *, tl, d):
    i = pl.program_id(0)
    fr = fr_ref[...]
    h = jnp.sin(fr * (_dot_f32(z_ref[...], w1_ref[...]) + b1_ref[...]))
    h = jnp.sin(fr * (_dot_f32(h, w2_ref[...]) + b2_ref[...]))
    h = jnp.sin(fr * (_dot_f32(h, w3_ref[...]) + b3_ref[...]))
    window = jnp.exp(-t_ref[...] * dl_ref[...])
    rows = lax.broadcasted_iota(jnp.int32, (tl, 1), 0) + i * tl

    @pl.when(i == 0)
    def _():
        sum_ref[...] = jnp.zeros_like(sum_ref)

    for part in range(2):
        hp = _dot_f32(h, wo_ref[:, part * d:(part + 1) * d]) * window
        if part == 1:
            hp = jnp.where(rows == 0, 0.0, hp)
        h_ref[:, part * d:(part + 1) * d] = hp.astype(h_ref.dtype)
        sum_ref[:, part * d:(part + 1) * d] += jnp.sum(jnp.abs(hp), axis=0, keepdims=True)


def hyena_filter(z, t_col, deltas, w1, b1, w2, b2, w3, b3, freq, w_out, tl=256):
    L = z.shape[0]
    d2 = w_out.shape[1]
    d = d2 // 2
    fw = w2.shape[0]
    tl = _tile(L, tl)
    full = lambda a: pl.BlockSpec(a.shape, lambda i: (0,) * a.ndim)
    ops = [w1, b1.reshape(1, fw), w2, b2.reshape(1, fw), w3, b3.reshape(1, fw), freq.reshape(1, fw), w_out]
    return pl.pallas_call(
        functools.partial(_hyfilter_kernel, tl=tl, d=d),
        grid=(L // tl,),
        in_specs=[pl.BlockSpec((tl, z.shape[1]), lambda i: (i, 0)),
                  pl.BlockSpec((tl, 1), lambda i: (i, 0)),
                  full(deltas)] + [full(a) for a in ops],
        out_specs=[pl.BlockSpec((tl, d2), lambda i: (i, 0)),
                   pl.BlockSpec((1, d2), lambda i: (0, 0))],
        out_shape=[jax.ShapeDtypeStruct((L, d2), F32),
                   jax.ShapeDtypeStruct((1, d2), F32)],
        compiler_params=_cparams("arbitrary"),
        name="hyena_filter",
    )(z, t_col, deltas, *ops)


DFT_ROWS = 8


def _dft1_kernel(g_ref, u_ref, o_ref):
    kh, rt, tc = u_ref.shape
    u = u_ref[...].reshape(kh * rt, tc).astype(BF16)
    a = _dot(g_ref[...], u).astype(BF16)
    o_ref[...] = pltpu.bitcast(a, jnp.uint32).reshape(o_ref.shape)


def dft_stage1(g1, u4, tc=2048):
    nb, kh, n2, c = u4.shape
    rt = DFT_ROWS
    n1 = g1.shape[0] // (2 * rt)
    tc = _tile(c, tc)
    return pl.pallas_call(
        _dft1_kernel,
        grid=(nb, n2 // rt, c // tc),
        in_specs=[pl.BlockSpec(g1.shape, lambda b, i, j: (0, 0)),
                  pl.BlockSpec((None, kh, rt, tc), lambda b, i, j: (b, 0, i, j))],
        out_specs=pl.BlockSpec((None, n1, rt, tc), lambda b, i, j: (b, 0, i, j)),
        out_shape=jax.ShapeDtypeStruct((nb, n1, n2, c), jnp.uint32),
        compiler_params=_cparams("parallel", "parallel", "arbitrary"),
        name="dft_stage1",
    )(g1, u4)


def _unpack_complex(ref):
    return pltpu.bitcast(ref[...], BF16)


def _spectrum_kernel(af_ref, ab_ref, h_ref, sum_ref, o_ref, *, d_cols):
    n2 = DFT_N2
    hm = h_ref[...]
    xf = _dot(hm, _unpack_complex(af_ref))
    xb = _dot(hm, _unpack_complex(ab_ref))
    inv = 1.0 / (sum_ref[:, 0:d_cols] + sum_ref[:, d_cols:2 * d_cols])
    o_ref[0] = (xf[:n2] + xb[:n2]) * inv
    o_ref[1] = (xf[n2:] - xb[n2:]) * inv


def filter_spectrum(a4, hmat, sums, d):
    n1 = a4.shape[1]
    n2 = DFT_N2
    return pl.pallas_call(
        functools.partial(_spectrum_kernel, d_cols=d),
        grid=(n1,),
        in_specs=[pl.BlockSpec((None, None, n2, d), lambda k: (0, k, 0, 0)),
                  pl.BlockSpec((None, None, n2, d), lambda k: (0, k, 0, 1)),
                  pl.BlockSpec((None, 2 * n2, 2 * n2), lambda k: (k, 0, 0)),
                  pl.BlockSpec((1, 2 * d), lambda k: (0, 0))],
        out_specs=pl.BlockSpec((2, None, n2, d), lambda k: (0, k, 0, 0)),
        out_shape=jax.ShapeDtypeStruct((2, n1, n2, d), F32),
        compiler_params=_cparams("arbitrary"),
        name="filter_spectrum",
    )(a4, a4, hmat, sums)


def _dftmid_kernel(a_ref, h_ref, g_ref, k_ref, o_ref):
    n2 = DFT_N2
    kr, ki = k_ref[0], k_ref[1]
    for b in range(a_ref.shape[0]):
        x = _dot(h_ref[...], pltpu.bitcast(a_ref[b], BF16))
        xr, xi = x[:n2], x[n2:]
        y = jnp.concatenate([xr * kr - xi * ki, xr * ki + xi * kr], axis=0).astype(BF16)
        zz = _dot(g_ref[...], y).astype(BF16)
        o_ref[b] = pltpu.bitcast(zz, jnp.uint32)


def dft_mid(a4, hmat, gmat, kspec):
    nb, n1, n2, c = a4.shape
    return pl.pallas_call(
        _dftmid_kernel,
        grid=(n1,),
        in_specs=[pl.BlockSpec((nb, None, n2, c), lambda k: (0, k, 0, 0)),
                  pl.BlockSpec((None, 2 * n2, 2 * n2), lambda k: (k, 0, 0)),
                  pl.BlockSpec((None, 2 * n2, 2 * n2), lambda k: (k, 0, 0)),
                  pl.BlockSpec((2, None, n2, c), lambda k: (0, k, 0, 0))],
        out_specs=pl.BlockSpec((nb, None, n2, c), lambda k: (0, k, 0, 0)),
        out_shape=jax.ShapeDtypeStruct(a4.shape, jnp.uint32),
        compiler_params=_cparams("parallel"),
        name="dft_mid",
    )(a4, hmat, gmat, kspec)


def _dftout_kernel(g_ref, z_ref, x0_ref, w_ref, skip_ref, o_ref):
    n1, rt, tc = z_ref.shape
    z = pltpu.bitcast(z_ref[...].reshape(n1 * rt, tc), BF16)
    y = _dot(g_ref[...], z).reshape(o_ref.shape)
    o_ref[...] = x0_ref[...] * (y + w_ref[...] * skip_ref[...])


def dft_out(g2, z4, x0, w, skip, tc=2048):
    nb, n1, n2, c = z4.shape
    rt = DFT_ROWS
    kh = g2.shape[0] // rt
    tc = _tile(c, tc)
    tok = pl.BlockSpec((None, kh, rt, tc), lambda b, i, j: (b, 0, i, j))
    return pl.pallas_call(
        _dftout_kernel,
        grid=(nb, n2 // rt, c // tc),
        in_specs=[pl.BlockSpec(g2.shape, lambda b, i, j: (0, 0)),
                  pl.BlockSpec((None, n1, rt, tc), lambda b, i, j: (b, 0, i, j)),
                  tok, tok,
                  pl.BlockSpec((1, 1, tc), lambda b, i, j: (0, 0, j))],
        out_specs=tok,
        out_shape=jax.ShapeDtypeStruct((nb, kh, n2, c), F32),
        compiler_params=_cparams("parallel", "parallel", "arbitrary"),
        name="dft_out",
    )(g2, z4, x0, w, skip.reshape(1, 1, c).astype(F32))


def _rope_tables(seq):
    hd = ATTN_HEAD_DIM
    axis_dim = hd // 2
    t = jnp.arange(seq)
    row = (t // GRID_W).astype(F32)
    col = (t % GRID_W).astype(F32)
    inv_freq = ROPE_THETA ** (-jnp.arange(0, axis_dim, 2, dtype=F32) / axis_dim)
    ang = jnp.concatenate([row[:, None] * inv_freq, col[:, None] * inv_freq], axis=-1)
    c, s = jnp.cos(ang), jnp.sin(ang)
    return jnp.concatenate([c, c], axis=-1), jnp.concatenate([-s, s], axis=-1)


def _hyena_features(seq, d):
    t = jnp.linspace(0.0, 1.0, seq, dtype=F32)
    w = 2.0 * math.pi * jnp.arange(seq, dtype=F32) / seq
    f = jnp.linspace(1e-4, HY_BANDS - 1, HY_BANDS, dtype=F32)
    fw = w[:, None] * f[None, :]
    z = jnp.concatenate([t[:, None], jnp.cos(fw), -jnp.sin(fw)], axis=-1)
    z = jnp.pad(z, ((0, 0), (0, LANES - HY_EMB)))
    deltas = jnp.abs(jnp.linspace(math.log(HY_TARGET) / HY_SLOW_PCT,
                                  math.log(HY_TARGET) / HY_FAST_PCT, d, dtype=F32))
    return z, t[:, None], deltas[None, :]


def _dft_tables(seq):
    n = 2 * seq
    n2 = DFT_N2
    n1 = n // n2

    def cs(phase_int, mod):
        ang = (-2.0 * math.pi / mod) * (phase_int % mod).astype(F32)
        return jnp.cos(ang), jnp.sin(ang)

    k1 = jnp.arange(n1)[:, None]
    m1 = jnp.arange(n1 // 2)[None, :]
    fr, fi = cs(k1 * m1, n1)
    base = jnp.stack([fr, fi], axis=-1)
    eye = jnp.eye(DFT_ROWS, dtype=F32)
    rows = DFT_ROWS
    g1 = jnp.einsum('knp,rs->krpns', base, eye).reshape(n1 * rows * 2, (n1 // 2) * rows).astype(BF16)
    g2 = (jnp.einsum('knp,rs->nrksp', base, eye).reshape((n1 // 2) * rows, n1 * rows * 2) / n).astype(BF16)
    kk = (jnp.arange(n1)[:, None, None] + n1 * jnp.arange(n2)[None, :, None])
    nn = jnp.arange(n2)[None, None, :]
    hr, hi = cs(kk * nn, n)
    hmat = jnp.concatenate([jnp.stack([hr, -hi], axis=-1).reshape(n1, n2, 2 * n2),
                            jnp.stack([hi, hr], axis=-1).reshape(n1, n2, 2 * n2)], axis=1).astype(BF16)
    gr, gi = jnp.swapaxes(hr, 1, 2), -jnp.swapaxes(hi, 1, 2)
    gmat = jnp.stack([jnp.concatenate([gr, -gi], axis=2),
                      jnp.concatenate([gi, gr], axis=2)], axis=2).reshape(n1, 2 * n2, 2 * n2).astype(BF16)
    return g1, g2, hmat, gmat


def _deinterleave(nheads):
    hd = ATTN_HEAD_DIM
    one = np.concatenate([np.arange(0, hd, 2), np.arange(1, hd, 2)])
    return np.concatenate([h * hd + one for h in range(nheads)])


def kernel(x_prompt, x_sample, mem_prompt, mem_sample, norm_mix, norm_xa, norm_mem, norm_ffn, xa_wq, xa_wk, xa_wv, xa_wo, ffn_w_in, ffn_conv_w, ffn_conv_b, ffn_w_out, mix_w_in, mix_w_out, ssd_conv_w, ssd_conv_b, ssd_a_log, ssd_dt_bias, ssd_d, ssd_norm, attn_q_norm, attn_k_norm, hy_w_in, hy_conv_w, hy_conv_b, hy_f_w1, hy_f_b1, hy_f_w2, hy_f_b2, hy_f_w3, hy_f_b3, hy_f_freq, hy_f_w_out, hy_skip, hy_w_out, final_norm):
    nbp, seq, d = x_prompt.shape
    nbs = x_sample.shape[0]
    assert x_sample.shape[1] == seq
    nb = nbp + nbs
    T = nb * seq
    depth = norm_mix.shape[0]
    n_mem = mem_prompt.shape[1]
    d_ff = ffn_w_out.shape[1]

    x = jnp.concatenate([x_prompt, x_sample], axis=0).reshape(T, d)
    mem = jnp.concatenate([mem_prompt, mem_sample], axis=0).reshape(nb * n_mem, d)

    d_ssd = d
    nheads = d_ssd // SSD_HEAD_DIM
    gn = SSD_GROUPS * SSD_STATE
    conv_ch = d_ssd + 2 * gn
    n_att = d // ATTN_HEAD_DIM
    d_kv = ATTN_KV_HEADS * ATTN_HEAD_DIM
    o1 = d_ssd
    o2 = o1 + conv_ch
    o3 = o2 + 2 * nheads
    o4 = o3 + d
    o5 = o4 + d_kv
    qcol = o1
    vcol = o1 + d + d_kv
    cos, sin = _rope_tables(seq)
    perm_q = _deinterleave(n_att)
    perm_k = _deinterleave(ATTN_KV_HEADS)
    perm_h = _deinterleave(1)

    n2 = DFT_N2
    n1 = 2 * seq // n2
    hz, t_col, deltas = _hyena_features(seq, d)
    g1, g2, hmat, gmat = _dft_tables(seq)

    for i in range(depth):
        if i % 2 == 0:
            e = i // 2
            w = mix_w_in[e].astype(BF16)
            w_main = jnp.concatenate([w[:, :o1], w[:, o3:o4][:, perm_q], w[:, o4:o5][:, perm_k], w[:, o5:]],
                                     axis=1)
            w_dt = jnp.pad(w[:, o2:o3], ((0, 0), (0, LANES - 2 * nheads)))
            proj = normmm(x, norm_mix[i], w_main, tm=512, tn=w_main.shape[1])
            dtraw = normmm(x, norm_mix[i], w_dt, out_dtype=F32, tn=LANES)
            (xbc,) = normmm_conv(x, norm_mix[i], w[:, o1:o2], ssd_conv_w[e], ssd_conv_b[e],
                                 [0], conv_ch, _epi_silu, 1, seq)
            pad_row = lambda a: jnp.pad(a.reshape(1, -1).astype(F32), ((0, 0), (0, LANES - 2 * nheads)))
            y_ssd = ssd_scan(xbc, dtraw, pad_row(ssd_dt_bias[e]), pad_row(ssd_a_log[e]),
                             jnp.repeat(ssd_d[e].astype(F32), SSD_HEAD_DIM)[None, :],
                             proj, ssd_norm[e].reshape(1, -1).astype(F32), nb, seq)
            scale = ATTN_HEAD_DIM ** -0.5 * math.log2(math.e)
            gains = jnp.concatenate([jnp.tile(attn_q_norm[e][perm_h][None, :] * scale, (n_att, 1)),
                                     jnp.tile(attn_k_norm[e][perm_h][None, :], (ATTN_KV_HEADS, 1))],
                                    axis=0)[:, None, :].astype(F32)
            qk = qk_prep(proj, qcol, n_att, ATTN_KV_HEADS, gains, cos, sin, seq)
            y_att = flash_attention(qk, proj, vcol, n_att, nb, seq)
            x = mm_res(jnp.concatenate([y_ssd, y_att], axis=1), mix_w_out[e].astype(BF16), x, tm=512, tn=d)
        else:
            o = i // 2
            x0, wv = normmm_conv(x, norm_mix[i], hy_w_in[o].astype(BF16), hy_conv_w[o], hy_conv_b[o],
                                 [0, d, 2 * d], d, _epi_hyena, 2, seq, out_dtype=F32)
            w1 = jnp.pad(hy_f_w1[o], ((0, LANES - HY_EMB), (0, 0)))
            hfb, sums = hyena_filter(hz, t_col, deltas, w1, hy_f_b1[o], hy_f_w2[o], hy_f_b2[o],
                                     hy_f_w3[o], hy_f_b3[o], hy_f_freq[o], hy_f_w_out[o])
            a_f = dft_stage1(g1, hfb.reshape(1, n1 // 2, n2, 2 * d))
            kspec = filter_spectrum(a_f, hmat, sums, d)
            a_u = dft_stage1(g1, wv.reshape(nb, n1 // 2, n2, d))
            zz = dft_mid(a_u, hmat, gmat, kspec)
            yh = dft_out(g2, zz, x0.reshape(nb, n1 // 2, n2, d), wv.reshape(nb, n1 // 2, n2, d), hy_skip[o])
            x = mm_res(yh.reshape(T, d), hy_w_out[o].astype(BF16), x, tm=512, tn=d)
        q = normmm(x, norm_xa[i], xa_wq[i].astype(BF16), tn=d)
        kv = normmm(mem, norm_mem[i], jnp.concatenate([xa_wk[i].astype(BF16), xa_wv[i].astype(BF16)], axis=1))
        x = mm_res(xattn(q, kv, nb, seq), xa_wo[i].astype(BF16), x, tm=512, tn=d)
        (act,) = normmm_conv(x, norm_ffn[i], ffn_w_in[i].astype(BF16), ffn_conv_w[i], ffn_conv_b[i],
                             [0, d_ff], d_ff, _epi_glu, 1, seq)
        x = mm_res(act, ffn_w_out[i].astype(BF16), x, tm=512, tn=d)

    y_prompt = rmsnorm(x, final_norm, 0, nbp * seq).reshape(nbp, seq, d)
    y_sample = rmsnorm(x, final_norm, nbp * seq, nbs * seq).reshape(nbs, seq, d)
    return (y_prompt, y_sample)
```
